```python
import jax, jax.numpy as jnp
from jax import lax
import numpy as np

D_MODEL = 1024
BATCH = 4
SEQ = 4096
DEPTH = 2

CHUNK = 64
D_MIX = D_MODEL
LRU_WIDTH = D_MIX // 4
LRU_HEADS = 4
LRU_HEAD_DIM = LRU_WIDTH // LRU_HEADS
CONV_WIDTH = 4
LRU_C = 8.0
FOX_WIDTH = D_MIX // 2
FOX_HEAD_DIM = 64
FOX_HEADS = FOX_WIDTH // FOX_HEAD_DIM
Q_BLOCK = 128
RWKV_WIDTH = D_MIX - LRU_WIDTH - FOX_WIDTH
RWKV_HEAD_DIM = 64
RWKV_HEADS = RWKV_WIDTH // RWKV_HEAD_DIM
DECAY_RANK = 32
AAA_RANK = 32
GATE_RANK = 64
N_GROUPS = 4
EXPERTS_PER_GROUP = 4
N_EXPERTS = N_GROUPS * EXPERTS_PER_GROUP
TOP_K = 2
D_EXPERT = 256
NORM_EPS = 1e-6
GN_EPS = 64e-5
LRU_COLS = 2 * LRU_WIDTH
FOX_COLS = 3 * FOX_WIDTH + FOX_HEADS
RWKV_COLS = 3 * RWKV_WIDTH + DECAY_RANK + AAA_RANK + GATE_RANK
D_IN = LRU_COLS + FOX_COLS + RWKV_COLS

kernel_name = 'hybrid_rglru_fox_rwkv7_hiermoe'


def rms_norm(x, g, eps=NORM_EPS):
    xf = x.astype(jnp.float32)
    y = xf * lax.rsqrt(jnp.mean(xf * xf, axis=-1, keepdims=True) + eps)
    return (y * g.astype(jnp.float32)).astype(x.dtype)


def rglru_group(x_in, gate_in, conv_w, conv_b, wa, ba, wx, bx, lam, norm_g):
    f32 = jnp.float32
    B, S, W = x_in.shape
    xp = jnp.pad(x_in, ((0, 0), (CONV_WIDTH - 1, 0), (0, 0)))
    xc = conv_b + sum(xp[:, j:j + S] * conv_w[j] for j in range(CONV_WIDTH))
    xh = xc.reshape(B, S, LRU_HEADS, LRU_HEAD_DIM)
    r = jax.nn.sigmoid((jnp.einsum('bshi,hij->bshj', xh, wa).reshape(B, S, W) + ba).astype(f32))
    i = jax.nn.sigmoid((jnp.einsum('bshi,hij->bshj', xh, wx).reshape(B, S, W) + bx).astype(f32))
    log_a = -LRU_C * r * jax.nn.softplus(-lam.astype(f32))
    a = jnp.exp(log_a)
    b = jnp.sqrt(-jnp.expm1(2.0 * log_a)) * (i * xc.astype(f32))

    def combine(left, right):
        a1, b1 = left
        a2, b2 = right
        return a1 * a2, a2 * b1 + b2

    _, h = lax.associative_scan(combine, (a, b), axis=1)
    y = jax.nn.gelu(gate_in.astype(f32)) * h
    return rms_norm(y.astype(x_in.dtype), norm_g)


def fox_group(q, k, v, f_logit, fb, qn_g, kn_g, norm_g):
    f32 = jnp.float32
    B, S, _ = q.shape
    H, Dh = FOX_HEADS, FOX_HEAD_DIM
    qh = rms_norm(q.reshape(B, S, H, Dh), qn_g).transpose(0, 2, 1, 3)
    kh = rms_norm(k.reshape(B, S, H, Dh), kn_g).transpose(0, 2, 1, 3)
    vh = v.reshape(B, S, H, Dh).transpose(0, 2, 1, 3)
    log_f = jax.nn.log_sigmoid((f_logit + fb).astype(f32))
    c = jnp.cumsum(log_f, axis=1).transpose(0, 2, 1)
    nb = S // Q_BLOCK
    q_blocks = qh.reshape(B, H, nb, Q_BLOCK, Dh).transpose(2, 0, 1, 3, 4)
    c_blocks = c.reshape(B, H, nb, Q_BLOCK).transpose(2, 0, 1, 3)
    starts = jnp.arange(nb, dtype=jnp.int32) * Q_BLOCK
    k_pos = jnp.arange(S, dtype=jnp.int32)
    scale = FOX_HEAD_DIM ** -0.5

    def block(args):
        qb, cb, s0 = args
        s = jnp.einsum('bhqd,bhkd->bhqk', qb, kh).astype(f32) * scale
        s = s + cb[..., None] - c[:, :, None, :]
        q_pos = s0 + jnp.arange(Q_BLOCK, dtype=jnp.int32)
        mask = k_pos[None, :] <= q_pos[:, None]
        p = jax.nn.softmax(jnp.where(mask, s, -jnp.inf), axis=-1)
        return jnp.einsum('bhqk,bhkd->bhqd', p.astype(vh.dtype), vh)

    o = lax.map(block, (q_blocks, c_blocks, starts))
    o = o.transpose(1, 0, 3, 2, 4).reshape(B, S, FOX_WIDTH)
    return rms_norm(o.astype(q.dtype), norm_g)


def rwkv7_group(p, mu, w0, w2, a0, a2, g2, k_k, k_a, r_k, ln_g, ln_b):
    f32 = jnp.float32
    B, S, _ = p.shape
    H, N = RWKV_HEADS, RWKV_HEAD_DIM
    prev = jnp.pad(p, ((0, 0), (1, 0), (0, 0)))[:, :-1]
    p = p + (prev - p) * mu
    o1 = RWKV_WIDTH
    o2 = 2 * RWKV_WIDTH
    o3 = 3 * RWKV_WIDTH
    o4 = o3 + DECAY_RANK
    o5 = o4 + AAA_RANK
    r, k, v, wd, ad, gd = jnp.split(p, [o1, o2, o3, o4, o5], axis=-1)
    r = r.astype(f32)
    k = k.astype(f32)
    v = v.astype(f32)
    w = -jax.nn.softplus(-(w0 + jnp.tanh(wd) @ w2).astype(f32)) - 0.5
    decay = jnp.exp(-jnp.exp(w))
    a = jax.nn.sigmoid((a0 + ad @ a2).astype(f32))
    g = (jax.nn.sigmoid(gd) @ g2).astype(f32)

    def heads(t):
        return t.reshape(B, S, H, N)

    kk = heads(k * k_k)
    kk = kk / jnp.maximum(jnp.sqrt(jnp.sum(kk * kk, axis=-1, keepdims=True)), 1e-12)
    k = k * (1.0 + (a - 1.0) * k_a)
    rh, wh, kh, vh, ah = heads(r), heads(decay), heads(k), heads(v), heads(a)
    n_chunks = S // CHUNK

    def to_chunks(t):
        return t.transpose(1, 0, 2, 3).reshape(n_chunks, CHUNK, B, H, N)

    xs = (to_chunks(rh), to_chunks(wh), to_chunks(kh), to_chunks(vh), to_chunks(kk), to_chunks(kk * ah))

    def frame_step(state, inp):
        r_t, w_t, k_t, v_t, kk_t, b_t = inp
        sa = jnp.einsum('bhvk,bhk->bhv', state, -kk_t)
        state = (state * w_t[:, :, None, :] + sa[..., None] * b_t[:, :, None, :]
                 + v_t[..., None] * k_t[:, :, None, :])
        return state, jnp.einsum('bhvk,bhk->bhv', state, r_t)

    def chunk_step(state, chunk):
        return lax.scan(frame_step, state, chunk)

    state0 = jnp.zeros((B, H, N, N), f32)
    _, o = lax.scan(chunk_step, state0, xs)
    o = o.reshape(S, B, H, N).transpose(1, 0, 2, 3)
    mean = jnp.mean(o, axis=-1, keepdims=True)
    var = jnp.mean(jnp.square(o - mean), axis=-1, keepdims=True)
    o = ((o - mean) * lax.rsqrt(var + GN_EPS)).reshape(B, S, RWKV_WIDTH) * ln_g + ln_b
    bonus = jnp.sum(rh * kh * r_k, axis=-1, keepdims=True) * vh
    return (o + bonus.reshape(B, S, RWKV_WIDTH)) * g


def hier_moe(x, wg, bg, we, be, w_gate, w_up, w_down):
    f32 = jnp.float32
    B, S, D = x.shape
    t = x.reshape(B * S, D)
    g_probs = jax.nn.softmax((t @ wg).astype(f32) + bg, axis=-1)
    g_p, g_idx = lax.top_k(g_probs, 1)
    e_logits = ((t @ we).astype(f32) + be).reshape(-1, N_GROUPS, EXPERTS_PER_GROUP)
    sel = jax.nn.one_hot(g_idx[:, 0], N_GROUPS, dtype=f32)
    e_logits = jnp.einsum('tg,tge->te', sel, e_logits)
    e_p, e_idx = lax.top_k(jax.nn.softmax(e_logits, axis=-1), TOP_K)
    e_p = e_p / jnp.sum(e_p, axis=-1, keepdims=True)
    weights = g_p * e_p
    ids = g_idx * EXPERTS_PER_GROUP + e_idx
    combine = jnp.sum(jax.nn.one_hot(ids, N_EXPERTS, dtype=f32) * weights[..., None], axis=1)
    hid = jax.nn.silu(jnp.einsum('td,edf->tef', t, w_gate)) * jnp.einsum('td,edf->tef', t, w_up)
    hid = hid * combine[..., None].astype(hid.dtype)
    y = jnp.einsum('tef,efd->td', hid, w_down)
    return y.reshape(B, S, D).astype(x.dtype)


def setup_inputs(seed: int = 0) -> dict:
    key = jax.random.key(seed)
    ks = iter(jax.random.split(key, 40))
    L = DEPTH

    def nrm(shape, scale):
        return jax.random.normal(next(ks), shape, jnp.float32) * scale

    def uni(shape, lo, hi):
        return jax.random.uniform(next(ks), shape, jnp.float32, lo, hi)

    x = nrm((BATCH, SEQ, D_MODEL), 1.0)
    norm1_g = 1.0 + nrm((L, D_MODEL), 0.02)
    w_in = nrm((L, D_MODEL, D_IN), D_MODEL ** -0.5)
    conv_w = nrm((L, CONV_WIDTH, LRU_WIDTH), CONV_WIDTH ** -0.5)
    conv_b = nrm((L, LRU_WIDTH), 0.01)
    lru_wa = nrm((L, LRU_HEADS, LRU_HEAD_DIM, LRU_HEAD_DIM), LRU_HEAD_DIM ** -0.5)
    lru_ba = nrm((L, LRU_WIDTH), 0.01)
    lru_wx = nrm((L, LRU_HEADS, LRU_HEAD_DIM, LRU_HEAD_DIM), LRU_HEAD_DIM ** -0.5)
    lru_bx = nrm((L, LRU_WIDTH), 0.01)
    a_pow = uni((L, LRU_WIDTH), 0.9, 0.999)
    s = a_pow ** (1.0 / LRU_C)
    lru_lambda = jnp.log(s) - jnp.log1p(-s)
    lru_norm_g = 1.0 + nrm((L, LRU_WIDTH), 0.02)
    fox_fb = uni((L, FOX_HEADS), 1.0, 5.0)
    fox_qnorm_g = 1.0 + nrm((L, FOX_HEAD_DIM), 0.02)
    fox_knorm_g = 1.0 + nrm((L, FOX_HEAD_DIM), 0.02)
    fox_norm_g = 1.0 + nrm((L, FOX_WIDTH), 0.02)
    rwkv_mu = uni((L, RWKV_COLS), 0.0, 1.0)
    rwkv_w0 = uni((L, RWKV_WIDTH), -4.0, 0.0)
    rwkv_w2 = nrm((L, DECAY_RANK, RWKV_WIDTH), 0.1 * DECAY_RANK ** -0.5)
    rwkv_a0 = nrm((L, RWKV_WIDTH), 0.1)
    rwkv_a2 = nrm((L, AAA_RANK, RWKV_WIDTH), 0.5 * AAA_RANK ** -0.5)
    rwkv_g2 = nrm((L, GATE_RANK, RWKV_WIDTH), GATE_RANK ** -0.5)
    rwkv_kk = 0.85 + nrm((L, RWKV_WIDTH), 0.05)
    rwkv_ka = 1.0 + nrm((L, RWKV_WIDTH), 0.05)
    rwkv_rk = nrm((L, RWKV_HEADS, RWKV_HEAD_DIM), 0.1)
    rwkv_ln_g = 1.0 + nrm((L, RWKV_WIDTH), 0.02)
    rwkv_ln_b = nrm((L, RWKV_WIDTH), 0.01)
    w_out = nrm((L, D_MIX, D_MODEL), D_MIX ** -0.5)
    norm2_g = 1.0 + nrm((L, D_MODEL), 0.02)
    router_gw = nrm((L, D_MODEL, N_GROUPS), D_MODEL ** -0.5)
    router_gb = nrm((L, N_GROUPS), 0.01)
    router_ew = nrm((L, D_MODEL, N_EXPERTS), D_MODEL ** -0.5)
    router_eb = nrm((L, N_EXPERTS), 0.01)
    exp_w_gate = nrm((L, N_EXPERTS, D_MODEL, D_EXPERT), D_MODEL ** -0.5)
    exp_w_up = nrm((L, N_EXPERTS, D_MODEL, D_EXPERT), D_MODEL ** -0.5)
    exp_w_down = nrm((L, N_EXPERTS, D_EXPERT, D_MODEL), D_EXPERT ** -0.5)
    return {'x': x, 'norm1_g': norm1_g, 'w_in': w_in, 'conv_w': conv_w, 'conv_b': conv_b,
            'lru_wa': lru_wa, 'lru_ba': lru_ba, 'lru_wx': lru_wx, 'lru_bx': lru_bx,
            'lru_lambda': lru_lambda, 'lru_norm_g': lru_norm_g, 'fox_fb': fox_fb,
            'fox_qnorm_g': fox_qnorm_g, 'fox_knorm_g': fox_knorm_g, 'fox_norm_g': fox_norm_g,
            'rwkv_mu': rwkv_mu, 'rwkv_w0': rwkv_w0, 'rwkv_w2': rwkv_w2, 'rwkv_a0': rwkv_a0,
            'rwkv_a2': rwkv_a2, 'rwkv_g2': rwkv_g2, 'rwkv_kk': rwkv_kk, 'rwkv_ka': rwkv_ka,
            'rwkv_rk': rwkv_rk, 'rwkv_ln_g': rwkv_ln_g, 'rwkv_ln_b': rwkv_ln_b, 'w_out': w_out,
            'norm2_g': norm2_g, 'router_gw': router_gw, 'router_gb': router_gb,
            'router_ew': router_ew, 'router_eb': router_eb, 'exp_w_gate': exp_w_gate,
            'exp_w_up': exp_w_up, 'exp_w_down': exp_w_down}


def reference(x, norm1_g, w_in, conv_w, conv_b, lru_wa, lru_ba, lru_wx, lru_bx, lru_lambda,
              lru_norm_g, fox_fb, fox_qnorm_g, fox_knorm_g, fox_norm_g, rwkv_mu, rwkv_w0,
              rwkv_w2, rwkv_a0, rwkv_a2, rwkv_g2, rwkv_kk, rwkv_ka, rwkv_rk, rwkv_ln_g,
              rwkv_ln_b, w_out, norm2_g, router_gw, router_gb, router_ew, router_eb,
              exp_w_gate, exp_w_up, exp_w_down):
    splits = [LRU_WIDTH, LRU_COLS, LRU_COLS + FOX_WIDTH, LRU_COLS + 2 * FOX_WIDTH,
              LRU_COLS + 3 * FOX_WIDTH, LRU_COLS + FOX_COLS]
    for l in range(DEPTH):
        h = rms_norm(x, norm1_g[l])
        proj = h @ w_in[l]
        xa, ga, q, k, v, fl, pc = jnp.split(proj, splits, axis=-1)
        ya = rglru_group(xa, ga, conv_w[l], conv_b[l], lru_wa[l], lru_ba[l], lru_wx[l],
                         lru_bx[l], lru_lambda[l], lru_norm_g[l])
        yb = fox_group(q, k, v, fl, fox_fb[l], fox_qnorm_g[l], fox_knorm_g[l], fox_norm_g[l])
        yc = rwkv7_group(pc, rwkv_mu[l], rwkv_w0[l], rwkv_w2[l], rwkv_a0[l], rwkv_a2[l],
                         rwkv_g2[l], rwkv_kk[l], rwkv_ka[l], rwkv_rk[l], rwkv_ln_g[l], rwkv_ln_b[l])
        y = jnp.concatenate([ya, yb.astype(x.dtype), yc.astype(x.dtype)], axis=-1)
        x = x + (y @ w_out[l]).astype(x.dtype)
        x = x + hier_moe(rms_norm(x, norm2_g[l]), router_gw[l], router_gb[l], router_ew[l],
                         router_eb[l], exp_w_gate[l], exp_w_up[l], exp_w_down[l])
    return x
```

```python
import functools

import jax
import jax.numpy as jnp
from jax import lax
from jax.experimental import pallas as pl
from jax.experimental.pallas import tpu as pltpu

F32 = jnp.float32
BF16 = jnp.bfloat16

NORM_EPS = 1e-6
GN_EPS = 64e-5
LRU_C = 8.0
HEAD_DIM = 64
HEAD_SHIFT = 6
LANES = 128
SUBLANES = 8
CHUNK = 64
CONV_WIDTH = 4
N_GROUPS = 4
EXPERTS_PER_GROUP = 4
N_EXPERTS = N_GROUPS * EXPERTS_PER_GROUP
NEG_BIG = -1e30
VMEM_LIMIT = 56 * 1024 * 1024


def _dot(a, b):
    return jnp.dot(a, b, preferred_element_type=F32)


def _dot_nt(a, b):
    return lax.dot_general(a, b, (((1,), (1,)), ((), ())), preferred_element_type=F32)


def _dot_tn(a, b):
    return lax.dot_general(a, b, (((0,), (0,)), ((), ())), preferred_element_type=F32)


def _split_dot_right(x, ones, parts):
    acc = None
    rem = x
    for _ in range(parts):
        hi = rem.astype(BF16)
        t = _dot(hi, ones)
        acc = t if acc is None else acc + t
        rem = rem - hi.astype(F32)
    return acc


def _split_dot_left(ones, x, parts):
    acc = None
    rem = x
    for _ in range(parts):
        hi = rem.astype(BF16)
        t = _dot(ones, hi)
        acc = t if acc is None else acc + t
        rem = rem - hi.astype(F32)
    return acc


def _head_ones(n):
    r = lax.broadcasted_iota(jnp.int32, (n, n), 0) >> HEAD_SHIFT
    c = lax.broadcasted_iota(jnp.int32, (n, n), 1) >> HEAD_SHIFT
    return jnp.where(r == c, 1.0, 0.0).astype(BF16)


def _softplus(z):
    return jnp.maximum(z, 0.0) + jnp.log(1.0 + jnp.exp(-jnp.abs(z)))


def _rms(x, eps=NORM_EPS):
    return x * lax.rsqrt(jnp.mean(x * x, axis=-1, keepdims=True) + eps)


def _resident(shape):
    zeros = (0,) * len(shape)
    return pl.BlockSpec(shape, lambda *_: zeros, pipeline_mode=pl.Buffered(1))


def _in_proj_kernel(x_ref, g_ref, w_ref, qg_ref, kg_ref, fb_ref,
                    lru_ref, q_ref, k_ref, v_ref, p_ref, c_ref, carry_ref,
                    *, tiles_per_seq, seg):
    i = pl.program_id(0)

    @pl.when(i % tiles_per_seq == 0)
    def _():
        carry_ref[...] = jnp.zeros_like(carry_ref)

    hb = (_rms(x_ref[...]) * g_ref[...]).astype(BF16)
    lru_ref[...] = _dot(hb, w_ref[:, seg[0]:seg[1]])

    fox = seg[2] - seg[1]
    ones = _head_ones(fox)
    inv_d = 1.0 / HEAD_DIM
    q = _dot(hb, w_ref[:, seg[1]:seg[2]])
    q = q * lax.rsqrt(_split_dot_right(q * q, ones, 2) * inv_d + NORM_EPS) * qg_ref[...]
    q_ref[...] = q.astype(BF16)
    k = _dot(hb, w_ref[:, seg[2]:seg[3]])
    k = k * lax.rsqrt(_split_dot_right(k * k, ones, 2) * inv_d + NORM_EPS) * kg_ref[...]
    k_ref[...] = k.astype(BF16)
    v_ref[...] = _dot(hb, w_ref[:, seg[3]:seg[4]]).astype(BF16)
    p_ref[...] = _dot(hb, w_ref[:, seg[4]:seg[5]])

    fl = _dot(hb, w_ref[:, seg[5]:seg[6]]) + fb_ref[...]
    lf = -_softplus(-fl)
    tm = fl.shape[0]
    rr = lax.broadcasted_iota(jnp.int32, (tm, tm), 0)
    cc = lax.broadcasted_iota(jnp.int32, (tm, tm), 1)
    tri = jnp.where(cc <= rr, 1.0, 0.0).astype(BF16)
    c = _split_dot_left(tri, lf, 3) + carry_ref[0:1, :]
    c_ref[...] = c
    carry_ref[...] = jnp.broadcast_to(c[tm - 1:tm, :], carry_ref.shape)


def _in_proj(x2, g, w, qg, kg, fb, *, seq, tm, seg):
    T, D = x2.shape
    n_out = w.shape[1]
    widths = [seg[j + 1] - seg[j] for j in range(6)]
    row = lambda i: (i, 0)
    out_shape = (
        jax.ShapeDtypeStruct((T, widths[0]), F32),
        jax.ShapeDtypeStruct((T, widths[1]), BF16),
        jax.ShapeDtypeStruct((T, widths[2]), BF16),
        jax.ShapeDtypeStruct((T, widths[3]), BF16),
        jax.ShapeDtypeStruct((T, widths[4]), F32),
        jax.ShapeDtypeStruct((T, widths[5]), F32),
    )
    return pl.pallas_call(
        functools.partial(_in_proj_kernel, tiles_per_seq=seq // tm, seg=seg),
        out_shape=out_shape,
        grid=(T // tm,),
        in_specs=[
            pl.BlockSpec((tm, D), row),
            _resident((1, D)),
            _resident((D, n_out)),
            _resident((1, widths[1])),
            _resident((1, widths[2])),
            _resident((1, widths[5])),
        ],
        out_specs=tuple(pl.BlockSpec((tm, wd), row) for wd in widths),
        scratch_shapes=[pltpu.VMEM((SUBLANES, widths[5]), F32)],
        compiler_params=pltpu.CompilerParams(
            dimension_semantics=("arbitrary",), vmem_limit_bytes=VMEM_LIMIT),
        name="in_proj",
    )(x2, g, w, qg, kg, fb)


def _shift_rows(x, d, fill, row):
    return jnp.where(row >= d, pltpu.roll(x, d, axis=0), fill)


def _lru_kernel(u_ref, cw_ref, cb_ref, wg_ref, bg_ref, lam_ref, ng_ref,
                y_ref, buf_ref, h_ref):
    j = pl.program_id(1)
    tb = u_ref.shape[0]
    w = u_ref.shape[1] // 2
    pad = SUBLANES

    @pl.when(j == 0)
    def _():
        buf_ref[0:pad, :] = jnp.zeros((pad, w), F32)
        h_ref[...] = jnp.zeros_like(h_ref)

    xa = u_ref[:, 0:w]
    ga = u_ref[:, w:2 * w]
    buf_ref[pad:pad + tb, :] = xa
    xc = cb_ref[...] + cw_ref[CONV_WIDTH - 1:CONV_WIDTH, :] * xa
    for d in range(1, CONV_WIDTH):
        xc = xc + cw_ref[CONV_WIDTH - 1 - d:CONV_WIDTH - d, :] * buf_ref[pad - d:pad - d + tb, :]
    buf_ref[0:pad, :] = xa[tb - pad:tb, :]

    gates = _dot(xc.astype(BF16), wg_ref[...]) + bg_ref[...]
    r = jax.nn.sigmoid(gates[:, 0:w])
    i = jax.nn.sigmoid(gates[:, w:2 * w])
    log_a = (-LRU_C) * r * _softplus(-lam_ref[...])
    a = jnp.exp(log_a)
    b = jnp.sqrt(jnp.tanh(-log_a) * (1.0 + a * a)) * (i * xc)

    row = lax.broadcasted_iota(jnp.int32, (tb, w), 0)
    d = 1
    while d < tb:
        a_sh = _shift_rows(a, d, 1.0, row)
        b_sh = _shift_rows(b, d, 0.0, row)
        b = a * b_sh + b
        a = a * a_sh
        d *= 2
    h = b + a * h_ref[0:1, :]
    h_ref[...] = jnp.broadcast_to(h[tb - 1:tb, :], h_ref.shape)

    y = jax.nn.gelu(ga) * h
    y_ref[...] = (_rms(y) * ng_ref[...]).astype(BF16)


def _lru(u, cw, cb, wg, bg, lam, ng, *, batch, seq, tb):
    T, w2 = u.shape
    w = w2 // 2
    nb = seq // tb
    return pl.pallas_call(
        _lru_kernel,
        out_shape=jax.ShapeDtypeStruct((T, w), BF16),
        grid=(batch, nb),
        in_specs=[
            pl.BlockSpec((tb, w2), lambda b, j: (b * nb + j, 0)),
            _resident((CONV_WIDTH, w)),
            _resident((1, w)),
            _resident((w, w2)),
            _resident((1, w2)),
            _resident((1, w)),
            _resident((1, w)),
        ],
        out_specs=pl.BlockSpec((tb, w), lambda b, j: (b * nb + j, 0)),
        scratch_shapes=[pltpu.VMEM((tb + SUBLANES, w), F32), pltpu.VMEM((SUBLANES, w), F32)],
        compiler_params=pltpu.CompilerParams(
            dimension_semantics=("parallel", "arbitrary"), vmem_limit_bytes=VMEM_LIMIT),
        name="lru",
    )(u, cw, cb, wg, bg, lam, ng)


def _fox_kernel(q_ref, k_ref, v_ref, cq_ref, ck_ref, o_ref, *, tq, tk):
    hp = pl.program_id(1)
    i = pl.program_id(2)
    kpq = tq // tk
    lane = lax.broadcasted_iota(jnp.int32, (tq, LANES), 1)
    q = q_ref[...]
    c_tile = cq_ref[...]
    rows = lax.broadcasted_iota(jnp.int32, (tq, tk), 0) + i * tq
    cols = lax.broadcasted_iota(jnp.int32, (tq, tk), 1)
    outs = []
    for u in range(LANES // HEAD_DIM):
        head = hp * (LANES // HEAD_DIM) + u
        qm = jnp.where((lane >> HEAD_SHIFT) == u, q, jnp.zeros_like(q))
        cq = jnp.sum(jnp.where(lane == head, c_tile, 0.0), axis=-1, keepdims=True)

        def step(j, carry, masked, qm=qm, cq=cq, head=head):
            m, l, acc = carry
            k0 = pl.multiple_of(j * tk, tk)
            kb = k_ref[pl.ds(k0, tk), :]
            vb = v_ref[pl.ds(k0, tk), :]
            ck = ck_ref[0, pl.ds(head, 1), pl.ds(k0, tk)]
            s = _dot_nt(qm, kb) + (cq - ck)
            if masked:
                s = jnp.where(cols + k0 <= rows, s, NEG_BIG)
            m_new = jnp.maximum(m, jnp.max(s, axis=-1, keepdims=True))
            alpha = jnp.exp(m - m_new)
            p = jnp.exp(s - m_new)
            l = alpha * l + jnp.sum(p, axis=-1, keepdims=True)
            acc = alpha * acc + _dot(p.astype(BF16), vb)
            return m_new, l, acc

        carry = (jnp.full((tq, 1), NEG_BIG, F32), jnp.zeros((tq, 1), F32),
                 jnp.zeros((tq, LANES), F32))
        carry = lax.fori_loop(0, i * kpq, functools.partial(step, masked=False), carry)
        for dblk in range(kpq):
            carry = step(i * kpq + dblk, carry, True)
        _, l, acc = carry
        outs.append(acc / l)
    o = jnp.where((lane >> HEAD_SHIFT) == 0, outs[0], outs[1])
    o_ref[...] = o.astype(BF16)


def _fox(q, k, v, c, c_rows, *, batch, seq, tq, tk):
    T, wf = q.shape
    heads = c_rows.shape[1]
    nq = seq // tq
    pairs = wf // LANES
    return pl.pallas_call(
        functools.partial(_fox_kernel, tq=tq, tk=tk),
        out_shape=jax.ShapeDtypeStruct((T, wf), BF16),
        grid=(batch, pairs, nq),
        in_specs=[
            pl.BlockSpec((tq, LANES), lambda b, h, i: (b * nq + i, h)),
            pl.BlockSpec((seq, LANES), lambda b, h, i: (b, h)),
            pl.BlockSpec((seq, LANES), lambda b, h, i: (b, h)),
            pl.BlockSpec((tq, c.shape[1]), lambda b, h, i: (b * nq + i, 0)),
            pl.BlockSpec((1, heads, seq), lambda b, h, i: (b, 0, 0)),
        ],
        out_specs=pl.BlockSpec((tq, LANES), lambda b, h, i: (b * nq + i, h)),
        compiler_params=pltpu.CompilerParams(
            dimension_semantics=("parallel", "parallel", "arbitrary"),
            vmem_limit_bytes=VMEM_LIMIT),
        name="fox",
    )(q, k, v, c, c_rows)


def _stack_heads(x, lane_head, n_heads):
    zero = jnp.zeros_like(x)
    return jnp.concatenate([jnp.where(lane_head == h, x, zero) for h in range(n_heads)], axis=0)


def _rwkv_kernel(p_ref, mu_ref, w0_ref, w2_ref, a0_ref, a2_ref, g2_ref, kk_ref, ka_ref,
                 rk_ref, lng_ref, lnb_ref, y_ref, prev_ref, st_ref):
    j = pl.program_id(1)
    tb, cols = p_ref.shape
    w = y_ref.shape[1]
    n_heads = w // HEAD_DIM
    C = CHUNK
    chunk_shift = CHUNK.bit_length() - 1

    @pl.when(j == 0)
    def _():
        prev_ref[...] = jnp.zeros_like(prev_ref)
        st_ref[...] = jnp.zeros_like(st_ref)

    p = p_ref[...]
    row = lax.broadcasted_iota(jnp.int32, (tb, cols), 0)
    prev = jnp.where(row == 0, prev_ref[0:1, :], pltpu.roll(p, 1, axis=0))
    prev_ref[...] = jnp.broadcast_to(p[tb - 1:tb, :], prev_ref.shape)
    ps = p + (prev - p) * mu_ref[...]
    r = ps[:, 0:w]
    k = ps[:, w:2 * w]
    v = ps[:, 2 * w:3 * w]
    lr = ps[:, 3 * w:cols]

    wpre = w0_ref[...] + _dot(jnp.tanh(lr).astype(BF16), w2_ref[...])
    lw = -jnp.exp(-_softplus(-wpre) - 0.5)
    a = jax.nn.sigmoid(a0_ref[...] + _dot(lr.astype(BF16), a2_ref[...]))
    g = _dot(jax.nn.sigmoid(lr).astype(BF16), g2_ref[...])

    ones = _head_ones(w)
    kk = k * kk_ref[...]
    kk = kk / jnp.maximum(jnp.sqrt(_split_dot_right(kk * kk, ones, 2)), 1e-12)
    k2 = k * (1.0 + (a - 1.0) * ka_ref[...])
    b = kk * a
    bonus = _split_dot_right(r * k2 * rk_ref[...], ones, 2) * v

    rr = lax.broadcasted_iota(jnp.int32, (tb, tb), 0)
    cc = lax.broadcasted_iota(jnp.int32, (tb, tb), 1)
    same = (rr >> chunk_shift) == (cc >> chunk_shift)
    tri = jnp.where(same & (cc <= rr), 1.0, 0.0).astype(BF16)
    tot = jnp.where(same, 1.0, 0.0).astype(BF16)
    LW = _split_dot_left(tri, lw, 3)
    LT = _split_dot_left(tot, lw, 3)
    e_out = jnp.exp(-LW)
    e_end = jnp.exp(LT - LW)
    at = (-kk) * jnp.exp(LW - lw)
    rt = r * jnp.exp(LW)
    bt = b * e_out
    kt = k2 * e_out
    b_end = b * e_end
    k_end = k2 * e_end
    w_tot = jnp.exp(LT)

    lane_head = lax.broadcasted_iota(jnp.int32, (C, w), 1) >> HEAD_SHIFT
    t_idx = lax.broadcasted_iota(jnp.int32, (C, w), 0)
    j_idx = lax.broadcasted_iota(jnp.int32, (C, w), 1) & (HEAD_DIM - 1)
    strict = j_idx < t_idx
    incl = j_idx <= t_idx
    eye = jnp.where(j_idx == t_idx, 1.0, 0.0)
    sq_r = lax.broadcasted_iota(jnp.int32, (w, w), 0) >> HEAD_SHIFT
    sq_c = lax.broadcasted_iota(jnp.int32, (w, w), 1) >> HEAD_SHIFT
    block_diag = sq_r == sq_c
    stack = functools.partial(_stack_heads, lane_head=lane_head, n_heads=n_heads)

    outs = []
    for c in range(tb // C):
        sl = slice(c * C, (c + 1) * C)
        ar = jnp.concatenate([at[sl], rt[sl]], axis=0).astype(BF16)
        bk = jnp.concatenate([stack(bt[sl].astype(BF16)), stack(kt[sl].astype(BF16))], axis=0)
        A = _dot_nt(ar, bk)
        a_ab = jnp.where(strict, A[0:C, 0:w], 0.0)
        a_ak = jnp.where(strict, A[0:C, w:2 * w], 0.0)
        a_rb = jnp.where(incl, A[C:2 * C, 0:w], 0.0)
        a_rk = jnp.where(incl, A[C:2 * C, w:2 * w], 0.0)

        pw = a_ab
        tinv = eye + a_ab
        sz = 1
        while 2 * sz < C:
            pw_b = stack(pw.astype(BF16))
            pw = _dot(pw.astype(BF16), pw_b)
            tinv = tinv + _dot(tinv.astype(BF16), stack(pw.astype(BF16)))
            sz *= 2

        v_c = v[sl]
        vs = stack(v_c.astype(BF16))
        yo = _dot(jnp.concatenate([a_ak, a_rk], axis=0).astype(BF16), vs)
        st = st_ref[...]
        xo = _dot_nt(ar, st.astype(BF16))
        x = xo[0:C] + yo[0:C]
        u = _dot(tinv.astype(BF16), stack(x.astype(BF16)))
        o = xo[C:2 * C] + yo[C:2 * C] + _dot(a_rb.astype(BF16), stack(u.astype(BF16)))
        uv = jnp.concatenate([u, v_c], axis=0).astype(BF16)
        bke = jnp.concatenate([b_end[sl], k_end[sl]], axis=0).astype(BF16)
        upd = _dot_tn(uv, bke)
        st_ref[...] = st * w_tot[c * C:c * C + 1, :] + jnp.where(block_diag, upd, 0.0)
        outs.append(o)

    o = jnp.concatenate(outs, axis=0)
    inv_d = 1.0 / HEAD_DIM
    mean = _split_dot_right(o, ones, 2) * inv_d
    cen = o - mean
    var = _split_dot_right(cen * cen, ones, 2) * inv_d
    o = cen * lax.rsqrt(var + GN_EPS) * lng_ref[...] + lnb_ref[...]
    y_ref[...] = ((o + bonus) * g).astype(BF16)


def _rwkv(p, mu, w0, w2, a0, a2, g2, kkp, ka, rk, lng, lnb, *, batch, seq, tb):
    T, cols = p.shape
    w = w0.shape[1]
    nb = seq // tb
    vec = _resident((1, w))
    low = _resident((cols - 3 * w, w))
    return pl.pallas_call(
        _rwkv_kernel,
        out_shape=jax.ShapeDtypeStruct((T, w), BF16),
        grid=(batch, nb),
        in_specs=[
            pl.BlockSpec((tb, cols), lambda b, j: (b * nb + j, 0)),
            _resident((1, cols)), vec, low, vec, low, low, vec, vec, vec, vec, vec,
        ],
        out_specs=pl.BlockSpec((tb, w), lambda b, j: (b * nb + j, 0)),
        scratch_shapes=[pltpu.VMEM((SUBLANES, cols), F32), pltpu.VMEM((w, w), F32)],
        compiler_params=pltpu.CompilerParams(
            dimension_semantics=("parallel", "arbitrary"), vmem_limit_bytes=VMEM_LIMIT),
        name="rwkv",
    )(p, mu, w0, w2, a0, a2, g2, kkp, ka, rk, lng, lnb)


def _out_proj_kernel(ya_ref, yb_ref, yc_ref, x_ref, wa_ref, wb_ref, wc_ref, fg_ref, n2_ref,
                     wr_ref, br_ref, x1_ref, h2_ref, comb_ref):
    ybn = (_rms(yb_ref[...].astype(F32)) * fg_ref[...]).astype(BF16)
    x1 = (x_ref[...] + _dot(ya_ref[...], wa_ref[...]) + _dot(ybn, wb_ref[...])
          + _dot(yc_ref[...], wc_ref[...]))
    x1_ref[...] = x1
    h2 = _rms(x1) * n2_ref[...]
    h2_ref[...] = h2.astype(BF16)

    logits = jnp.dot(h2, wr_ref[...], preferred_element_type=F32,
                     precision=lax.Precision.HIGHEST) + br_ref[...]
    lane = lax.broadcasted_iota(jnp.int32, logits.shape, 1)
    lane_f = lane.astype(F32)
    far = float(LANES)

    def first_argmax(vals, vmax):
        return jnp.min(jnp.where(vals == vmax, lane_f, far), axis=-1, keepdims=True)

    gl = jnp.where(lane < N_GROUPS, logits, NEG_BIG)
    gmax = jnp.max(gl, axis=-1, keepdims=True)
    gidx = first_argmax(gl, gmax)
    g_p = 1.0 / jnp.sum(jnp.exp(gl - gmax), axis=-1, keepdims=True)

    e_group = ((lane - N_GROUPS) >> 2).astype(F32)
    el = jnp.where((lane >= N_GROUPS) & (e_group == gidx), logits, NEG_BIG)
    emax = jnp.max(el, axis=-1, keepdims=True)
    esum = jnp.sum(jnp.exp(el - emax), axis=-1, keepdims=True)
    i1 = first_argmax(el, emax)
    el2 = jnp.where(lane_f == i1, NEG_BIG, el)
    emax2 = jnp.max(el2, axis=-1, keepdims=True)
    i2 = first_argmax(el2, emax2)
    p1 = 1.0 / esum
    p2 = jnp.exp(emax2 - emax) / esum
    den = p1 + p2
    comb_ref[...] = (jnp.where(lane_f == i1, g_p * (p1 / den), 0.0)
                     + jnp.where(lane_f == i2, g_p * (p2 / den), 0.0))


def _out_proj(ya, yb, yc, x2, wa, wb, wc, fg, n2, wr, br, *, tm):
    T, D = x2.shape
    row = lambda i: (i, 0)
    return pl.pallas_call(
        _out_proj_kernel,
        out_shape=(jax.ShapeDtypeStruct((T, D), F32), jax.ShapeDtypeStruct((T, D), BF16),
                   jax.ShapeDtypeStruct((T, LANES), F32)),
        grid=(T // tm,),
        in_specs=[
            pl.BlockSpec((tm, ya.shape[1]), row),
            pl.BlockSpec((tm, yb.shape[1]), row),
            pl.BlockSpec((tm, yc.shape[1]), row),
            pl.BlockSpec((tm, D), row),
            _resident(wa.shape), _resident(wb.shape), _resident(wc.shape),
            _resident(fg.shape), _resident(n2.shape), _resident(wr.shape), _resident(br.shape),
        ],
        out_specs=(pl.BlockSpec((tm, D), row), pl.BlockSpec((tm, D), row),
                   pl.BlockSpec((tm, LANES), row)),
        compiler_params=pltpu.CompilerParams(
            dimension_semantics=("parallel",), vmem_limit_bytes=VMEM_LIMIT),
        name="out_proj",
    )(ya, yb, yc, x2, wa, wb, wc, fg, n2, wr, br)


def _moe_kernel(h_ref, comb_ref, x_ref, wgu_ref, wd_ref, o_ref, hid_ref):
    h = h_ref[...]
    comb = comb_ref[...]
    n_exp, _, two_f = wgu_ref.shape
    f = two_f // 2
    for e in range(n_exp):
        gu = _dot(h, wgu_ref[e])
        gate = gu[:, 0:f]
        hid = gate * jax.nn.sigmoid(gate) * gu[:, f:two_f]
        hid = hid * comb[:, N_GROUPS + e:N_GROUPS + e + 1]
        hid_ref[:, e * f:(e + 1) * f] = hid.astype(BF16)
    o_ref[...] = x_ref[...] + _dot(hid_ref[...], wd_ref[...])


def _moe(h2, comb, x1, wgu, wd, *, tm):
    T, D = x1.shape
    row = lambda i: (i, 0)
    return pl.pallas_call(
        _moe_kernel,
        out_shape=jax.ShapeDtypeStruct((T, D), F32),
        grid=(T // tm,),
        in_specs=[
            pl.BlockSpec((tm, D), row),
            pl.BlockSpec((tm, LANES), row),
            pl.BlockSpec((tm, D), row),
            _resident(wgu.shape),
            _resident(wd.shape),
        ],
        out_specs=pl.BlockSpec((tm, D), row),
        scratch_shapes=[pltpu.VMEM((tm, wd.shape[0]), BF16)],
        compiler_params=pltpu.CompilerParams(
            dimension_semantics=("parallel",), vmem_limit_bytes=VMEM_LIMIT),
        name="moe",
    )(h2, comb, x1, wgu, wd)


def _block_diag(wh):
    n_heads, d, _ = wh.shape
    eye = jnp.eye(n_heads, dtype=wh.dtype)
    return jnp.einsum("hij,hg->higj", wh, eye).reshape(n_heads * d, n_heads * d)


def _row(vec):
    return vec.reshape(1, -1).astype(F32)


def _pad_rows(mat, start, total):
    return jnp.zeros((total, mat.shape[1]), mat.dtype).at[start:start + mat.shape[0]].set(mat)


def kernel(x, norm1_g, w_in, conv_w, conv_b, lru_wa, lru_ba, lru_wx, lru_bx, lru_lambda,
           lru_norm_g, fox_fb, fox_qnorm_g, fox_knorm_g, fox_norm_g, rwkv_mu, rwkv_w0,
           rwkv_w2, rwkv_a0, rwkv_a2, rwkv_g2, rwkv_kk, rwkv_ka, rwkv_rk, rwkv_ln_g,
           rwkv_ln_b, w_out, norm2_g, router_gw, router_gb, router_ew, router_eb,
           exp_w_gate, exp_w_up, exp_w_down):
    batch, seq, d_model = x.shape
    depth = w_in.shape[0]
    lru_w = conv_w.shape[2]
    fox_heads = fox_fb.shape[1]
    fox_w = fox_heads * HEAD_DIM
    rwkv_w = rwkv_w0.shape[1]
    rwkv_cols = rwkv_mu.shape[1]
    d_rank, a_rank, g_rank = rwkv_w2.shape[1], rwkv_a2.shape[1], rwkv_g2.shape[1]
    low = d_rank + a_rank + g_rank
    n_exp, _, d_exp = exp_w_gate.shape[1:]
    assert low == LANES and rwkv_cols == 3 * rwkv_w + low
    assert fox_heads <= LANES and n_exp == N_EXPERTS

    o_fox = 2 * lru_w
    o_fl = o_fox + 3 * fox_w
    o_rwkv = o_fl + fox_heads
    seg = (0, o_fox, o_fox + fox_w, o_fox + 2 * fox_w, o_fl, o_fl + rwkv_cols,
           o_fl + rwkv_cols + LANES)

    x2 = x.reshape(batch * seq, d_model)
    for l in range(depth):
        wl = w_in[l]
        w_perm = jnp.concatenate(
            [wl[:, 0:o_fl], wl[:, o_rwkv:o_rwkv + rwkv_cols], wl[:, o_fl:o_rwkv],
             jnp.zeros((d_model, LANES - fox_heads), wl.dtype)], axis=1).astype(BF16)
        qg = _row(jnp.tile(fox_qnorm_g[l], fox_heads) * (HEAD_DIM ** -0.5))
        kg = _row(jnp.tile(fox_knorm_g[l], fox_heads))
        fb = _row(jnp.pad(fox_fb[l], (0, LANES - fox_heads)))
        u, q, k, v, p, c = _in_proj(x2, _row(norm1_g[l]), w_perm, qg, kg, fb,
                                    seq=seq, tm=512, seg=seg)

        wg = jnp.concatenate([_block_diag(lru_wa[l]), _block_diag(lru_wx[l])], axis=1).astype(BF16)
        bg = _row(jnp.concatenate([lru_ba[l], lru_bx[l]]))
        ya = _lru(u, conv_w[l], _row(conv_b[l]), wg, bg, _row(lru_lambda[l]),
                  _row(lru_norm_g[l]), batch=batch, seq=seq, tb=512)

        c_rows = c[:, 0:fox_heads].reshape(batch, seq, fox_heads).transpose(0, 2, 1)
        yb = _fox(q, k, v, c, c_rows, batch=batch, seq=seq, tq=512, tk=512)

        yc = _rwkv(p, _row(rwkv_mu[l]), _row(rwkv_w0[l]),
                   _pad_rows(rwkv_w2[l], 0, low).astype(BF16), _row(rwkv_a0[l]),
                   _pad_rows(rwkv_a2[l], d_rank, low).astype(BF16),
                   _pad_rows(rwkv_g2[l], d_rank + a_rank, low).astype(BF16),
                   _row(rwkv_kk[l]), _row(rwkv_ka[l]), _row(rwkv_rk[l]),
                   _row(rwkv_ln_g[l]), _row(rwkv_ln_b[l]), batch=batch, seq=seq, tb=256)

        wo = w_out[l].astype(BF16)
        wr = jnp.concatenate(
            [router_gw[l], router_ew[l],
             jnp.zeros((d_model, LANES - N_GROUPS - n_exp), F32)], axis=1)
        br = _row(jnp.pad(jnp.concatenate([router_gb[l], router_eb[l]]),
                          (0, LANES - N_GROUPS - n_exp)))
        x1, h2, comb = _out_proj(
            ya, yb, yc, x2, wo[0:lru_w], wo[lru_w:lru_w + fox_w], wo[lru_w + fox_w:],
            _row(fox_norm_g[l]), _row(norm2_g[l]), wr, br, tm=512)

        wgu = jnp.concatenate([exp_w_gate[l], exp_w_up[l]], axis=2).astype(BF16)
        wd = exp_w_down[l].reshape(n_exp * d_exp, d_model).astype(BF16)
        x2 = _moe(h2, comb, x1, wgu, wd, tm=512)
    return x2.reshape(batch, seq, d_model)
```

```python
import functools

import jax
import jax.numpy as jnp
from jax import lax
from jax.experimental import pallas as pl
from jax.experimental.pallas import tpu as pltpu

F32 = jnp.float32
BF16 = jnp.bfloat16

NORM_EPS = 1e-6
GN_EPS = 64e-5
LRU_C = 8.0
HEAD_DIM = 64
HEAD_SHIFT = 6
LANES = 128
SUBLANES = 8
CHUNK = 64
CONV_WIDTH = 4
N_GROUPS = 4
EXPERTS_PER_GROUP = 4
N_EXPERTS = N_GROUPS * EXPERTS_PER_GROUP
NEG_BIG = -1e30
LOG2E = 1.4426950408889634
FOX_BIAS_PIECES = 3
FOX_DEN_ROWS = 16
VMEM_LIMIT = 56 * 1024 * 1024


def _dot(a, b):
    return jnp.dot(a, b, preferred_element_type=F32)


def _dot_nt(a, b):
    return lax.dot_general(a, b, (((1,), (1,)), ((), ())), preferred_element_type=F32)


def _dot_tn(a, b):
    return lax.dot_general(a, b, (((0,), (0,)), ((), ())), preferred_element_type=F32)


def _split_dot_right(x, ones, parts):
    acc = None
    rem = x
    for _ in range(parts):
        hi = rem.astype(BF16)
        t = _dot(hi, ones)
        acc = t if acc is None else acc + t
        rem = rem - hi.astype(F32)
    return acc


def _split_dot_left(ones, x, parts):
    acc = None
    rem = x
    for _ in range(parts):
        hi = rem.astype(BF16)
        t = _dot(ones, hi)
        acc = t if acc is None else acc + t
        rem = rem - hi.astype(F32)
    return acc


def _head_ones(n):
    r = lax.broadcasted_iota(jnp.int32, (n, n), 0) >> HEAD_SHIFT
    c = lax.broadcasted_iota(jnp.int32, (n, n), 1) >> HEAD_SHIFT
    return jnp.where(r == c, 1.0, 0.0).astype(BF16)


def _softplus(z):
    return jnp.maximum(z, 0.0) + jnp.log(1.0 + jnp.exp(-jnp.abs(z)))


def _rms(x, eps=NORM_EPS):
    return x * lax.rsqrt(jnp.mean(x * x, axis=-1, keepdims=True) + eps)


def _resident(shape):
    zeros = (0,) * len(shape)
    return pl.BlockSpec(shape, lambda *_: zeros, pipeline_mode=pl.Buffered(1))


def _in_proj_kernel(x_ref, g_ref, w_ref, qg_ref, kg_ref, fb_ref,
                    lru_ref, q_ref, k_ref, v_ref, p_ref, carry_ref,
                    *, tiles_per_seq, seg):
    i = pl.program_id(0)

    @pl.when(i % tiles_per_seq == 0)
    def _():
        carry_ref[...] = jnp.zeros_like(carry_ref)

    hb = (_rms(x_ref[...]) * g_ref[...]).astype(BF16)
    lru_ref[...] = _dot(hb, w_ref[:, seg[0]:seg[1]])

    fox = seg[2] - seg[1]
    ones = _head_ones(fox)
    inv_d = 1.0 / HEAD_DIM
    q = _dot(hb, w_ref[:, seg[1]:seg[2]])
    q = q * lax.rsqrt(_split_dot_right(q * q, ones, 2) * inv_d + NORM_EPS) * qg_ref[...]
    k = _dot(hb, w_ref[:, seg[2]:seg[3]])
    k = k * lax.rsqrt(_split_dot_right(k * k, ones, 2) * inv_d + NORM_EPS) * kg_ref[...]
    v_ref[...] = _dot(hb, w_ref[:, seg[3]:seg[4]]).T.astype(BF16)
    p_ref[...] = _dot(hb, w_ref[:, seg[4]:seg[5]])

    fl = _dot(hb, w_ref[:, seg[5]:seg[6]]) + fb_ref[...]
    lf = -_softplus(-fl)
    tm = fl.shape[0]
    rr = lax.broadcasted_iota(jnp.int32, (tm, tm), 0)
    cc = lax.broadcasted_iota(jnp.int32, (tm, tm), 1)
    tri = jnp.where(cc <= rr, 1.0, 0.0).astype(BF16)
    c = _split_dot_left(tri, lf, 3) + carry_ref[0:1, :]
    carry_ref[...] = jnp.broadcast_to(c[tm - 1:tm, :], carry_ref.shape)

    wide = 2 * fox
    sr = lax.broadcasted_iota(jnp.int32, (fox, wide), 0)
    sc = lax.broadcasted_iota(jnp.int32, (fox, wide), 1)
    spread = jnp.where(sc == (sr >> HEAD_SHIFT) * LANES + (sr & (HEAD_DIM - 1)), 1.0, 0.0).astype(BF16)
    lane_w = lax.broadcasted_iota(jnp.int32, (1, wide), 1) & (LANES - 1)
    bias_lanes = (lane_w >= HEAD_DIM) & (lane_w < HEAD_DIM + FOX_BIAS_PIECES)
    q_ref[...] = (_dot(q.astype(BF16), spread) + jnp.where(bias_lanes, 1.0, 0.0)).astype(BF16)

    k_wide = _dot(k.astype(BF16), spread)
    rem = c * (-LOG2E)
    pr = lax.broadcasted_iota(jnp.int32, (LANES, wide), 0)
    pc = lax.broadcasted_iota(jnp.int32, (LANES, wide), 1)
    for piece in range(FOX_BIAS_PIECES):
        hi = rem.astype(BF16)
        place = jnp.where(pc == pr * LANES + (HEAD_DIM + piece), 1.0, 0.0).astype(BF16)
        k_wide = k_wide + _dot(hi, place)
        rem = rem - hi.astype(F32)
    k_ref[...] = k_wide.astype(BF16)


def _in_proj(x2, g, w, qg, kg, fb, *, seq, tm, seg):
    T, D = x2.shape
    n_out = w.shape[1]
    widths = [seg[j + 1] - seg[j] for j in range(6)]
    row = lambda i: (i, 0)
    col = lambda i: (0, i)
    outs = [
        ((T, widths[0]), (tm, widths[0]), row, F32),
        ((T, 2 * widths[1]), (tm, 2 * widths[1]), row, BF16),
        ((T, 2 * widths[2]), (tm, 2 * widths[2]), row, BF16),
        ((widths[3], T), (widths[3], tm), col, BF16),
        ((T, widths[4]), (tm, widths[4]), row, F32),
    ]
    return pl.pallas_call(
        functools.partial(_in_proj_kernel, tiles_per_seq=seq // tm, seg=seg),
        out_shape=tuple(jax.ShapeDtypeStruct(shape, dt) for shape, _, _, dt in outs),
        grid=(T // tm,),
        in_specs=[
            pl.BlockSpec((tm, D), row),
            _resident((1, D)),
            _resident((D, n_out)),
            _resident((1, widths[1])),
            _resident((1, widths[2])),
            _resident((1, widths[5])),
        ],
        out_specs=tuple(pl.BlockSpec(blk, imap) for _, blk, imap, _ in outs),
        scratch_shapes=[pltpu.VMEM((SUBLANES, widths[5]), F32)],
        compiler_params=pltpu.CompilerParams(
            dimension_semantics=("arbitrary",), vmem_limit_bytes=VMEM_LIMIT),
        name="in_proj",
    )(x2, g, w, qg, kg, fb)


def _shift_rows(x, d, fill, row):
    return jnp.where(row >= d, pltpu.roll(x, d, axis=0), fill)


def _lru_kernel(u_ref, cw_ref, cb_ref, wg_ref, bg_ref, lam_ref, ng_ref,
                y_ref, buf_ref, h_ref):
    j = pl.program_id(1)
    tb = u_ref.shape[0]
    w = u_ref.shape[1] // 2
    pad = SUBLANES

    @pl.when(j == 0)
    def _():
        buf_ref[0:pad, :] = jnp.zeros((pad, w), F32)
        h_ref[...] = jnp.zeros_like(h_ref)

    xa = u_ref[:, 0:w]
    ga = u_ref[:, w:2 * w]
    buf_ref[pad:pad + tb, :] = xa
    xc = cb_ref[...] + cw_ref[CONV_WIDTH - 1:CONV_WIDTH, :] * xa
    for d in range(1, CONV_WIDTH):
        xc = xc + cw_ref[CONV_WIDTH - 1 - d:CONV_WIDTH - d, :] * buf_ref[pad - d:pad - d + tb, :]
    buf_ref[0:pad, :] = xa[tb - pad:tb, :]

    gates = _dot(xc.astype(BF16), wg_ref[...]) + bg_ref[...]
    r = jax.nn.sigmoid(gates[:, 0:w])
    i = jax.nn.sigmoid(gates[:, w:2 * w])
    log_a = (-LRU_C) * r * _softplus(-lam_ref[...])
    a = jnp.exp(log_a)
    b = jnp.sqrt(jnp.tanh(-log_a) * (1.0 + a * a)) * (i * xc)

    row = lax.broadcasted_iota(jnp.int32, (tb, w), 0)
    d = 1
    while d < tb:
        a_sh = _shift_rows(a, d, 1.0, row)
        b_sh = _shift_rows(b, d, 0.0, row)
        b = a * b_sh + b
        a = a * a_sh
        d *= 2
    h = b + a * h_ref[0:1, :]
    h_ref[...] = jnp.broadcast_to(h[tb - 1:tb, :], h_ref.shape)

    y = jax.nn.gelu(ga) * h
    y_ref[...] = (_rms(y) * ng_ref[...]).astype(BF16)


def _lru(u, cw, cb, wg, bg, lam, ng, *, batch, seq, tb):
    T, w2 = u.shape
    w = w2 // 2
    nb = seq // tb
    return pl.pallas_call(
        _lru_kernel,
        out_shape=jax.ShapeDtypeStruct((T, w), BF16),
        grid=(batch, nb),
        in_specs=[
            pl.BlockSpec((tb, w2), lambda b, j: (b * nb + j, 0)),
            _resident((CONV_WIDTH, w)),
            _resident((1, w)),
            _resident((w, w2)),
            _resident((1, w2)),
            _resident((1, w)),
            _resident((1, w)),
        ],
        out_specs=pl.BlockSpec((tb, w), lambda b, j: (b * nb + j, 0)),
        scratch_shapes=[pltpu.VMEM((tb + SUBLANES, w), F32), pltpu.VMEM((SUBLANES, w), F32)],
        compiler_params=pltpu.CompilerParams(
            dimension_semantics=("parallel", "arbitrary"), vmem_limit_bytes=VMEM_LIMIT),
        name="lru",
    )(u, cw, cb, wg, bg, lam, ng)


def _fox_kernel(q_ref, k_ref, v_ref, o_ref, s_buf, mx_buf, m_ref, acc_ref, *, tq):
    i = pl.program_id(2)
    n_heads = q_ref.shape[1] // LANES
    causal = (lax.broadcasted_iota(jnp.int32, (tq, tq), 0)
              <= lax.broadcasted_iota(jnp.int32, (tq, tq), 1))

    def scores(j, slot, masked):
        k0 = pl.multiple_of(j * tq, tq)
        for u in range(n_heads):
            hl = slice(u * LANES, (u + 1) * LANES)
            s = _dot_nt(k_ref[pl.ds(k0, tq), hl], q_ref[:, hl])
            if masked:
                s = jnp.where(causal, s, NEG_BIG)
            s_buf[slot, u] = s
            mx_buf[slot, u] = jnp.max(s, axis=0, keepdims=True)

    def consume(j, slot):
        k0 = pl.multiple_of(j * tq, tq)
        for u in range(n_heads):
            m = m_ref[u]
            m_new = jnp.maximum(m, mx_buf[slot, u])
            alpha = jnp.exp2(m - m_new)
            p = jnp.exp2(s_buf[slot, u] - m_new)
            m_ref[u] = m_new
            vt = jnp.concatenate([v_ref[u * HEAD_DIM:(u + 1) * HEAD_DIM, pl.ds(k0, tq)],
                                  jnp.ones((FOX_DEN_ROWS, tq), BF16)], axis=0)
            acc_ref[u] = alpha * acc_ref[u] + _dot(vt, p.astype(BF16))

    m_ref[...] = jnp.full(m_ref.shape, NEG_BIG, F32)
    acc_ref[...] = jnp.zeros_like(acc_ref)

    def stage(j, slot, masked_next):
        scores(j + 1, 1 - slot, masked_next)
        consume(j, slot)

    @pl.when(i == 0)
    def _():
        scores(0, 0, True)
        consume(0, 0)

    @pl.when(i > 0)
    def _():
        scores(0, 0, False)

        def body(t, carry):
            stage(2 * t, 0, False)
            stage(2 * t + 1, 1, False)
            return carry

        lax.fori_loop(0, (i - 1) // 2, body, 0)

        @pl.when(i % 2 == 1)
        def _():
            stage(i - 1, 0, True)
            consume(i, 1)

        @pl.when(i % 2 == 0)
        def _():
            stage(i - 2, 0, False)
            stage(i - 1, 1, True)
            consume(i, 0)

    o_t = jnp.concatenate(
        [acc_ref[u, 0:HEAD_DIM, :] / acc_ref[u, HEAD_DIM:HEAD_DIM + 1, :] for u in range(n_heads)],
        axis=0)
    o_ref[...] = o_t.T.astype(BF16)


def _fox(q_aug, k_aug, v_t, *, batch, seq, tq):
    T, wide = q_aug.shape
    heads_per_step = 2
    blk = heads_per_step * LANES
    nq = seq // tq
    steps = wide // blk
    return pl.pallas_call(
        functools.partial(_fox_kernel, tq=tq),
        out_shape=jax.ShapeDtypeStruct((T, steps * heads_per_step * HEAD_DIM), BF16),
        grid=(batch, steps, nq),
        in_specs=[
            pl.BlockSpec((tq, blk), lambda b, h, i: (b * nq + i, h)),
            pl.BlockSpec((seq, blk), lambda b, h, i: (b, h)),
            pl.BlockSpec((heads_per_step * HEAD_DIM, seq), lambda b, h, i: (h, b)),
        ],
        out_specs=pl.BlockSpec((tq, heads_per_step * HEAD_DIM), lambda b, h, i: (b * nq + i, h)),
        scratch_shapes=[
            pltpu.VMEM((2, heads_per_step, tq, tq), F32),
            pltpu.VMEM((2, heads_per_step, 1, tq), F32),
            pltpu.VMEM((heads_per_step, 1, tq), F32),
            pltpu.VMEM((heads_per_step, HEAD_DIM + FOX_DEN_ROWS, tq), F32),
        ],
        compiler_params=pltpu.CompilerParams(
            dimension_semantics=("parallel", "parallel", "arbitrary"),
            vmem_limit_bytes=VMEM_LIMIT),
        name="fox",
    )(q_aug, k_aug, v_t)


def _stack_heads(x, lane_head, n_heads):
    zero = jnp.zeros_like(x)
    return jnp.concatenate([jnp.where(lane_head == h, x, zero) for h in range(n_heads)], axis=0)


def _rwkv_kernel(p_ref, mu_ref, w0_ref, w2_ref, a0_ref, a2_ref, g2_ref, kk_ref, ka_ref,
                 rk_ref, lng_ref, lnb_ref, y_ref, prev_ref, st_ref):
    j = pl.program_id(1)
    tb, cols = p_ref.shape
    w = y_ref.shape[1]
    n_heads = w // HEAD_DIM
    C = CHUNK
    chunk_shift = CHUNK.bit_length() - 1

    @pl.when(j == 0)
    def _():
        prev_ref[...] = jnp.zeros_like(prev_ref)
        st_ref[...] = jnp.zeros_like(st_ref)

    p = p_ref[...]
    row = lax.broadcasted_iota(jnp.int32, (tb, cols), 0)
    prev = jnp.where(row == 0, prev_ref[0:1, :], pltpu.roll(p, 1, axis=0))
    prev_ref[...] = jnp.broadcast_to(p[tb - 1:tb, :], prev_ref.shape)
    ps = p + (prev - p) * mu_ref[...]
    r = ps[:, 0:w]
    k = ps[:, w:2 * w]
    v = ps[:, 2 * w:3 * w]
    lr = ps[:, 3 * w:cols]

    wpre = w0_ref[...] + _dot(jnp.tanh(lr).astype(BF16), w2_ref[...])
    lw = -jnp.exp(-_softplus(-wpre) - 0.5)
    a = jax.nn.sigmoid(a0_ref[...] + _dot(lr.astype(BF16), a2_ref[...]))
    g = _dot(jax.nn.sigmoid(lr).astype(BF16), g2_ref[...])

    ones = _head_ones(w)
    kk = k * kk_ref[...]
    kk = kk / jnp.maximum(jnp.sqrt(_split_dot_right(kk * kk, ones, 2)), 1e-12)
    k2 = k * (1.0 + (a - 1.0) * ka_ref[...])
    b = kk * a
    bonus = _split_dot_right(r * k2 * rk_ref[...], ones, 2) * v

    rr = lax.broadcasted_iota(jnp.int32, (tb, tb), 0)
    cc = lax.broadcasted_iota(jnp.int32, (tb, tb), 1)
    same = (rr >> chunk_shift) == (cc >> chunk_shift)
    tri = jnp.where(same & (cc <= rr), 1.0, 0.0).astype(BF16)
    tot = jnp.where(same, 1.0, 0.0).astype(BF16)
    LW = _split_dot_left(tri, lw, 3)
    LT = _split_dot_left(tot, lw, 3)
    e_out = jnp.exp(-LW)
    e_end = jnp.exp(LT - LW)
    at = (-kk) * jnp.exp(LW - lw)
    rt = r * jnp.exp(LW)
    bt = b * e_out
    kt = k2 * e_out
    b_end = b * e_end
    k_end = k2 * e_end
    w_tot = jnp.exp(LT)

    lane_head = lax.broadcasted_iota(jnp.int32, (C, w), 1) >> HEAD_SHIFT
    t_idx = lax.broadcasted_iota(jnp.int32, (C, w), 0)
    j_idx = lax.broadcasted_iota(jnp.int32, (C, w), 1) & (HEAD_DIM - 1)
    strict = j_idx < t_idx
    incl = j_idx <= t_idx
    eye = jnp.where(j_idx == t_idx, 1.0, 0.0)
    sq_r = lax.broadcasted_iota(jnp.int32, (w, w), 0) >> HEAD_SHIFT
    sq_c = lax.broadcasted_iota(jnp.int32, (w, w), 1) >> HEAD_SHIFT
    block_diag = sq_r == sq_c
    stack = functools.partial(_stack_heads, lane_head=lane_head, n_heads=n_heads)

    outs = []
    for c in range(tb // C):
        sl = slice(c * C, (c + 1) * C)
        ar = jnp.concatenate([at[sl], rt[sl]], axis=0).astype(BF16)
        bk = jnp.concatenate([stack(bt[sl].astype(BF16)), stack(kt[sl].astype(BF16))], axis=0)
        A = _dot_nt(ar, bk)
        a_ab = jnp.where(strict, A[0:C, 0:w], 0.0)
        a_ak = jnp.where(strict, A[0:C, w:2 * w], 0.0)
        a_rb = jnp.where(incl, A[C:2 * C, 0:w], 0.0)
        a_rk = jnp.where(incl, A[C:2 * C, w:2 * w], 0.0)

        pw = a_ab
        tinv = eye + a_ab
        sz = 1
        while 2 * sz < C:
            pw_b = stack(pw.astype(BF16))
            pw = _dot(pw.astype(BF16), pw_b)
            tinv = tinv + _dot(tinv.astype(BF16), stack(pw.astype(BF16)))
            sz *= 2

        v_c = v[sl]
        vs = stack(v_c.astype(BF16))
        yo = _dot(jnp.concatenate([a_ak, a_rk], axis=0).astype(BF16), vs)
        st = st_ref[...]
        xo = _dot_nt(ar, st.astype(BF16))
        x = xo[0:C] + yo[0:C]
        u = _dot(tinv.astype(BF16), stack(x.astype(BF16)))
        o = xo[C:2 * C] + yo[C:2 * C] + _dot(a_rb.astype(BF16), stack(u.astype(BF16)))
        uv = jnp.concatenate([u, v_c], axis=0).astype(BF16)
        bke = jnp.concatenate([b_end[sl], k_end[sl]], axis=0).astype(BF16)
        upd = _dot_tn(uv, bke)
        st_ref[...] = st * w_tot[c * C:c * C + 1, :] + jnp.where(block_diag, upd, 0.0)
        outs.append(o)

    o = jnp.concatenate(outs, axis=0)
    inv_d = 1.0 / HEAD_DIM
    mean = _split_dot_right(o, ones, 2) * inv_d
    cen = o - mean
    var = _split_dot_right(cen * cen, ones, 2) * inv_d
    o = cen * lax.rsqrt(var + GN_EPS) * lng_ref[...] + lnb_ref[...]
    y_ref[...] = ((o + bonus) * g).astype(BF16)


def _rwkv(p, mu, w0, w2, a0, a2, g2, kkp, ka, rk, lng, lnb, *, batch, seq, tb):
    T, cols = p.shape
    w = w0.shape[1]
    nb = seq // tb
    vec = _resident((1, w))
    low = _resident((cols - 3 * w, w))
    return pl.pallas_call(
        _rwkv_kernel,
        out_shape=jax.ShapeDtypeStruct((T, w), BF16),
        grid=(batch, nb),
        in_specs=[
            pl.BlockSpec((tb, cols), lambda b, j: (b * nb + j, 0)),
            _resident((1, cols)), vec, low, vec, low, low, vec, vec, vec, vec, vec,
        ],
        out_specs=pl.BlockSpec((tb, w), lambda b, j: (b * nb + j, 0)),
        scratch_shapes=[pltpu.VMEM((SUBLANES, cols), F32), pltpu.VMEM((w, w), F32)],
        compiler_params=pltpu.CompilerParams(
            dimension_semantics=("parallel", "arbitrary"), vmem_limit_bytes=VMEM_LIMIT),
        name="rwkv",
    )(p, mu, w0, w2, a0, a2, g2, kkp, ka, rk, lng, lnb)


def _out_proj_kernel(ya_ref, yb_ref, yc_ref, x_ref, wa_ref, wb_ref, wc_ref, fg_ref, n2_ref,
                     wr_ref, br_ref, x1_ref, h2_ref, comb_ref):
    ybn = (_rms(yb_ref[...].astype(F32)) * fg_ref[...]).astype(BF16)
    x1 = (x_ref[...] + _dot(ya_ref[...], wa_ref[...]) + _dot(ybn, wb_ref[...])
          + _dot(yc_ref[...], wc_ref[...]))
    x1_ref[...] = x1
    h2 = _rms(x1) * n2_ref[...]
    h2_ref[...] = h2.astype(BF16)

    logits = jnp.dot(h2, wr_ref[...], preferred_element_type=F32,
                     precision=lax.Precision.HIGHEST) + br_ref[...]
    lane = lax.broadcasted_iota(jnp.int32, logits.shape, 1)
    lane_f = lane.astype(F32)
    far = float(LANES)

    def first_argmax(vals, vmax):
        return jnp.min(jnp.where(vals == vmax, lane_f, far), axis=-1, keepdims=True)

    gl = jnp.where(lane < N_GROUPS, logits, NEG_BIG)
    gmax = jnp.max(gl, axis=-1, keepdims=True)
    gidx = first_argmax(gl, gmax)
    g_p = 1.0 / jnp.sum(jnp.exp(gl - gmax), axis=-1, keepdims=True)

    e_group = ((lane - N_GROUPS) >> 2).astype(F32)
    el = jnp.where((lane >= N_GROUPS) & (e_group == gidx), logits, NEG_BIG)
    emax = jnp.max(el, axis=-1, keepdims=True)
    esum = jnp.sum(jnp.exp(el - emax), axis=-1, keepdims=True)
    i1 = first_argmax(el, emax)
    el2 = jnp.where(lane_f == i1, NEG_BIG, el)
    emax2 = jnp.max(el2, axis=-1, keepdims=True)
    i2 = first_argmax(el2, emax2)
    p1 = 1.0 / esum
    p2 = jnp.exp(emax2 - emax) / esum
    den = p1 + p2
    comb_ref[...] = (jnp.where(lane_f == i1, g_p * (p1 / den), 0.0)
                     + jnp.where(lane_f == i2, g_p * (p2 / den), 0.0))


def _out_proj(ya, yb, yc, x2, wa, wb, wc, fg, n2, wr, br, *, tm):
    T, D = x2.shape
    row = lambda i: (i, 0)
    return pl.pallas_call(
        _out_proj_kernel,
        out_shape=(jax.ShapeDtypeStruct((T, D), F32), jax.ShapeDtypeStruct((T, D), BF16),
                   jax.ShapeDtypeStruct((T, LANES), F32)),
        grid=(T // tm,),
        in_specs=[
            pl.BlockSpec((tm, ya.shape[1]), row),
            pl.BlockSpec((tm, yb.shape[1]), row),
            pl.BlockSpec((tm, yc.shape[1]), row),
            pl.BlockSpec((tm, D), row),
            _resident(wa.shape), _resident(wb.shape), _resident(wc.shape),
            _resident(fg.shape), _resident(n2.shape), _resident(wr.shape), _resident(br.shape),
        ],
        out_specs=(pl.BlockSpec((tm, D), row), pl.BlockSpec((tm, D), row),
                   pl.BlockSpec((tm, LANES), row)),
        compiler_params=pltpu.CompilerParams(
            dimension_semantics=("parallel",), vmem_limit_bytes=VMEM_LIMIT),
        name="out_proj",
    )(ya, yb, yc, x2, wa, wb, wc, fg, n2, wr, br)


def _moe_kernel(h_ref, comb_ref, x_ref, wgu_ref, wd_ref, o_ref, hid_ref):
    h = h_ref[...]
    comb = comb_ref[...]
    n_exp, _, two_f = wgu_ref.shape
    f = two_f // 2
    for e in range(n_exp):
        gu = _dot(h, wgu_ref[e])
        gate = gu[:, 0:f]
        hid = gate * jax.nn.sigmoid(gate) * gu[:, f:two_f]
        hid = hid * comb[:, N_GROUPS + e:N_GROUPS + e + 1]
        hid_ref[:, e * f:(e + 1) * f] = hid.astype(BF16)
    o_ref[...] = x_ref[...] + _dot(hid_ref[...], wd_ref[...])


def _moe(h2, comb, x1, wgu, wd, *, tm):
    T, D = x1.shape
    row = lambda i: (i, 0)
    return pl.pallas_call(
        _moe_kernel,
        out_shape=jax.ShapeDtypeStruct((T, D), F32),
        grid=(T // tm,),
        in_specs=[
            pl.BlockSpec((tm, D), row),
            pl.BlockSpec((tm, LANES), row),
            pl.BlockSpec((tm, D), row),
            _resident(wgu.shape),
            _resident(wd.shape),
        ],
        out_specs=pl.BlockSpec((tm, D), row),
        scratch_shapes=[pltpu.VMEM((tm, wd.shape[0]), BF16)],
        compiler_params=pltpu.CompilerParams(
            dimension_semantics=("parallel",), vmem_limit_bytes=VMEM_LIMIT),
        name="moe",
    )(h2, comb, x1, wgu, wd)


def _block_diag(wh):
    n_heads, d, _ = wh.shape
    eye = jnp.eye(n_heads, dtype=wh.dtype)
    return jnp.einsum("hij,hg->higj", wh, eye).reshape(n_heads * d, n_heads * d)


def _row(vec):
    return vec.reshape(1, -1).astype(F32)


def _pad_rows(mat, start, total):
    return jnp.zeros((total, mat.shape[1]), mat.dtype).at[start:start + mat.shape[0]].set(mat)


def kernel(x, norm1_g, w_in, conv_w, conv_b, lru_wa, lru_ba, lru_wx, lru_bx, lru_lambda,
           lru_norm_g, fox_fb, fox_qnorm_g, fox_knorm_g, fox_norm_g, rwkv_mu, rwkv_w0,
           rwkv_w2, rwkv_a0, rwkv_a2, rwkv_g2, rwkv_kk, rwkv_ka, rwkv_rk, rwkv_ln_g,
           rwkv_ln_b, w_out, norm2_g, router_gw, router_gb, router_ew, router_eb,
           exp_w_gate, exp_w_up, exp_w_down):
    batch, seq, d_model = x.shape
    depth = w_in.shape[0]
    lru_w = conv_w.shape[2]
    fox_heads = fox_fb.shape[1]
    fox_w = fox_heads * HEAD_DIM
    rwkv_w = rwkv_w0.shape[1]
    rwkv_cols = rwkv_mu.shape[1]
    d_rank, a_rank, g_rank = rwkv_w2.shape[1], rwkv_a2.shape[1], rwkv_g2.shape[1]
    low = d_rank + a_rank + g_rank
    n_exp, _, d_exp = exp_w_gate.shape[1:]
    assert low == LANES and rwkv_cols == 3 * rwkv_w + low
    assert fox_heads <= LANES and n_exp == N_EXPERTS

    o_fox = 2 * lru_w
    o_fl = o_fox + 3 * fox_w
    o_rwkv = o_fl + fox_heads
    seg = (0, o_fox, o_fox + fox_w, o_fox + 2 * fox_w, o_fl, o_fl + rwkv_cols,
           o_fl + rwkv_cols + LANES)

    x2 = x.reshape(batch * seq, d_model)
    for l in range(depth):
        wl = w_in[l]
        w_perm = jnp.concatenate(
            [wl[:, 0:o_fl], wl[:, o_rwkv:o_rwkv + rwkv_cols], wl[:, o_fl:o_rwkv],
             jnp.zeros((d_model, LANES - fox_heads), wl.dtype)], axis=1).astype(BF16)
        qg = _row(jnp.tile(fox_qnorm_g[l], fox_heads) * (HEAD_DIM ** -0.5 * LOG2E))
        kg = _row(jnp.tile(fox_knorm_g[l], fox_heads))
        fb = _row(jnp.pad(fox_fb[l], (0, LANES - fox_heads)))
        u, q, k_aug, v, p = _in_proj(x2, _row(norm1_g[l]), w_perm, qg, kg, fb,
                                    seq=seq, tm=512, seg=seg)

        wg = jnp.concatenate([_block_diag(lru_wa[l]), _block_diag(lru_wx[l])], axis=1).astype(BF16)
        bg = _row(jnp.concatenate([lru_ba[l], lru_bx[l]]))
        ya = _lru(u, conv_w[l], _row(conv_b[l]), wg, bg, _row(lru_lambda[l]),
                  _row(lru_norm_g[l]), batch=batch, seq=seq, tb=512)

        yb = _fox(q, k_aug, v, batch=batch, seq=seq, tq=512)

        yc = _rwkv(p, _row(rwkv_mu[l]), _row(rwkv_w0[l]),
                   _pad_rows(rwkv_w2[l], 0, low).astype(BF16), _row(rwkv_a0[l]),
                   _pad_rows(rwkv_a2[l], d_rank, low).astype(BF16),
                   _pad_rows(rwkv_g2[l], d_rank + a_rank, low).astype(BF16),
                   _row(rwkv_kk[l]), _row(rwkv_ka[l]), _row(rwkv_rk[l]),
                   _row(rwkv_ln_g[l]), _row(rwkv_ln_b[l]), batch=batch, seq=seq, tb=256)

        wo = w_out[l].astype(BF16)
        wr = jnp.concatenate(
            [router_gw[l], router_ew[l],
             jnp.zeros((d_model, LANES - N_GROUPS - n_exp), F32)], axis=1)
        br = _row(jnp.pad(jnp.concatenate([router_gb[l], router_eb[l]]),
                          (0, LANES - N_GROUPS - n_exp)))
        x1, h2, comb = _out_proj(
            ya, yb, yc, x2, wo[0:lru_w], wo[lru_w:lru_w + fox_w], wo[lru_w + fox_w:],
            _row(fox_norm_g[l]), _row(norm2_g[l]), wr, br, tm=512)

        wgu = jnp.concatenate([exp_w_gate[l], exp_w_up[l]], axis=2).astype(BF16)
        wd = exp_w_down[l].reshape(n_exp * d_exp, d_model).astype(BF16)
        x2 = _moe(h2, comb, x1, wgu, wd, tm=512)
    return x2.reshape(batch, seq, d_model)
```

```python
import functools

import jax
import jax.numpy as jnp
from jax import lax
from jax.experimental import pallas as pl
from jax.experimental.pallas import tpu as pltpu

F32 = jnp.float32
BF16 = jnp.bfloat16

NORM_EPS = 1e-6
GN_EPS = 64e-5
LRU_C = 8.0
HEAD_DIM = 64
HEAD_SHIFT = 6
LANES = 128
SUBLANES = 8
CHUNK = 64
CONV_WIDTH = 4
N_GROUPS = 4
EXPERTS_PER_GROUP = 4
N_EXPERTS = N_GROUPS * EXPERTS_PER_GROUP
NEG_BIG = -1e30
LOG2E = 1.4426950408889634
FOX_BIAS_PIECES = 3
FOX_DEN_ROWS = 16
VMEM_LIMIT = 56 * 1024 * 1024


def _dot(a, b):
    return jnp.dot(a, b, preferred_element_type=F32)


def _dot_nt(a, b):
    return lax.dot_general(a, b, (((1,), (1,)), ((), ())), preferred_element_type=F32)


def _dot_tn(a, b):
    return lax.dot_general(a, b, (((0,), (0,)), ((), ())), preferred_element_type=F32)


def _split_dot_right(x, ones, parts):
    acc = None
    rem = x
    for _ in range(parts):
        hi = rem.astype(BF16)
        t = _dot(hi, ones)
        acc = t if acc is None else acc + t
        rem = rem - hi.astype(F32)
    return acc


def _split_dot_left(ones, x, parts):
    acc = None
    rem = x
    for _ in range(parts):
        hi = rem.astype(BF16)
        t = _dot(ones, hi)
        acc = t if acc is None else acc + t
        rem = rem - hi.astype(F32)
    return acc


def _head_ones(n):
    r = lax.broadcasted_iota(jnp.int32, (n, n), 0) >> HEAD_SHIFT
    c = lax.broadcasted_iota(jnp.int32, (n, n), 1) >> HEAD_SHIFT
    return jnp.where(r == c, 1.0, 0.0).astype(BF16)


def _softplus(z):
    return jnp.maximum(z, 0.0) + jnp.log(1.0 + jnp.exp(-jnp.abs(z)))


def _rms(x, eps=NORM_EPS):
    return x * lax.rsqrt(jnp.mean(x * x, axis=-1, keepdims=True) + eps)


def _resident(shape):
    zeros = (0,) * len(shape)
    return pl.BlockSpec(shape, lambda *_: zeros, pipeline_mode=pl.Buffered(1))


def _in_proj_kernel(x_ref, g_ref, w_ref, qg_ref, kg_ref, fb_ref,
                    lru_ref, q_ref, k_ref, v_ref, p_ref, carry_ref,
                    *, tiles_per_seq, seg, sub):
    i = pl.program_id(0)

    @pl.when(i % tiles_per_seq == 0)
    def _():
        carry_ref[...] = jnp.zeros_like(carry_ref)

    fox = seg[2] - seg[1]
    n_heads = fox // HEAD_DIM
    ones = _head_ones(fox)
    inv_d = 1.0 / HEAD_DIM
    rr = lax.broadcasted_iota(jnp.int32, (sub, sub), 0)
    cc = lax.broadcasted_iota(jnp.int32, (sub, sub), 1)
    tri = jnp.where(cc <= rr, 1.0, 0.0).astype(BF16)
    lane = lax.broadcasted_iota(jnp.int32, (sub, LANES), 1)
    feat = lane < HEAD_DIM
    bias_lanes = (lane >= HEAD_DIM) & (lane < HEAD_DIM + FOX_BIAS_PIECES)
    q_bias = jnp.where(bias_lanes, 1.0, 0.0)
    piece_id = lax.rem(lane, FOX_BIAS_PIECES)
    carry = carry_ref[0:1, :]

    def head_norm(t):
        return t * lax.rsqrt(_dot((t * t).astype(BF16), ones) * inv_d + NORM_EPS)

    for r0 in range(0, x_ref.shape[0], sub):
        rows = slice(r0, r0 + sub)
        hb = (_rms(x_ref[rows, :]) * g_ref[...]).astype(BF16)
        lru_ref[rows, :] = _dot(hb, w_ref[:, seg[0]:seg[1]])
        q = head_norm(_dot(hb, w_ref[:, seg[1]:seg[2]])) * qg_ref[...]
        k = head_norm(_dot(hb, w_ref[:, seg[2]:seg[3]])) * kg_ref[...]
        v_ref[:, rows] = _dot(hb, w_ref[:, seg[3]:seg[4]]).T.astype(BF16)
        p_ref[rows, :] = _dot(hb, w_ref[:, seg[4]:seg[5]])

        lf = -_softplus(-(_dot(hb, w_ref[:, seg[5]:seg[6]]) + fb_ref[...]))
        c = _split_dot_left(tri, lf, 3) + carry
        carry = c[sub - 1:sub, :]

        rem = c * (-LOG2E)
        pieces = None
        for piece in range(FOX_BIAS_PIECES):
            hi = rem.astype(BF16).astype(F32)
            pieces = hi if pieces is None else jnp.where(piece_id == piece, hi, pieces)
            rem = rem - hi

        for h in range(n_heads):
            src = slice((h // 2) * LANES, (h // 2 + 1) * LANES)
            dst = slice(h * LANES, (h + 1) * LANES)
            qh, kh = q[:, src], k[:, src]
            if h % 2:
                qh = pltpu.roll(qh, HEAD_DIM, axis=1)
                kh = pltpu.roll(kh, HEAD_DIM, axis=1)
            k_bias = pltpu.roll(pieces, HEAD_DIM - FOX_BIAS_PIECES * h, axis=1)
            q_ref[rows, dst] = jnp.where(feat, qh, q_bias).astype(BF16)
            k_ref[rows, dst] = jnp.where(feat, kh, jnp.where(bias_lanes, k_bias, 0.0)).astype(BF16)

    carry_ref[...] = jnp.broadcast_to(carry, carry_ref.shape)


def _in_proj(x2, g, w, qg, kg, fb, *, seq, tm, seg):
    T, D = x2.shape
    n_out = w.shape[1]
    widths = [seg[j + 1] - seg[j] for j in range(6)]
    row = lambda i: (i, 0)
    col = lambda i: (0, i)
    outs = [
        ((T, widths[0]), (tm, widths[0]), row, F32),
        ((T, 2 * widths[1]), (tm, 2 * widths[1]), row, BF16),
        ((T, 2 * widths[2]), (tm, 2 * widths[2]), row, BF16),
        ((widths[3], T), (widths[3], tm), col, BF16),
        ((T, widths[4]), (tm, widths[4]), row, F32),
    ]
    return pl.pallas_call(
        functools.partial(_in_proj_kernel, tiles_per_seq=seq // tm, seg=seg, sub=tm // 2),
        out_shape=tuple(jax.ShapeDtypeStruct(shape, dt) for shape, _, _, dt in outs),
        grid=(T // tm,),
        in_specs=[
            pl.BlockSpec((tm, D), row),
            _resident((1, D)),
            _resident((D, n_out)),
            _resident((1, widths[1])),
            _resident((1, widths[2])),
            _resident((1, widths[5])),
        ],
        out_specs=tuple(pl.BlockSpec(blk, imap) for _, blk, imap, _ in outs),
        scratch_shapes=[pltpu.VMEM((SUBLANES, widths[5]), F32)],
        compiler_params=pltpu.CompilerParams(
            dimension_semantics=("arbitrary",), vmem_limit_bytes=VMEM_LIMIT),
        name="in_proj",
    )(x2, g, w, qg, kg, fb)


def _shift_rows(x, d, fill, row):
    return jnp.where(row >= d, pltpu.roll(x, d, axis=0), fill)


def _lru_kernel(u_ref, cw_ref, cb_ref, wg_ref, bg_ref, lam_ref, ng_ref,
                y_ref, buf_ref, h_ref):
    j = pl.program_id(1)
    tb = u_ref.shape[0]
    w = u_ref.shape[1] // 2
    pad = SUBLANES

    @pl.when(j == 0)
    def _():
        buf_ref[0:pad, :] = jnp.zeros((pad, w), F32)
        h_ref[...] = jnp.zeros_like(h_ref)

    xa = u_ref[:, 0:w]
    ga = u_ref[:, w:2 * w]
    buf_ref[pad:pad + tb, :] = xa
    xc = cb_ref[...] + cw_ref[CONV_WIDTH - 1:CONV_WIDTH, :] * xa
    for d in range(1, CONV_WIDTH):
        xc = xc + cw_ref[CONV_WIDTH - 1 - d:CONV_WIDTH - d, :] * buf_ref[pad - d:pad - d + tb, :]
    buf_ref[0:pad, :] = xa[tb - pad:tb, :]

    gates = _dot(xc.astype(BF16), wg_ref[...]) + bg_ref[...]
    r = jax.nn.sigmoid(gates[:, 0:w])
    i = jax.nn.sigmoid(gates[:, w:2 * w])
    log_a = (-LRU_C) * r * _softplus(-lam_ref[...])
    a = jnp.exp(log_a)
    b = jnp.sqrt(jnp.tanh(-log_a) * (1.0 + a * a)) * (i * xc)

    row = lax.broadcasted_iota(jnp.int32, (tb, w), 0)
    d = 1
    while d < tb:
        a_sh = _shift_rows(a, d, 1.0, row)
        b_sh = _shift_rows(b, d, 0.0, row)
        b = a * b_sh + b
        a = a * a_sh
        d *= 2
    h = b + a * h_ref[0:1, :]
    h_ref[...] = jnp.broadcast_to(h[tb - 1:tb, :], h_ref.shape)

    y = jax.nn.gelu(ga) * h
    y_ref[...] = (_rms(y) * ng_ref[...]).astype(BF16)


def _lru(u, cw, cb, wg, bg, lam, ng, *, batch, seq, tb):
    T, w2 = u.shape
    w = w2 // 2
    nb = seq // tb
    return pl.pallas_call(
        _lru_kernel,
        out_shape=jax.ShapeDtypeStruct((T, w), BF16),
        grid=(batch, nb),
        in_specs=[
            pl.BlockSpec((tb, w2), lambda b, j: (b * nb + j, 0)),
            _resident((CONV_WIDTH, w)),
            _resident((1, w)),
            _resident((w, w2)),
            _resident((1, w2)),
            _resident((1, w)),
            _resident((1, w)),
        ],
        out_specs=pl.BlockSpec((tb, w), lambda b, j: (b * nb + j, 0)),
        scratch_shapes=[pltpu.VMEM((tb + SUBLANES, w), F32), pltpu.VMEM((SUBLANES, w), F32)],
        compiler_params=pltpu.CompilerParams(
            dimension_semantics=("parallel", "arbitrary"), vmem_limit_bytes=VMEM_LIMIT),
        name="lru",
    )(u, cw, cb, wg, bg, lam, ng)


def _fox_kernel(q_ref, k_ref, v_ref, o_ref, s_buf, mx_buf, m_ref, acc_ref, *, tq):
    i = pl.program_id(2)
    n_heads = q_ref.shape[1] // LANES
    causal = (lax.broadcasted_iota(jnp.int32, (tq, tq), 0)
              <= lax.broadcasted_iota(jnp.int32, (tq, tq), 1))

    def scores(j, slot, masked):
        k0 = pl.multiple_of(j * tq, tq)
        for u in range(n_heads):
            hl = slice(u * LANES, (u + 1) * LANES)
            s = _dot_nt(k_ref[pl.ds(k0, tq), hl], q_ref[:, hl])
            if masked:
                s = jnp.where(causal, s, NEG_BIG)
            s_buf[slot, u] = s
            mx_buf[slot, u] = jnp.max(s, axis=0, keepdims=True)

    def consume(j, slot):
        k0 = pl.multiple_of(j * tq, tq)
        for u in range(n_heads):
            m = m_ref[u]
            m_new = jnp.maximum(m, mx_buf[slot, u])
            alpha = jnp.exp2(m - m_new)
            p = jnp.exp2(s_buf[slot, u] - m_new)
            m_ref[u] = m_new
            vt = jnp.concatenate([v_ref[u * HEAD_DIM:(u + 1) * HEAD_DIM, pl.ds(k0, tq)],
                                  jnp.ones((FOX_DEN_ROWS, tq), BF16)], axis=0)
            acc_ref[u] = alpha * acc_ref[u] + _dot(vt, p.astype(BF16))

    m_ref[...] = jnp.full(m_ref.shape, NEG_BIG, F32)
    acc_ref[...] = jnp.zeros_like(acc_ref)

    def stage(j, slot, masked_next):
        scores(j + 1, 1 - slot, masked_next)
        consume(j, slot)

    @pl.when(i == 0)
    def _():
        scores(0, 0, True)
        consume(0, 0)

    @pl.when(i > 0)
    def _():
        scores(0, 0, False)

        def body(t, carry):
            stage(2 * t, 0, False)
            stage(2 * t + 1, 1, False)
            return carry

        lax.fori_loop(0, (i - 1) // 2, body, 0)

        @pl.when(i % 2 == 1)
        def _():
            stage(i - 1, 0, True)
            consume(i, 1)

        @pl.when(i % 2 == 0)
        def _():
            stage(i - 2, 0, False)
            stage(i - 1, 1, True)
            consume(i, 0)

    o_t = jnp.concatenate(
        [acc_ref[u, 0:HEAD_DIM, :] / acc_ref[u, HEAD_DIM:HEAD_DIM + 1, :] for u in range(n_heads)],
        axis=0)
    o_ref[...] = o_t.T.astype(BF16)


def _fox(q_aug, k_aug, v_t, *, batch, seq, tq):
    T, wide = q_aug.shape
    heads_per_step = 2
    blk = heads_per_step * LANES
    nq = seq // tq
    steps = wide // blk
    return pl.pallas_call(
        functools.partial(_fox_kernel, tq=tq),
        out_shape=jax.ShapeDtypeStruct((T, steps * heads_per_step * HEAD_DIM), BF16),
        grid=(batch, steps, nq),
        in_specs=[
            pl.BlockSpec((tq, blk), lambda b, h, i: (b * nq + i, h)),
            pl.BlockSpec((seq, blk), lambda b, h, i: (b, h)),
            pl.BlockSpec((heads_per_step * HEAD_DIM, seq), lambda b, h, i: (h, b)),
        ],
        out_specs=pl.BlockSpec((tq, heads_per_step * HEAD_DIM), lambda b, h, i: (b * nq + i, h)),
        scratch_shapes=[
            pltpu.VMEM((2, heads_per_step, tq, tq), F32),
            pltpu.VMEM((2, heads_per_step, 1, tq), F32),
            pltpu.VMEM((heads_per_step, 1, tq), F32),
            pltpu.VMEM((heads_per_step, HEAD_DIM + FOX_DEN_ROWS, tq), F32),
        ],
        compiler_params=pltpu.CompilerParams(
            dimension_semantics=("parallel", "parallel", "arbitrary"),
            vmem_limit_bytes=VMEM_LIMIT),
        name="fox",
    )(q_aug, k_aug, v_t)


def _stack_heads(x, lane_head, n_heads):
    zero = jnp.zeros_like(x)
    return jnp.concatenate([jnp.where(lane_head == h, x, zero) for h in range(n_heads)], axis=0)


def _rwkv_kernel(p_ref, mu_ref, w0_ref, w2_ref, a0_ref, a2_ref, g2_ref, kk_ref, ka_ref,
                 rk_ref, lng_ref, lnb_ref, y_ref, prev_ref, st_ref):
    j = pl.program_id(1)
    tb, cols = p_ref.shape
    w = y_ref.shape[1]
    n_heads = w // HEAD_DIM
    C = CHUNK
    chunk_shift = CHUNK.bit_length() - 1

    @pl.when(j == 0)
    def _():
        prev_ref[...] = jnp.zeros_like(prev_ref)
        st_ref[...] = jnp.zeros_like(st_ref)

    p = p_ref[...]
    row = lax.broadcasted_iota(jnp.int32, (tb, cols), 0)
    prev = jnp.where(row == 0, prev_ref[0:1, :], pltpu.roll(p, 1, axis=0))
    prev_ref[...] = jnp.broadcast_to(p[tb - 1:tb, :], prev_ref.shape)
    ps = p + (prev - p) * mu_ref[...]
    r = ps[:, 0:w]
    k = ps[:, w:2 * w]
    v = ps[:, 2 * w:3 * w]
    lr = ps[:, 3 * w:cols]

    wpre = w0_ref[...] + _dot(jnp.tanh(lr).astype(BF16), w2_ref[...])
    lw = -jnp.exp(-_softplus(-wpre) - 0.5)
    a = jax.nn.sigmoid(a0_ref[...] + _dot(lr.astype(BF16), a2_ref[...]))
    g = _dot(jax.nn.sigmoid(lr).astype(BF16), g2_ref[...])

    ones = _head_ones(w)
    kk = k * kk_ref[...]
    kk = kk / jnp.maximum(jnp.sqrt(_split_dot_right(kk * kk, ones, 2)), 1e-12)
    k2 = k * (1.0 + (a - 1.0) * ka_ref[...])
    b = kk * a
    bonus = _split_dot_right(r * k2 * rk_ref[...], ones, 2) * v

    rr = lax.broadcasted_iota(jnp.int32, (tb, tb), 0)
    cc = lax.broadcasted_iota(jnp.int32, (tb, tb), 1)
    same = (rr >> chunk_shift) == (cc >> chunk_shift)
    tri = jnp.where(same & (cc <= rr), 1.0, 0.0).astype(BF16)
    tot = jnp.where(same, 1.0, 0.0).astype(BF16)
    LW = _split_dot_left(tri, lw, 3)
    LT = _split_dot_left(tot, lw, 3)
    e_out = jnp.exp(-LW)
    e_end = jnp.exp(LT - LW)
    at = (-kk) * jnp.exp(LW - lw)
    rt = r * jnp.exp(LW)
    bt = b * e_out
    kt = k2 * e_out
    b_end = b * e_end
    k_end = k2 * e_end
    w_tot = jnp.exp(LT)

    lane_head = lax.broadcasted_iota(jnp.int32, (C, w), 1) >> HEAD_SHIFT
    t_idx = lax.broadcasted_iota(jnp.int32, (C, w), 0)
    j_idx = lax.broadcasted_iota(jnp.int32, (C, w), 1) & (HEAD_DIM - 1)
    strict = j_idx < t_idx
    incl = j_idx <= t_idx
    eye = jnp.where(j_idx == t_idx, 1.0, 0.0)
    sq_r = lax.broadcasted_iota(jnp.int32, (w, w), 0) >> HEAD_SHIFT
    sq_c = lax.broadcasted_iota(jnp.int32, (w, w), 1) >> HEAD_SHIFT
    block_diag = sq_r == sq_c
    stack = functools.partial(_stack_heads, lane_head=lane_head, n_heads=n_heads)

    outs = []
    for c in range(tb // C):
        sl = slice(c * C, (c + 1) * C)
        ar = jnp.concatenate([at[sl], rt[sl]], axis=0).astype(BF16)
        bk = jnp.concatenate([stack(bt[sl].astype(BF16)), stack(kt[sl].astype(BF16))], axis=0)
        A = _dot_nt(ar, bk)
        a_ab = jnp.where(strict, A[0:C, 0:w], 0.0)
        a_ak = jnp.where(strict, A[0:C, w:2 * w], 0.0)
        a_rb = jnp.where(incl, A[C:2 * C, 0:w], 0.0)
        a_rk = jnp.where(incl, A[C:2 * C, w:2 * w], 0.0)

        pw = a_ab
        tinv = eye + a_ab
        sz = 1
        while 2 * sz < C:
            pw_b = stack(pw.astype(BF16))
            pw = _dot(pw.astype(BF16), pw_b)
            tinv = tinv + _dot(tinv.astype(BF16), stack(pw.astype(BF16)))
            sz *= 2

        v_c = v[sl]
        vs = stack(v_c.astype(BF16))
        yo = _dot(jnp.concatenate([a_ak, a_rk], axis=0).astype(BF16), vs)
        st = st_ref[...]
        xo = _dot_nt(ar, st.astype(BF16))
        x = xo[0:C] + yo[0:C]
        u = _dot(tinv.astype(BF16), stack(x.astype(BF16)))
        o = xo[C:2 * C] + yo[C:2 * C] + _dot(a_rb.astype(BF16), stack(u.astype(BF16)))
        uv = jnp.concatenate([u, v_c], axis=0).astype(BF16)
        bke = jnp.concatenate([b_end[sl], k_end[sl]], axis=0).astype(BF16)
        upd = _dot_tn(uv, bke)
        st_ref[...] = st * w_tot[c * C:c * C + 1, :] + jnp.where(block_diag, upd, 0.0)
        outs.append(o)

    o = jnp.concatenate(outs, axis=0)
    inv_d = 1.0 / HEAD_DIM
    mean = _split_dot_right(o, ones, 2) * inv_d
    cen = o - mean
    var = _split_dot_right(cen * cen, ones, 2) * inv_d
    o = cen * lax.rsqrt(var + GN_EPS) * lng_ref[...] + lnb_ref[...]
    y_ref[...] = ((o + bonus) * g).astype(BF16)


def _rwkv(p, mu, w0, w2, a0, a2, g2, kkp, ka, rk, lng, lnb, *, batch, seq, tb):
    T, cols = p.shape
    w = w0.shape[1]
    nb = seq // tb
    vec = _resident((1, w))
    low = _resident((cols - 3 * w, w))
    return pl.pallas_call(
        _rwkv_kernel,
        out_shape=jax.ShapeDtypeStruct((T, w), BF16),
        grid=(batch, nb),
        in_specs=[
            pl.BlockSpec((tb, cols), lambda b, j: (b * nb + j, 0)),
            _resident((1, cols)), vec, low, vec, low, low, vec, vec, vec, vec, vec,
        ],
        out_specs=pl.BlockSpec((tb, w), lambda b, j: (b * nb + j, 0)),
        scratch_shapes=[pltpu.VMEM((SUBLANES, cols), F32), pltpu.VMEM((w, w), F32)],
        compiler_params=pltpu.CompilerParams(
            dimension_semantics=("parallel", "arbitrary"), vmem_limit_bytes=VMEM_LIMIT),
        name="rwkv",
    )(p, mu, w0, w2, a0, a2, g2, kkp, ka, rk, lng, lnb)


def _out_proj_kernel(ya_ref, yb_ref, yc_ref, x_ref, wa_ref, wb_ref, wc_ref, fg_ref, n2_ref,
                     wr_ref, br_ref, x1_ref, h2_ref, comb_ref, *, sub):
    for r0 in range(0, x_ref.shape[0], sub):
        rows = slice(r0, r0 + sub)
        ybn = (_rms(yb_ref[rows, :].astype(F32)) * fg_ref[...]).astype(BF16)
        x1 = (x_ref[rows, :] + _dot(ya_ref[rows, :], wa_ref[...]) + _dot(ybn, wb_ref[...])
              + _dot(yc_ref[rows, :], wc_ref[...]))
        x1_ref[rows, :] = x1
        h2 = _rms(x1) * n2_ref[...]
        h_hi = h2.astype(BF16)
        h2_ref[rows, :] = h_hi

        h_lo = (h2 - h_hi.astype(F32)).astype(BF16)
        hw = _dot(h_hi, wr_ref[...])
        logits = (hw[:, 0:LANES] + hw[:, LANES:2 * LANES] + _dot(h_lo, wr_ref[:, 0:LANES])
                  + br_ref[...])
        comb_ref[rows, :] = _route(logits)


def _route(logits):
    lane = lax.broadcasted_iota(jnp.int32, logits.shape, 1)
    lane_f = lane.astype(F32)
    far = float(LANES)

    def first_argmax(vals, vmax):
        return jnp.min(jnp.where(vals == vmax, lane_f, far), axis=-1, keepdims=True)

    gl = jnp.where(lane < N_GROUPS, logits, NEG_BIG)
    gmax = jnp.max(gl, axis=-1, keepdims=True)
    gidx = first_argmax(gl, gmax)
    g_p = 1.0 / jnp.sum(jnp.exp(gl - gmax), axis=-1, keepdims=True)

    e_group = ((lane - N_GROUPS) >> 2).astype(F32)
    el = jnp.where((lane >= N_GROUPS) & (e_group == gidx), logits, NEG_BIG)
    emax = jnp.max(el, axis=-1, keepdims=True)
    esum = jnp.sum(jnp.exp(el - emax), axis=-1, keepdims=True)
    i1 = first_argmax(el, emax)
    el2 = jnp.where(lane_f == i1, NEG_BIG, el)
    emax2 = jnp.max(el2, axis=-1, keepdims=True)
    i2 = first_argmax(el2, emax2)
    p1 = 1.0 / esum
    p2 = jnp.exp(emax2 - emax) / esum
    den = p1 + p2
    return (jnp.where(lane_f == i1, g_p * (p1 / den), 0.0)
            + jnp.where(lane_f == i2, g_p * (p2 / den), 0.0))


def _out_proj(ya, yb, yc, x2, wa, wb, wc, fg, n2, wr, br, *, tm):
    T, D = x2.shape
    row = lambda i: (i, 0)
    return pl.pallas_call(
        functools.partial(_out_proj_kernel, sub=tm // 2),
        out_shape=(jax.ShapeDtypeStruct((T, D), F32), jax.ShapeDtypeStruct((T, D), BF16),
                   jax.ShapeDtypeStruct((T, LANES), F32)),
        grid=(T // tm,),
        in_specs=[
            pl.BlockSpec((tm, ya.shape[1]), row),
            pl.BlockSpec((tm, yb.shape[1]), row),
            pl.BlockSpec((tm, yc.shape[1]), row),
            pl.BlockSpec((tm, D), row),
            _resident(wa.shape), _resident(wb.shape), _resident(wc.shape),
            _resident(fg.shape), _resident(n2.shape), _resident(wr.shape), _resident(br.shape),
        ],
        out_specs=(pl.BlockSpec((tm, D), row), pl.BlockSpec((tm, D), row),
                   pl.BlockSpec((tm, LANES), row)),
        compiler_params=pltpu.CompilerParams(
            dimension_semantics=("parallel",), vmem_limit_bytes=VMEM_LIMIT),
        name="out_proj",
    )(ya, yb, yc, x2, wa, wb, wc, fg, n2, wr, br)


def _moe_kernel(h_ref, comb_ref, x_ref, wgu_ref, wd_ref, o_ref, hid_ref):
    h = h_ref[...]
    comb = comb_ref[...]
    n_exp, _, two_f = wgu_ref.shape
    f = two_f // 2
    for e in range(n_exp):
        gu = _dot(h, wgu_ref[e])
        gate = gu[:, 0:f]
        hid = gate * jax.nn.sigmoid(gate) * gu[:, f:two_f]
        hid = hid * comb[:, N_GROUPS + e:N_GROUPS + e + 1]
        hid_ref[:, e * f:(e + 1) * f] = hid.astype(BF16)
    o_ref[...] = x_ref[...] + _dot(hid_ref[...], wd_ref[...])


def _moe(h2, comb, x1, wgu, wd, *, tm):
    T, D = x1.shape
    row = lambda i: (i, 0)
    return pl.pallas_call(
        _moe_kernel,
        out_shape=jax.ShapeDtypeStruct((T, D), F32),
        grid=(T // tm,),
        in_specs=[
            pl.BlockSpec((tm, D), row),
            pl.BlockSpec((tm, LANES), row),
            pl.BlockSpec((tm, D), row),
            _resident(wgu.shape),
            _resident(wd.shape),
        ],
        out_specs=pl.BlockSpec((tm, D), row),
        scratch_shapes=[pltpu.VMEM((tm, wd.shape[0]), BF16)],
        compiler_params=pltpu.CompilerParams(
            dimension_semantics=("parallel",), vmem_limit_bytes=VMEM_LIMIT),
        name="moe",
    )(h2, comb, x1, wgu, wd)


def _block_diag(wh):
    n_heads, d, _ = wh.shape
    eye = jnp.eye(n_heads, dtype=wh.dtype)
    return jnp.einsum("hij,hg->higj", wh, eye).reshape(n_heads * d, n_heads * d)


def _row(vec):
    return vec.reshape(1, -1).astype(F32)


def _pad_rows(mat, start, total):
    return jnp.zeros((total, mat.shape[1]), mat.dtype).at[start:start + mat.shape[0]].set(mat)


def kernel(x, norm1_g, w_in, conv_w, conv_b, lru_wa, lru_ba, lru_wx, lru_bx, lru_lambda,
           lru_norm_g, fox_fb, fox_qnorm_g, fox_knorm_g, fox_norm_g, rwkv_mu, rwkv_w0,
           rwkv_w2, rwkv_a0, rwkv_a2, rwkv_g2, rwkv_kk, rwkv_ka, rwkv_rk, rwkv_ln_g,
           rwkv_ln_b, w_out, norm2_g, router_gw, router_gb, router_ew, router_eb,
           exp_w_gate, exp_w_up, exp_w_down):
    batch, seq, d_model = x.shape
    depth = w_in.shape[0]
    lru_w = conv_w.shape[2]
    fox_heads = fox_fb.shape[1]
    fox_w = fox_heads * HEAD_DIM
    rwkv_w = rwkv_w0.shape[1]
    rwkv_cols = rwkv_mu.shape[1]
    d_rank, a_rank, g_rank = rwkv_w2.shape[1], rwkv_a2.shape[1], rwkv_g2.shape[1]
    low = d_rank + a_rank + g_rank
    n_exp, _, d_exp = exp_w_gate.shape[1:]
    assert low == LANES and rwkv_cols == 3 * rwkv_w + low
    assert fox_heads <= LANES and n_exp == N_EXPERTS

    o_fox = 2 * lru_w
    o_fl = o_fox + 3 * fox_w
    o_rwkv = o_fl + fox_heads
    seg = (0, o_fox, o_fox + fox_w, o_fox + 2 * fox_w, o_fl, o_fl + rwkv_cols,
           o_fl + rwkv_cols + LANES)

    x2 = x.reshape(batch * seq, d_model)
    for l in range(depth):
        wl = w_in[l]
        n_fl = FOX_BIAS_PIECES * fox_heads
        w_perm = jnp.concatenate(
            [wl[:, 0:o_fl], wl[:, o_rwkv:o_rwkv + rwkv_cols],
             jnp.repeat(wl[:, o_fl:o_rwkv], FOX_BIAS_PIECES, axis=1),
             jnp.zeros((d_model, LANES - n_fl), wl.dtype)], axis=1).astype(BF16)
        qg = _row(jnp.tile(fox_qnorm_g[l], fox_heads) * (HEAD_DIM ** -0.5 * LOG2E))
        kg = _row(jnp.tile(fox_knorm_g[l], fox_heads))
        fb = _row(jnp.pad(jnp.repeat(fox_fb[l], FOX_BIAS_PIECES), (0, LANES - n_fl)))
        u, q, k_aug, v, p = _in_proj(x2, _row(norm1_g[l]), w_perm, qg, kg, fb,
                                    seq=seq, tm=512, seg=seg)

        wg = jnp.concatenate([_block_diag(lru_wa[l]), _block_diag(lru_wx[l])], axis=1).astype(BF16)
        bg = _row(jnp.concatenate([lru_ba[l], lru_bx[l]]))
        ya = _lru(u, conv_w[l], _row(conv_b[l]), wg, bg, _row(lru_lambda[l]),
                  _row(lru_norm_g[l]), batch=batch, seq=seq, tb=512)

        yb = _fox(q, k_aug, v, batch=batch, seq=seq, tq=512)

        yc = _rwkv(p, _row(rwkv_mu[l]), _row(rwkv_w0[l]),
                   _pad_rows(rwkv_w2[l], 0, low).astype(BF16), _row(rwkv_a0[l]),
                   _pad_rows(rwkv_a2[l], d_rank, low).astype(BF16),
                   _pad_rows(rwkv_g2[l], d_rank + a_rank, low).astype(BF16),
                   _row(rwkv_kk[l]), _row(rwkv_ka[l]), _row(rwkv_rk[l]),
                   _row(rwkv_ln_g[l]), _row(rwkv_ln_b[l]), batch=batch, seq=seq, tb=256)

        wo = w_out[l].astype(BF16)
        wr = jnp.concatenate(
            [router_gw[l], router_ew[l],
             jnp.zeros((d_model, LANES - N_GROUPS - n_exp), F32)], axis=1)
        wr_hi = wr.astype(BF16)
        wr = jnp.concatenate([wr_hi, (wr - wr_hi.astype(F32)).astype(BF16)], axis=1)
        br = _row(jnp.pad(jnp.concatenate([router_gb[l], router_eb[l]]),
                          (0, LANES - N_GROUPS - n_exp)))
        x1, h2, comb = _out_proj(
            ya, yb, yc, x2, wo[0:lru_w], wo[lru_w:lru_w + fox_w], wo[lru_w + fox_w:],
            _row(fox_norm_g[l]), _row(norm2_g[l]), wr, br, tm=512)

        wgu = jnp.concatenate([exp_w_gate[l], exp_w_up[l]], axis=2).astype(BF16)
        wd = exp_w_down[l].reshape(n_exp * d_exp, d_model).astype(BF16)
        x2 = _moe(h2, comb, x1, wgu, wd, tm=512)
    return x2.reshape(batch, seq, d_model)
```

```python
import functools

import jax
import jax.numpy as jnp
from jax import lax
from jax.experimental import pallas as pl
from jax.experimental.pallas import tpu as pltpu

F32 = jnp.float32
BF16 = jnp.bfloat16

NORM_EPS = 1e-6
GN_EPS = 64e-5
LRU_C = 8.0
HEAD_DIM = 64
HEAD_SHIFT = 6
LANES = 128
SUBLANES = 8
CHUNK = 64
CONV_WIDTH = 4
N_GROUPS = 4
EXPERTS_PER_GROUP = 4
N_EXPERTS = N_GROUPS * EXPERTS_PER_GROUP
NEG_BIG = -1e30
LOG2E = 1.4426950408889634
FOX_BIAS_PIECES = 3
FOX_DEN_ROWS = 16
VMEM_LIMIT = 56 * 1024 * 1024


def _dot(a, b):
    return jnp.dot(a, b, preferred_element_type=F32)


def _dot_nt(a, b):
    return lax.dot_general(a, b, (((1,), (1,)), ((), ())), preferred_element_type=F32)


def _dot_tn(a, b):
    return lax.dot_general(a, b, (((0,), (0,)), ((), ())), preferred_element_type=F32)


def _split_dot_right(x, ones, parts):
    acc = None
    rem = x
    for _ in range(parts):
        hi = rem.astype(BF16)
        t = _dot(hi, ones)
        acc = t if acc is None else acc + t
        rem = rem - hi.astype(F32)
    return acc


def _split_dot_left(ones, x, parts):
    acc = None
    rem = x
    for _ in range(parts):
        hi = rem.astype(BF16)
        t = _dot(ones, hi)
        acc = t if acc is None else acc + t
        rem = rem - hi.astype(F32)
    return acc


def _head_ones(n):
    r = lax.broadcasted_iota(jnp.int32, (n, n), 0) >> HEAD_SHIFT
    c = lax.broadcasted_iota(jnp.int32, (n, n), 1) >> HEAD_SHIFT
    return jnp.where(r == c, 1.0, 0.0).astype(BF16)


def _softplus(z):
    return jnp.maximum(z, 0.0) + jnp.log(1.0 + jnp.exp(-jnp.abs(z)))


def _rms(x, eps=NORM_EPS):
    return x * lax.rsqrt(jnp.mean(x * x, axis=-1, keepdims=True) + eps)


def _resident(shape):
    zeros = (0,) * len(shape)
    return pl.BlockSpec(shape, lambda *_: zeros, pipeline_mode=pl.Buffered(1))


def _in_proj_kernel(x_ref, g_ref, w_ref, qg_ref, kg_ref, fb_ref,
                    lru_ref, q_ref, k_ref, v_ref, p_ref, carry_ref,
                    *, tiles_per_seq, seg, sub):
    i = pl.program_id(0)

    @pl.when(i % tiles_per_seq == 0)
    def _():
        carry_ref[...] = jnp.zeros_like(carry_ref)

    fox = seg[2] - seg[1]
    n_heads = fox // HEAD_DIM
    ones = _head_ones(fox)
    inv_d = 1.0 / HEAD_DIM
    rr = lax.broadcasted_iota(jnp.int32, (sub, sub), 0)
    cc = lax.broadcasted_iota(jnp.int32, (sub, sub), 1)
    tri = jnp.where(cc <= rr, 1.0, 0.0).astype(BF16)
    lane = lax.broadcasted_iota(jnp.int32, (sub, LANES), 1)
    feat = lane < HEAD_DIM
    bias_lanes = (lane >= HEAD_DIM) & (lane < HEAD_DIM + FOX_BIAS_PIECES)
    q_bias = jnp.where(bias_lanes, 1.0, 0.0)
    piece_id = lax.rem(lane, FOX_BIAS_PIECES)
    carry = carry_ref[0:1, :]

    def head_norm(t):
        return t * lax.rsqrt(_dot((t * t).astype(BF16), ones) * inv_d + NORM_EPS)

    for r0 in range(0, x_ref.shape[0], sub):
        rows = slice(r0, r0 + sub)
        hb = (_rms(x_ref[rows, :]) * g_ref[...]).astype(BF16)
        lru_ref[rows, :] = _dot(hb, w_ref[:, seg[0]:seg[1]])
        q = head_norm(_dot(hb, w_ref[:, seg[1]:seg[2]])) * qg_ref[...]
        k = head_norm(_dot(hb, w_ref[:, seg[2]:seg[3]])) * kg_ref[...]
        v_ref[:, rows] = _dot(hb, w_ref[:, seg[3]:seg[4]]).T.astype(BF16)
        p_ref[rows, :] = _dot(hb, w_ref[:, seg[4]:seg[5]])

        lf = -_softplus(-(_dot(hb, w_ref[:, seg[5]:seg[6]]) + fb_ref[...]))
        c = _split_dot_left(tri, lf, 3) + carry
        carry = c[sub - 1:sub, :]

        rem = c * (-LOG2E)
        pieces = None
        for piece in range(FOX_BIAS_PIECES):
            hi = rem.astype(BF16).astype(F32)
            pieces = hi if pieces is None else jnp.where(piece_id == piece, hi, pieces)
            rem = rem - hi

        for h in range(n_heads):
            src = slice((h // 2) * LANES, (h // 2 + 1) * LANES)
            dst = slice(h * LANES, (h + 1) * LANES)
            qh, kh = q[:, src], k[:, src]
            if h % 2:
                qh = pltpu.roll(qh, HEAD_DIM, axis=1)
                kh = pltpu.roll(kh, HEAD_DIM, axis=1)
            k_bias = pltpu.roll(pieces, HEAD_DIM - FOX_BIAS_PIECES * h, axis=1)
            q_ref[rows, dst] = jnp.where(feat, qh, q_bias).astype(BF16)
            k_ref[rows, dst] = jnp.where(feat, kh, jnp.where(bias_lanes, k_bias, 0.0)).astype(BF16)

    carry_ref[...] = jnp.broadcast_to(carry, carry_ref.shape)


def _in_proj(x2, g, w, qg, kg, fb, *, seq, tm, seg):
    T, D = x2.shape
    n_out = w.shape[1]
    widths = [seg[j + 1] - seg[j] for j in range(6)]
    row = lambda i: (i, 0)
    col = lambda i: (0, i)
    outs = [
        ((T, widths[0]), (tm, widths[0]), row, F32),
        ((T, 2 * widths[1]), (tm, 2 * widths[1]), row, BF16),
        ((T, 2 * widths[2]), (tm, 2 * widths[2]), row, BF16),
        ((widths[3], T), (widths[3], tm), col, BF16),
        ((T, widths[4]), (tm, widths[4]), row, F32),
    ]
    return pl.pallas_call(
        functools.partial(_in_proj_kernel, tiles_per_seq=seq // tm, seg=seg, sub=tm // 2),
        out_shape=tuple(jax.ShapeDtypeStruct(shape, dt) for shape, _, _, dt in outs),
        grid=(T // tm,),
        in_specs=[
            pl.BlockSpec((tm, D), row),
            _resident((1, D)),
            _resident((D, n_out)),
            _resident((1, widths[1])),
            _resident((1, widths[2])),
            _resident((1, widths[5])),
        ],
        out_specs=tuple(pl.BlockSpec(blk, imap) for _, blk, imap, _ in outs),
        scratch_shapes=[pltpu.VMEM((SUBLANES, widths[5]), F32)],
        compiler_params=pltpu.CompilerParams(
            dimension_semantics=("arbitrary",), vmem_limit_bytes=VMEM_LIMIT),
        name="in_proj",
    )(x2, g, w, qg, kg, fb)


def _shift_rows(x, d, fill, row):
    return jnp.where(row >= d, pltpu.roll(x, d, axis=0), fill)


def _lru_kernel(u_ref, cw_ref, cb_ref, wg_ref, bg_ref, lam_ref, ng_ref,
                y_ref, buf_ref, h_ref):
    j = pl.program_id(1)
    tb = u_ref.shape[0]
    w = u_ref.shape[1] // 2
    pad = SUBLANES

    @pl.when(j == 0)
    def _():
        buf_ref[0:pad, :] = jnp.zeros((pad, w), F32)
        h_ref[...] = jnp.zeros_like(h_ref)

    xa = u_ref[:, 0:w]
    ga = u_ref[:, w:2 * w]
    buf_ref[pad:pad + tb, :] = xa
    xc = cb_ref[...] + cw_ref[CONV_WIDTH - 1:CONV_WIDTH, :] * xa
    for d in range(1, CONV_WIDTH):
        xc = xc + cw_ref[CONV_WIDTH - 1 - d:CONV_WIDTH - d, :] * buf_ref[pad - d:pad - d + tb, :]
    buf_ref[0:pad, :] = xa[tb - pad:tb, :]

    gates = _dot(xc.astype(BF16), wg_ref[...]) + bg_ref[...]
    r = jax.nn.sigmoid(gates[:, 0:w])
    i = jax.nn.sigmoid(gates[:, w:2 * w])
    log_a = (-LRU_C) * r * _softplus(-lam_ref[...])
    a = jnp.exp(log_a)
    b = jnp.sqrt(jnp.tanh(-log_a) * (1.0 + a * a)) * (i * xc)

    row = lax.broadcasted_iota(jnp.int32, (tb, w), 0)
    d = 1
    while d < tb:
        a_sh = _shift_rows(a, d, 1.0, row)
        b_sh = _shift_rows(b, d, 0.0, row)
        b = a * b_sh + b
        a = a * a_sh
        d *= 2
    h = b + a * h_ref[0:1, :]
    h_ref[...] = jnp.broadcast_to(h[tb - 1:tb, :], h_ref.shape)

    y = jax.nn.gelu(ga) * h
    y_ref[...] = (_rms(y) * ng_ref[...]).astype(BF16)


def _lru(u, cw, cb, wg, bg, lam, ng, *, batch, seq, tb):
    T, w2 = u.shape
    w = w2 // 2
    nb = seq // tb
    return pl.pallas_call(
        _lru_kernel,
        out_shape=jax.ShapeDtypeStruct((T, w), BF16),
        grid=(batch, nb),
        in_specs=[
            pl.BlockSpec((tb, w2), lambda b, j: (b * nb + j, 0)),
            _resident((CONV_WIDTH, w)),
            _resident((1, w)),
            _resident((w, w2)),
            _resident((1, w2)),
            _resident((1, w)),
            _resident((1, w)),
        ],
        out_specs=pl.BlockSpec((tb, w), lambda b, j: (b * nb + j, 0)),
        scratch_shapes=[pltpu.VMEM((tb + SUBLANES, w), F32), pltpu.VMEM((SUBLANES, w), F32)],
        compiler_params=pltpu.CompilerParams(
            dimension_semantics=("parallel", "arbitrary"), vmem_limit_bytes=VMEM_LIMIT),
        name="lru",
    )(u, cw, cb, wg, bg, lam, ng)


def _fox_kernel(q_ref, k_ref, v_ref, o_ref, s_buf, mx_buf, m_ref, acc_ref, *, tq):
    i = pl.program_id(2)
    n_heads = q_ref.shape[1] // LANES
    causal = (lax.broadcasted_iota(jnp.int32, (tq, tq), 0)
              <= lax.broadcasted_iota(jnp.int32, (tq, tq), 1))

    all_heads = tuple(range(n_heads))

    def scores(j, slot, masked, heads=all_heads):
        k0 = pl.multiple_of(j * tq, tq)
        for u in heads:
            hl = slice(u * LANES, (u + 1) * LANES)
            s = _dot_nt(k_ref[pl.ds(k0, tq), hl], q_ref[:, hl])
            if masked:
                s = jnp.where(causal, s, NEG_BIG)
            s_buf[slot, u] = s
            mx_buf[slot, u] = jnp.max(s, axis=0, keepdims=True)

    def consume(j, slot, heads=all_heads):
        k0 = pl.multiple_of(j * tq, tq)
        for u in heads:
            m = m_ref[u]
            m_new = jnp.maximum(m, mx_buf[slot, u])
            alpha = jnp.exp2(m - m_new)
            p = jnp.exp2(s_buf[slot, u] - m_new)
            m_ref[u] = m_new
            vt = jnp.concatenate([v_ref[u * HEAD_DIM:(u + 1) * HEAD_DIM, pl.ds(k0, tq)],
                                  jnp.ones((FOX_DEN_ROWS, tq), BF16)], axis=0)
            acc_ref[u] = alpha * acc_ref[u] + _dot(vt, p.astype(BF16))

    m_ref[...] = jnp.full(m_ref.shape, NEG_BIG, F32)
    acc_ref[...] = jnp.zeros_like(acc_ref)

    def stage(j, slot, masked_next):
        for u in all_heads:
            scores(j + 1, 1 - slot, masked_next, (u,))
            consume(j, slot, (u,))

    @pl.when(i == 0)
    def _():
        scores(0, 0, True)
        consume(0, 0)

    @pl.when(i > 0)
    def _():
        scores(0, 0, False)

        def body(t, carry):
            stage(2 * t, 0, False)
            stage(2 * t + 1, 1, False)
            return carry

        lax.fori_loop(0, (i - 1) // 2, body, 0)

        @pl.when(i % 2 == 1)
        def _():
            stage(i - 1, 0, True)
            consume(i, 1)

        @pl.when(i % 2 == 0)
        def _():
            stage(i - 2, 0, False)
            stage(i - 1, 1, True)
            consume(i, 0)

    o_t = jnp.concatenate(
        [acc_ref[u, 0:HEAD_DIM, :] / acc_ref[u, HEAD_DIM:HEAD_DIM + 1, :] for u in range(n_heads)],
        axis=0)
    o_ref[...] = o_t.T.astype(BF16)


def _fox(q_aug, k_aug, v_t, *, batch, seq, tq):
    T, wide = q_aug.shape
    heads_per_step = 2
    blk = heads_per_step * LANES
    nq = seq // tq
    steps = wide // blk
    return pl.pallas_call(
        functools.partial(_fox_kernel, tq=tq),
        out_shape=jax.ShapeDtypeStruct((T, steps * heads_per_step * HEAD_DIM), BF16),
        grid=(batch, steps, nq),
        in_specs=[
            pl.BlockSpec((tq, blk), lambda b, h, i: (b * nq + i, h)),
            pl.BlockSpec((seq, blk), lambda b, h, i: (b, h)),
            pl.BlockSpec((heads_per_step * HEAD_DIM, seq), lambda b, h, i: (h, b)),
        ],
        out_specs=pl.BlockSpec((tq, heads_per_step * HEAD_DIM), lambda b, h, i: (b * nq + i, h)),
        scratch_shapes=[
            pltpu.VMEM((2, heads_per_step, tq, tq), F32),
            pltpu.VMEM((2, heads_per_step, 1, tq), F32),
            pltpu.VMEM((heads_per_step, 1, tq), F32),
            pltpu.VMEM((heads_per_step, HEAD_DIM + FOX_DEN_ROWS, tq), F32),
        ],
        compiler_params=pltpu.CompilerParams(
            dimension_semantics=("parallel", "parallel", "arbitrary"),
            vmem_limit_bytes=VMEM_LIMIT),
        name="fox",
    )(q_aug, k_aug, v_t)


def _stack_heads(x, lane_head, n_heads):
    zero = jnp.zeros_like(x)
    return jnp.concatenate([jnp.where(lane_head == h, x, zero) for h in range(n_heads)], axis=0)


def _rwkv_kernel(p_ref, mu_ref, w0_ref, w2_ref, a0_ref, a2_ref, g2_ref, kk_ref, ka_ref,
                 rk_ref, lng_ref, lnb_ref, y_ref, prev_ref, st_ref):
    j = pl.program_id(1)
    tb, cols = p_ref.shape
    w = y_ref.shape[1]
    n_heads = w // HEAD_DIM
    C = CHUNK
    chunk_shift = CHUNK.bit_length() - 1

    @pl.when(j == 0)
    def _():
        prev_ref[...] = jnp.zeros_like(prev_ref)
        st_ref[...] = jnp.zeros_like(st_ref)

    p = p_ref[...]
    row = lax.broadcasted_iota(jnp.int32, (tb, cols), 0)
    prev = jnp.where(row == 0, prev_ref[0:1, :], pltpu.roll(p, 1, axis=0))
    prev_ref[...] = jnp.broadcast_to(p[tb - 1:tb, :], prev_ref.shape)
    ps = p + (prev - p) * mu_ref[...]
    r = ps[:, 0:w]
    k = ps[:, w:2 * w]
    v = ps[:, 2 * w:3 * w]
    lr = ps[:, 3 * w:cols]

    wpre = w0_ref[...] + _dot(jnp.tanh(lr).astype(BF16), w2_ref[...])
    lw = -jnp.exp(-_softplus(-wpre) - 0.5)
    a = jax.nn.sigmoid(a0_ref[...] + _dot(lr.astype(BF16), a2_ref[...]))
    g = _dot(jax.nn.sigmoid(lr).astype(BF16), g2_ref[...])

    ones = _head_ones(w)
    kk = k * kk_ref[...]
    kk = kk / jnp.maximum(jnp.sqrt(_split_dot_right(kk * kk, ones, 1)), 1e-12)
    k2 = k * (1.0 + (a - 1.0) * ka_ref[...])
    b = kk * a
    bonus = _split_dot_right(r * k2 * rk_ref[...], ones, 1) * v

    rr = lax.broadcasted_iota(jnp.int32, (tb, tb), 0)
    cc = lax.broadcasted_iota(jnp.int32, (tb, tb), 1)
    same = (rr >> chunk_shift) == (cc >> chunk_shift)
    tri = jnp.where(same & (cc <= rr), 1.0, 0.0).astype(BF16)
    tot = jnp.where(same, 1.0, 0.0).astype(BF16)
    LW = _split_dot_left(tri, lw, 3)
    LT = _split_dot_left(tot, lw, 3)
    e_out = jnp.exp(-LW)
    e_end = jnp.exp(LT - LW)
    at = (-kk) * jnp.exp(LW - lw)
    rt = r * jnp.exp(LW)
    bt = b * e_out
    kt = k2 * e_out
    b_end = b * e_end
    k_end = k2 * e_end
    w_tot = jnp.exp(LT)

    lane_head = lax.broadcasted_iota(jnp.int32, (C, w), 1) >> HEAD_SHIFT
    t_idx = lax.broadcasted_iota(jnp.int32, (C, w), 0)
    j_idx = lax.broadcasted_iota(jnp.int32, (C, w), 1) & (HEAD_DIM - 1)
    strict = j_idx < t_idx
    incl = j_idx <= t_idx
    eye = jnp.where(j_idx == t_idx, 1.0, 0.0)
    sq_r = lax.broadcasted_iota(jnp.int32, (w, w), 0) >> HEAD_SHIFT
    sq_c = lax.broadcasted_iota(jnp.int32, (w, w), 1) >> HEAD_SHIFT
    block_diag = sq_r == sq_c
    stack = functools.partial(_stack_heads, lane_head=lane_head, n_heads=n_heads)

    chunks = range(tb // C)
    rows_of = [slice(c * C, (c + 1) * C) for c in chunks]
    bf = lambda t: t.astype(BF16)
    cat0 = lambda *ts: jnp.concatenate(ts, axis=0)
    cat1 = lambda *ts: jnp.concatenate(ts, axis=1)

    A = [_dot_nt(bf(cat0(at[sl], rt[sl])), cat0(stack(bf(bt[sl])), stack(bf(kt[sl]))))
         for sl in rows_of]
    a_ab = [jnp.where(strict, a[0:C, 0:w], 0.0) for a in A]
    a_ak = [jnp.where(strict, a[0:C, w:2 * w], 0.0) for a in A]
    a_rb = [jnp.where(incl, a[C:2 * C, 0:w], 0.0) for a in A]
    a_rk = [jnp.where(incl, a[C:2 * C, w:2 * w], 0.0) for a in A]

    pw = [_dot(bf(l), stack(bf(l))) for l in a_ab]
    tinv = [eye + l for l in a_ab]
    sz = 2
    while 2 * sz < C:
        both = [_dot(bf(cat0(t, p)), stack(bf(p))) for t, p in zip(tinv, pw)]
        tinv = [t + r[0:C] for t, r in zip(tinv, both)]
        pw = [r[C:2 * C] for r in both]
        sz *= 2
    tinv = [t + _dot(bf(t), stack(bf(p))) for t, p in zip(tinv, pw)]

    yo = [_dot(bf(cat0(ak, ark)), stack(bf(v[sl]))) for ak, ark, sl in zip(a_ak, a_rk, rows_of)]
    gu = [_dot(bf(t), cat1(stack(bf(at[sl])), stack(bf(y[0:C]))))
          for t, y, sl in zip(tinv, yo, rows_of)]
    pq = [_dot(bf(rb), cat1(stack(bf(x[:, 0:w])), stack(bf(x[:, w:2 * w]))))
          for rb, x in zip(a_rb, gu)]
    p_mat = [rt[sl] + x[:, 0:w] for sl, x in zip(rows_of, pq)]
    q_mat = [y[C:2 * C] + x[:, w:2 * w] for y, x in zip(yo, pq)]
    m_mat = [jnp.where(block_diag, _dot_tn(bf(x[:, 0:w]), bf(b_end[sl])), 0.0)
             for x, sl in zip(gu, rows_of)]
    n_mat = [jnp.where(block_diag,
                       _dot_tn(bf(cat0(x[:, w:2 * w], v[sl])), bf(cat0(b_end[sl], k_end[sl]))), 0.0)
             for x, sl in zip(gu, rows_of)]

    st = st_ref[...]
    outs = []
    for c in chunks:
        stb = bf(st)
        outs.append(_dot_nt(bf(p_mat[c]), stb) + q_mat[c])
        st = st * w_tot[c * C:c * C + 1, :] + _dot(stb, bf(m_mat[c])) + n_mat[c]
    st_ref[...] = st

    o = jnp.concatenate(outs, axis=0)
    inv_d = 1.0 / HEAD_DIM
    mean = _split_dot_right(o, ones, 1) * inv_d
    cen = o - mean
    var = _split_dot_right(cen * cen, ones, 1) * inv_d
    o = cen * lax.rsqrt(var + GN_EPS) * lng_ref[...] + lnb_ref[...]
    y_ref[...] = ((o + bonus) * g).astype(BF16)


def _rwkv(p, mu, w0, w2, a0, a2, g2, kkp, ka, rk, lng, lnb, *, batch, seq, tb):
    T, cols = p.shape
    w = w0.shape[1]
    nb = seq // tb
    vec = _resident((1, w))
    low = _resident((cols - 3 * w, w))
    return pl.pallas_call(
        _rwkv_kernel,
        out_shape=jax.ShapeDtypeStruct((T, w), BF16),
        grid=(batch, nb),
        in_specs=[
            pl.BlockSpec((tb, cols), lambda b, j: (b * nb + j, 0)),
            _resident((1, cols)), vec, low, vec, low, low, vec, vec, vec, vec, vec,
        ],
        out_specs=pl.BlockSpec((tb, w), lambda b, j: (b * nb + j, 0)),
        scratch_shapes=[pltpu.VMEM((SUBLANES, cols), F32), pltpu.VMEM((w, w), F32)],
        compiler_params=pltpu.CompilerParams(
            dimension_semantics=("parallel", "arbitrary"), vmem_limit_bytes=VMEM_LIMIT),
        name="rwkv",
    )(p, mu, w0, w2, a0, a2, g2, kkp, ka, rk, lng, lnb)


def _out_proj_kernel(ya_ref, yb_ref, yc_ref, x_ref, wa_ref, wb_ref, wc_ref, fg_ref, n2_ref,
                     wr_ref, br_ref, x1_ref, h2_ref, comb_ref, *, sub):
    for r0 in range(0, x_ref.shape[0], sub):
        rows = slice(r0, r0 + sub)
        ybn = (_rms(yb_ref[rows, :].astype(F32)) * fg_ref[...]).astype(BF16)
        x1 = (x_ref[rows, :] + _dot(ya_ref[rows, :], wa_ref[...]) + _dot(ybn, wb_ref[...])
              + _dot(yc_ref[rows, :], wc_ref[...]))
        x1_ref[rows, :] = x1
        h2 = _rms(x1) * n2_ref[...]
        h_hi = h2.astype(BF16)
        h2_ref[rows, :] = h_hi

        h_lo = (h2 - h_hi.astype(F32)).astype(BF16)
        hw = _dot(h_hi, wr_ref[...])
        logits = (hw[:, 0:LANES] + hw[:, LANES:2 * LANES] + _dot(h_lo, wr_ref[:, 0:LANES])
                  + br_ref[...])
        comb_ref[rows, :] = _route(logits)


def _route(logits):
    lane = lax.broadcasted_iota(jnp.int32, logits.shape, 1)
    lane_f = lane.astype(F32)
    far = float(LANES)

    def first_argmax(vals, vmax):
        return jnp.min(jnp.where(vals == vmax, lane_f, far), axis=-1, keepdims=True)

    gl = jnp.where(lane < N_GROUPS, logits, NEG_BIG)
    gmax = jnp.max(gl, axis=-1, keepdims=True)
    gidx = first_argmax(gl, gmax)
    g_p = 1.0 / jnp.sum(jnp.exp(gl - gmax), axis=-1, keepdims=True)

    e_group = ((lane - N_GROUPS) >> 2).astype(F32)
    el = jnp.where((lane >= N_GROUPS) & (e_group == gidx), logits, NEG_BIG)
    emax = jnp.max(el, axis=-1, keepdims=True)
    esum = jnp.sum(jnp.exp(el - emax), axis=-1, keepdims=True)
    i1 = first_argmax(el, emax)
    el2 = jnp.where(lane_f == i1, NEG_BIG, el)
    emax2 = jnp.max(el2, axis=-1, keepdims=True)
    i2 = first_argmax(el2, emax2)
    p1 = 1.0 / esum
    p2 = jnp.exp(emax2 - emax) / esum
    den = p1 + p2
    return (jnp.where(lane_f == i1, g_p * (p1 / den), 0.0)
            + jnp.where(lane_f == i2, g_p * (p2 / den), 0.0))


def _out_proj(ya, yb, yc, x2, wa, wb, wc, fg, n2, wr, br, *, tm):
    T, D = x2.shape
    row = lambda i: (i, 0)
    return pl.pallas_call(
        functools.partial(_out_proj_kernel, sub=tm // 2),
        out_shape=(jax.ShapeDtypeStruct((T, D), F32), jax.ShapeDtypeStruct((T, D), BF16),
                   jax.ShapeDtypeStruct((T, LANES), F32)),
        grid=(T // tm,),
        in_specs=[
            pl.BlockSpec((tm, ya.shape[1]), row),
            pl.BlockSpec((tm, yb.shape[1]), row),
            pl.BlockSpec((tm, yc.shape[1]), row),
            pl.BlockSpec((tm, D), row),
            _resident(wa.shape), _resident(wb.shape), _resident(wc.shape),
            _resident(fg.shape), _resident(n2.shape), _resident(wr.shape), _resident(br.shape),
        ],
        out_specs=(pl.BlockSpec((tm, D), row), pl.BlockSpec((tm, D), row),
                   pl.BlockSpec((tm, LANES), row)),
        compiler_params=pltpu.CompilerParams(
            dimension_semantics=("parallel",), vmem_limit_bytes=VMEM_LIMIT),
        name="out_proj",
    )(ya, yb, yc, x2, wa, wb, wc, fg, n2, wr, br)


def _moe_kernel(h_ref, comb_ref, x_ref, wgu_ref, wd_ref, o_ref, hid_ref):
    h = h_ref[...]
    comb = comb_ref[...]
    n_exp, _, two_f = wgu_ref.shape
    f = two_f // 2
    for e in range(n_exp):
        gu = _dot(h, wgu_ref[e])
        gate = gu[:, 0:f]
        hid = gate * jax.nn.sigmoid(gate) * gu[:, f:two_f]
        hid = hid * comb[:, N_GROUPS + e:N_GROUPS + e + 1]
        hid_ref[:, e * f:(e + 1) * f] = hid.astype(BF16)
    o_ref[...] = x_ref[...] + _dot(hid_ref[...], wd_ref[...])


def _moe(h2, comb, x1, wgu, wd, *, tm):
    T, D = x1.shape
    row = lambda i: (i, 0)
    return pl.pallas_call(
        _moe_kernel,
        out_shape=jax.ShapeDtypeStruct((T, D), F32),
        grid=(T // tm,),
        in_specs=[
            pl.BlockSpec((tm, D), row),
            pl.BlockSpec((tm, LANES), row),
            pl.BlockSpec((tm, D), row),
            _resident(wgu.shape),
            _resident(wd.shape),
        ],
        out_specs=pl.BlockSpec((tm, D), row),
        scratch_shapes=[pltpu.VMEM((tm, wd.shape[0]), BF16)],
        compiler_params=pltpu.CompilerParams(
            dimension_semantics=("parallel",), vmem_limit_bytes=VMEM_LIMIT),
        name="moe",
    )(h2, comb, x1, wgu, wd)


def _block_diag(wh):
    n_heads, d, _ = wh.shape
    eye = jnp.eye(n_heads, dtype=wh.dtype)
    return jnp.einsum("hij,hg->higj", wh, eye).reshape(n_heads * d, n_heads * d)


def _row(vec):
    return vec.reshape(1, -1).astype(F32)


def _pad_rows(mat, start, total):
    return jnp.zeros((total, mat.shape[1]), mat.dtype).at[start:start + mat.shape[0]].set(mat)


def kernel(x, norm1_g, w_in, conv_w, conv_b, lru_wa, lru_ba, lru_wx, lru_bx, lru_lambda,
           lru_norm_g, fox_fb, fox_qnorm_g, fox_knorm_g, fox_norm_g, rwkv_mu, rwkv_w0,
           rwkv_w2, rwkv_a0, rwkv_a2, rwkv_g2, rwkv_kk, rwkv_ka, rwkv_rk, rwkv_ln_g,
           rwkv_ln_b, w_out, norm2_g, router_gw, router_gb, router_ew, router_eb,
           exp_w_gate, exp_w_up, exp_w_down):
    batch, seq, d_model = x.shape
    depth = w_in.shape[0]
    lru_w = conv_w.shape[2]
    fox_heads = fox_fb.shape[1]
    fox_w = fox_heads * HEAD_DIM
    rwkv_w = rwkv_w0.shape[1]
    rwkv_cols = rwkv_mu.shape[1]
    d_rank, a_rank, g_rank = rwkv_w2.shape[1], rwkv_a2.shape[1], rwkv_g2.shape[1]
    low = d_rank + a_rank + g_rank
    n_exp, _, d_exp = exp_w_gate.shape[1:]
    assert low == LANES and rwkv_cols == 3 * rwkv_w + low
    assert fox_heads <= LANES and n_exp == N_EXPERTS

    o_fox = 2 * lru_w
    o_fl = o_fox + 3 * fox_w
    o_rwkv = o_fl + fox_heads
    seg = (0, o_fox, o_fox + fox_w, o_fox + 2 * fox_w, o_fl, o_fl + rwkv_cols,
           o_fl + rwkv_cols + LANES)

    x2 = x.reshape(batch * seq, d_model)
    for l in range(depth):
        wl = w_in[l]
        n_fl = FOX_BIAS_PIECES * fox_heads
        w_perm = jnp.concatenate(
            [wl[:, 0:o_fl], wl[:, o_rwkv:o_rwkv + rwkv_cols],
             jnp.repeat(wl[:, o_fl:o_rwkv], FOX_BIAS_PIECES, axis=1),
             jnp.zeros((d_model, LANES - n_fl), wl.dtype)], axis=1).astype(BF16)
        qg = _row(jnp.tile(fox_qnorm_g[l], fox_heads) * (HEAD_DIM ** -0.5 * LOG2E))
        kg = _row(jnp.tile(fox_knorm_g[l], fox_heads))
        fb = _row(jnp.pad(jnp.repeat(fox_fb[l], FOX_BIAS_PIECES), (0, LANES - n_fl)))
        u, q, k_aug, v, p = _in_proj(x2, _row(norm1_g[l]), w_perm, qg, kg, fb,
                                    seq=seq, tm=512, seg=seg)

        wg = jnp.concatenate([_block_diag(lru_wa[l]), _block_diag(lru_wx[l])], axis=1).astype(BF16)
        bg = _row(jnp.concatenate([lru_ba[l], lru_bx[l]]))
        ya = _lru(u, conv_w[l], _row(conv_b[l]), wg, bg, _row(lru_lambda[l]),
                  _row(lru_norm_g[l]), batch=batch, seq=seq, tb=512)

        yb = _fox(q, k_aug, v, batch=batch, seq=seq, tq=512)

        yc = _rwkv(p, _row(rwkv_mu[l]), _row(rwkv_w0[l]),
                   _pad_rows(rwkv_w2[l], 0, low).astype(BF16), _row(rwkv_a0[l]),
                   _pad_rows(rwkv_a2[l], d_rank, low).astype(BF16),
                   _pad_rows(rwkv_g2[l], d_rank + a_rank, low).astype(BF16),
                   _row(rwkv_kk[l]), _row(rwkv_ka[l]), _row(rwkv_rk[l]),
                   _row(rwkv_ln_g[l]), _row(rwkv_ln_b[l]), batch=batch, seq=seq, tb=512)

        wo = w_out[l].astype(BF16)
        wr = jnp.concatenate(
            [router_gw[l], router_ew[l],
             jnp.zeros((d_model, LANES - N_GROUPS - n_exp), F32)], axis=1)
        wr_hi = wr.astype(BF16)
        wr = jnp.concatenate([wr_hi, (wr - wr_hi.astype(F32)).astype(BF16)], axis=1)
        br = _row(jnp.pad(jnp.concatenate([router_gb[l], router_eb[l]]),
                          (0, LANES - N_GROUPS - n_exp)))
        x1, h2, comb = _out_proj(
            ya, yb, yc, x2, wo[0:lru_w], wo[lru_w:lru_w + fox_w], wo[lru_w + fox_w:],
            _row(fox_norm_g[l]), _row(norm2_g[l]), wr, br, tm=512)

        wgu = jnp.concatenate([exp_w_gate[l], exp_w_up[l]], axis=2).astype(BF16)
        wd = exp_w_down[l].reshape(n_exp * d_exp, d_model).astype(BF16)
        x2 = _moe(h2, comb, x1, wgu, wd, tm=512)
    return x2.reshape(batch, seq, d_model)
```

```python
import functools

import jax
import jax.numpy as jnp
from jax import lax
from jax.experimental import pallas as pl
from jax.experimental.pallas import tpu as pltpu

F32 = jnp.float32
BF16 = jnp.bfloat16

NORM_EPS = 1e-6
GN_EPS = 64e-5
LRU_C = 8.0
HEAD_DIM = 64
HEAD_SHIFT = 6
LANES = 128
SUBLANES = 8
CHUNK = 64
CONV_WIDTH = 4
N_GROUPS = 4
EXPERTS_PER_GROUP = 4
N_EXPERTS = N_GROUPS * EXPERTS_PER_GROUP
MOE_BLOCK = 128
NEG_BIG = -1e30
LOG2E = 1.4426950408889634
FOX_BIAS_PIECES = 3
FOX_DEN_ROWS = 16
VMEM_LIMIT = 56 * 1024 * 1024


def _dot(a, b):
    return jnp.dot(a, b, preferred_element_type=F32)


def _dot_nt(a, b):
    return lax.dot_general(a, b, (((1,), (1,)), ((), ())), preferred_element_type=F32)


def _dot_tn(a, b):
    return lax.dot_general(a, b, (((0,), (0,)), ((), ())), preferred_element_type=F32)


def _split_dot_right(x, ones, parts):
    acc = None
    rem = x
    for _ in range(parts):
        hi = rem.astype(BF16)
        t = _dot(hi, ones)
        acc = t if acc is None else acc + t
        rem = rem - hi.astype(F32)
    return acc


def _split_dot_left(ones, x, parts):
    acc = None
    rem = x
    for _ in range(parts):
        hi = rem.astype(BF16)
        t = _dot(ones, hi)
        acc = t if acc is None else acc + t
        rem = rem - hi.astype(F32)
    return acc


def _head_ones(n):
    r = lax.broadcasted_iota(jnp.int32, (n, n), 0) >> HEAD_SHIFT
    c = lax.broadcasted_iota(jnp.int32, (n, n), 1) >> HEAD_SHIFT
    return jnp.where(r == c, 1.0, 0.0).astype(BF16)


def _softplus(z):
    return jnp.maximum(z, 0.0) + jnp.log(1.0 + jnp.exp(-jnp.abs(z)))


def _rms(x, eps=NORM_EPS):
    return x * lax.rsqrt(jnp.mean(x * x, axis=-1, keepdims=True) + eps)


def _resident(shape):
    zeros = (0,) * len(shape)
    return pl.BlockSpec(shape, lambda *_: zeros, pipeline_mode=pl.Buffered(1))


def _in_proj_kernel(x_ref, g_ref, w_ref, qg_ref, kg_ref, fb_ref,
                    lru_ref, q_ref, k_ref, v_ref, p_ref, carry_ref,
                    *, tiles_per_seq, seg, sub):
    i = pl.program_id(0)

    @pl.when(i % tiles_per_seq == 0)
    def _():
        carry_ref[...] = jnp.zeros_like(carry_ref)

    fox = seg[2] - seg[1]
    n_heads = fox // HEAD_DIM
    ones = _head_ones(fox)
    inv_d = 1.0 / HEAD_DIM
    rr = lax.broadcasted_iota(jnp.int32, (sub, sub), 0)
    cc = lax.broadcasted_iota(jnp.int32, (sub, sub), 1)
    tri = jnp.where(cc <= rr, 1.0, 0.0).astype(BF16)
    lane = lax.broadcasted_iota(jnp.int32, (sub, LANES), 1)
    feat = lane < HEAD_DIM
    bias_lanes = (lane >= HEAD_DIM) & (lane < HEAD_DIM + FOX_BIAS_PIECES)
    q_bias = jnp.where(bias_lanes, 1.0, 0.0)
    piece_id = lax.rem(lane, FOX_BIAS_PIECES)
    carry = carry_ref[0:1, :]

    def head_norm(t):
        return t * lax.rsqrt(_dot((t * t).astype(BF16), ones) * inv_d + NORM_EPS)

    for r0 in range(0, x_ref.shape[0], sub):
        rows = slice(r0, r0 + sub)
        hb = (_rms(x_ref[rows, :]) * g_ref[...]).astype(BF16)
        lru_ref[rows, :] = _dot(hb, w_ref[:, seg[0]:seg[1]])
        q = head_norm(_dot(hb, w_ref[:, seg[1]:seg[2]])) * qg_ref[...]
        k = head_norm(_dot(hb, w_ref[:, seg[2]:seg[3]])) * kg_ref[...]
        v_ref[:, rows] = _dot(hb, w_ref[:, seg[3]:seg[4]]).T.astype(BF16)
        p_ref[rows, :] = _dot(hb, w_ref[:, seg[4]:seg[5]])

        lf = -_softplus(-(_dot(hb, w_ref[:, seg[5]:seg[6]]) + fb_ref[...]))
        c = _split_dot_left(tri, lf, 3) + carry
        carry = c[sub - 1:sub, :]

        rem = c * (-LOG2E)
        pieces = None
        for piece in range(FOX_BIAS_PIECES):
            hi = rem.astype(BF16).astype(F32)
            pieces = hi if pieces is None else jnp.where(piece_id == piece, hi, pieces)
            rem = rem - hi

        for h in range(n_heads):
            src = slice((h // 2) * LANES, (h // 2 + 1) * LANES)
            dst = slice(h * LANES, (h + 1) * LANES)
            qh, kh = q[:, src], k[:, src]
            if h % 2:
                qh = pltpu.roll(qh, HEAD_DIM, axis=1)
                kh = pltpu.roll(kh, HEAD_DIM, axis=1)
            k_bias = pltpu.roll(pieces, HEAD_DIM - FOX_BIAS_PIECES * h, axis=1)
            q_ref[rows, dst] = jnp.where(feat, qh, q_bias).astype(BF16)
            k_ref[rows, dst] = jnp.where(feat, kh, jnp.where(bias_lanes, k_bias, 0.0)).astype(BF16)

    carry_ref[...] = jnp.broadcast_to(carry, carry_ref.shape)


def _in_proj(x2, g, w, qg, kg, fb, *, seq, tm, seg):
    T, D = x2.shape
    n_out = w.shape[1]
    widths = [seg[j + 1] - seg[j] for j in range(6)]
    row = lambda i: (i, 0)
    col = lambda i: (0, i)
    outs = [
        ((T, widths[0]), (tm, widths[0]), row, F32),
        ((T, 2 * widths[1]), (tm, 2 * widths[1]), row, BF16),
        ((T, 2 * widths[2]), (tm, 2 * widths[2]), row, BF16),
        ((widths[3], T), (widths[3], tm), col, BF16),
        ((T, widths[4]), (tm, widths[4]), row, F32),
    ]
    return pl.pallas_call(
        functools.partial(_in_proj_kernel, tiles_per_seq=seq // tm, seg=seg, sub=tm // 2),
        out_shape=tuple(jax.ShapeDtypeStruct(shape, dt) for shape, _, _, dt in outs),
        grid=(T // tm,),
        in_specs=[
            pl.BlockSpec((tm, D), row),
            _resident((1, D)),
            _resident((D, n_out)),
            _resident((1, widths[1])),
            _resident((1, widths[2])),
            _resident((1, widths[5])),
        ],
        out_specs=tuple(pl.BlockSpec(blk, imap) for _, blk, imap, _ in outs),
        scratch_shapes=[pltpu.VMEM((SUBLANES, widths[5]), F32)],
        compiler_params=pltpu.CompilerParams(
            dimension_semantics=("arbitrary",), vmem_limit_bytes=VMEM_LIMIT),
        name="in_proj",
    )(x2, g, w, qg, kg, fb)


def _shift_rows(x, d, fill, row):
    return jnp.where(row >= d, pltpu.roll(x, d, axis=0), fill)


def _lru_kernel(u_ref, cw_ref, cb_ref, wg_ref, bg_ref, lam_ref, ng_ref,
                y_ref, buf_ref, h_ref):
    j = pl.program_id(1)
    tb = u_ref.shape[0]
    w = u_ref.shape[1] // 2
    pad = SUBLANES

    @pl.when(j == 0)
    def _():
        buf_ref[0:pad, :] = jnp.zeros((pad, w), F32)
        h_ref[...] = jnp.zeros_like(h_ref)

    xa = u_ref[:, 0:w]
    ga = u_ref[:, w:2 * w]
    buf_ref[pad:pad + tb, :] = xa
    xc = cb_ref[...] + cw_ref[CONV_WIDTH - 1:CONV_WIDTH, :] * xa
    for d in range(1, CONV_WIDTH):
        xc = xc + cw_ref[CONV_WIDTH - 1 - d:CONV_WIDTH - d, :] * buf_ref[pad - d:pad - d + tb, :]
    buf_ref[0:pad, :] = xa[tb - pad:tb, :]

    gates = _dot(xc.astype(BF16), wg_ref[...]) + bg_ref[...]
    r = jax.nn.sigmoid(gates[:, 0:w])
    i = jax.nn.sigmoid(gates[:, w:2 * w])
    log_a = (-LRU_C) * r * _softplus(-lam_ref[...])
    a = jnp.exp(log_a)
    b = jnp.sqrt(jnp.tanh(-log_a) * (1.0 + a * a)) * (i * xc)

    row = lax.broadcasted_iota(jnp.int32, (tb, w), 0)
    d = 1
    while d < tb:
        a_sh = _shift_rows(a, d, 1.0, row)
        b_sh = _shift_rows(b, d, 0.0, row)
        b = a * b_sh + b
        a = a * a_sh
        d *= 2
    h = b + a * h_ref[0:1, :]
    h_ref[...] = jnp.broadcast_to(h[tb - 1:tb, :], h_ref.shape)

    y = jax.nn.gelu(ga) * h
    y_ref[...] = (_rms(y) * ng_ref[...]).astype(BF16)


def _lru(u, cw, cb, wg, bg, lam, ng, *, batch, seq, tb):
    T, w2 = u.shape
    w = w2 // 2
    nb = seq // tb
    return pl.pallas_call(
        _lru_kernel,
        out_shape=jax.ShapeDtypeStruct((T, w), BF16),
        grid=(batch, nb),
        in_specs=[
            pl.BlockSpec((tb, w2), lambda b, j: (b * nb + j, 0)),
            _resident((CONV_WIDTH, w)),
            _resident((1, w)),
            _resident((w, w2)),
            _resident((1, w2)),
            _resident((1, w)),
            _resident((1, w)),
        ],
        out_specs=pl.BlockSpec((tb, w), lambda b, j: (b * nb + j, 0)),
        scratch_shapes=[pltpu.VMEM((tb + SUBLANES, w), F32), pltpu.VMEM((SUBLANES, w), F32)],
        compiler_params=pltpu.CompilerParams(
            dimension_semantics=("parallel", "arbitrary"), vmem_limit_bytes=VMEM_LIMIT),
        name="lru",
    )(u, cw, cb, wg, bg, lam, ng)


def _fox_kernel(q_ref, k_ref, v_ref, o_ref, s_buf, mx_buf, m_ref, acc_ref, *, tq):
    i = pl.program_id(2)
    n_heads = q_ref.shape[1] // LANES
    causal = (lax.broadcasted_iota(jnp.int32, (tq, tq), 0)
              <= lax.broadcasted_iota(jnp.int32, (tq, tq), 1))

    all_heads = tuple(range(n_heads))

    def scores(j, slot, masked, heads=all_heads):
        k0 = pl.multiple_of(j * tq, tq)
        for u in heads:
            hl = slice(u * LANES, (u + 1) * LANES)
            s = _dot_nt(k_ref[pl.ds(k0, tq), hl], q_ref[:, hl])
            if masked:
                s = jnp.where(causal, s, NEG_BIG)
            s_buf[slot, u] = s
            mx_buf[slot, u] = jnp.max(s, axis=0, keepdims=True)

    def consume(j, slot, heads=all_heads):
        k0 = pl.multiple_of(j * tq, tq)
        for u in heads:
            m = m_ref[u]
            m_new = jnp.maximum(m, mx_buf[slot, u])
            alpha = jnp.exp2(m - m_new)
            p = jnp.exp2(s_buf[slot, u] - m_new)
            m_ref[u] = m_new
            vt = jnp.concatenate([v_ref[u * HEAD_DIM:(u + 1) * HEAD_DIM, pl.ds(k0, tq)],
                                  jnp.ones((FOX_DEN_ROWS, tq), BF16)], axis=0)
            acc_ref[u] = alpha * acc_ref[u] + _dot(vt, p.astype(BF16))

    m_ref[...] = jnp.full(m_ref.shape, NEG_BIG, F32)
    acc_ref[...] = jnp.zeros_like(acc_ref)

    def stage(j, slot, masked_next):
        for u in all_heads:
            scores(j + 1, 1 - slot, masked_next, (u,))
            consume(j, slot, (u,))

    @pl.when(i == 0)
    def _():
        scores(0, 0, True)
        consume(0, 0)

    @pl.when(i > 0)
    def _():
        scores(0, 0, False)

        def body(t, carry):
            stage(2 * t, 0, False)
            stage(2 * t + 1, 1, False)
            return carry

        lax.fori_loop(0, (i - 1) // 2, body, 0)

        @pl.when(i % 2 == 1)
        def _():
            stage(i - 1, 0, True)
            consume(i, 1)

        @pl.when(i % 2 == 0)
        def _():
            stage(i - 2, 0, False)
            stage(i - 1, 1, True)
            consume(i, 0)

    o_t = jnp.concatenate(
        [acc_ref[u, 0:HEAD_DIM, :] / acc_ref[u, HEAD_DIM:HEAD_DIM + 1, :] for u in range(n_heads)],
        axis=0)
    o_ref[...] = o_t.T.astype(BF16)


def _fox(q_aug, k_aug, v_t, *, batch, seq, tq):
    T, wide = q_aug.shape
    heads_per_step = 2
    blk = heads_per_step * LANES
    nq = seq // tq
    steps = wide // blk
    return pl.pallas_call(
        functools.partial(_fox_kernel, tq=tq),
        out_shape=jax.ShapeDtypeStruct((T, steps * heads_per_step * HEAD_DIM), BF16),
        grid=(batch, steps, nq),
        in_specs=[
            pl.BlockSpec((tq, blk), lambda b, h, i: (b * nq + i, h)),
            pl.BlockSpec((seq, blk), lambda b, h, i: (b, h)),
            pl.BlockSpec((heads_per_step * HEAD_DIM, seq), lambda b, h, i: (h, b)),
        ],
        out_specs=pl.BlockSpec((tq, heads_per_step * HEAD_DIM), lambda b, h, i: (b * nq + i, h)),
        scratch_shapes=[
            pltpu.VMEM((2, heads_per_step, tq, tq), F32),
            pltpu.VMEM((2, heads_per_step, 1, tq), F32),
            pltpu.VMEM((heads_per_step, 1, tq), F32),
            pltpu.VMEM((heads_per_step, HEAD_DIM + FOX_DEN_ROWS, tq), F32),
        ],
        compiler_params=pltpu.CompilerParams(
            dimension_semantics=("parallel", "parallel", "arbitrary"),
            vmem_limit_bytes=VMEM_LIMIT),
        name="fox",
    )(q_aug, k_aug, v_t)


def _stack_heads(x, lane_head, n_heads):
    zero = jnp.zeros_like(x)
    return jnp.concatenate([jnp.where(lane_head == h, x, zero) for h in range(n_heads)], axis=0)


def _rwkv_kernel(p_ref, mu_ref, w0_ref, w2_ref, a0_ref, a2_ref, g2_ref, kk_ref, ka_ref,
                 rk_ref, lng_ref, lnb_ref, y_ref, prev_ref, st_ref):
    j = pl.program_id(1)
    tb, cols = p_ref.shape
    w = y_ref.shape[1]
    n_heads = w // HEAD_DIM
    C = CHUNK
    chunk_shift = CHUNK.bit_length() - 1

    @pl.when(j == 0)
    def _():
        prev_ref[...] = jnp.zeros_like(prev_ref)
        st_ref[...] = jnp.zeros_like(st_ref)

    p = p_ref[...]
    row = lax.broadcasted_iota(jnp.int32, (tb, cols), 0)
    prev = jnp.where(row == 0, prev_ref[0:1, :], pltpu.roll(p, 1, axis=0))
    prev_ref[...] = jnp.broadcast_to(p[tb - 1:tb, :], prev_ref.shape)
    ps = p + (prev - p) * mu_ref[...]
    r = ps[:, 0:w]
    k = ps[:, w:2 * w]
    v = ps[:, 2 * w:3 * w]
    lr = ps[:, 3 * w:cols]

    wpre = w0_ref[...] + _dot(jnp.tanh(lr).astype(BF16), w2_ref[...])
    lw = -jnp.exp(-_softplus(-wpre) - 0.5)
    a = jax.nn.sigmoid(a0_ref[...] + _dot(lr.astype(BF16), a2_ref[...]))
    g = _dot(jax.nn.sigmoid(lr).astype(BF16), g2_ref[...])

    ones = _head_ones(w)
    kk = k * kk_ref[...]
    kk = kk / jnp.maximum(jnp.sqrt(_split_dot_right(kk * kk, ones, 1)), 1e-12)
    k2 = k * (1.0 + (a - 1.0) * ka_ref[...])
    b = kk * a
    bonus = _split_dot_right(r * k2 * rk_ref[...], ones, 1) * v

    rr = lax.broadcasted_iota(jnp.int32, (tb, tb), 0)
    cc = lax.broadcasted_iota(jnp.int32, (tb, tb), 1)
    same = (rr >> chunk_shift) == (cc >> chunk_shift)
    tri = jnp.where(same & (cc <= rr), 1.0, 0.0).astype(BF16)
    tot = jnp.where(same, 1.0, 0.0).astype(BF16)
    LW = _split_dot_left(tri, lw, 3)
    LT = _split_dot_left(tot, lw, 3)
    e_out = jnp.exp(-LW)
    e_end = jnp.exp(LT - LW)
    at = (-kk) * jnp.exp(LW - lw)
    rt = r * jnp.exp(LW)
    bt = b * e_out
    kt = k2 * e_out
    b_end = b * e_end
    k_end = k2 * e_end
    w_tot = jnp.exp(LT)

    lane_head = lax.broadcasted_iota(jnp.int32, (C, w), 1) >> HEAD_SHIFT
    t_idx = lax.broadcasted_iota(jnp.int32, (C, w), 0)
    j_idx = lax.broadcasted_iota(jnp.int32, (C, w), 1) & (HEAD_DIM - 1)
    strict = j_idx < t_idx
    incl = j_idx <= t_idx
    eye = jnp.where(j_idx == t_idx, 1.0, 0.0)
    sq_r = lax.broadcasted_iota(jnp.int32, (w, w), 0) >> HEAD_SHIFT
    sq_c = lax.broadcasted_iota(jnp.int32, (w, w), 1) >> HEAD_SHIFT
    block_diag = sq_r == sq_c
    stack = functools.partial(_stack_heads, lane_head=lane_head, n_heads=n_heads)

    chunks = range(tb // C)
    rows_of = [slice(c * C, (c + 1) * C) for c in chunks]
    bf = lambda t: t.astype(BF16)
    cat0 = lambda *ts: jnp.concatenate(ts, axis=0)
    cat1 = lambda *ts: jnp.concatenate(ts, axis=1)

    A = [_dot_nt(bf(cat0(at[sl], rt[sl])), cat0(stack(bf(bt[sl])), stack(bf(kt[sl]))))
         for sl in rows_of]
    a_ab = [jnp.where(strict, a[0:C, 0:w], 0.0) for a in A]
    a_ak = [jnp.where(strict, a[0:C, w:2 * w], 0.0) for a in A]
    a_rb = [jnp.where(incl, a[C:2 * C, 0:w], 0.0) for a in A]
    a_rk = [jnp.where(incl, a[C:2 * C, w:2 * w], 0.0) for a in A]

    pw = [_dot(bf(l), stack(bf(l))) for l in a_ab]
    tinv = [eye + l for l in a_ab]
    sz = 2
    while 2 * sz < C:
        both = [_dot(bf(cat0(t, p)), stack(bf(p))) for t, p in zip(tinv, pw)]
        tinv = [t + r[0:C] for t, r in zip(tinv, both)]
        pw = [r[C:2 * C] for r in both]
        sz *= 2
    tinv = [t + _dot(bf(t), stack(bf(p))) for t, p in zip(tinv, pw)]

    yo = [_dot(bf(cat0(ak, ark)), stack(bf(v[sl]))) for ak, ark, sl in zip(a_ak, a_rk, rows_of)]
    gu = [_dot(bf(t), cat1(stack(bf(at[sl])), stack(bf(y[0:C]))))
          for t, y, sl in zip(tinv, yo, rows_of)]
    pq = [_dot(bf(rb), cat1(stack(bf(x[:, 0:w])), stack(bf(x[:, w:2 * w]))))
          for rb, x in zip(a_rb, gu)]
    p_mat = [rt[sl] + x[:, 0:w] for sl, x in zip(rows_of, pq)]
    q_mat = [y[C:2 * C] + x[:, w:2 * w] for y, x in zip(yo, pq)]
    m_mat = [jnp.where(block_diag, _dot_tn(bf(x[:, 0:w]), bf(b_end[sl])), 0.0)
             for x, sl in zip(gu, rows_of)]
    n_mat = [jnp.where(block_diag,
                       _dot_tn(bf(cat0(x[:, w:2 * w], v[sl])), bf(cat0(b_end[sl], k_end[sl]))), 0.0)
             for x, sl in zip(gu, rows_of)]

    st = st_ref[...]
    outs = []
    for c in chunks:
        stb = bf(st)
        outs.append(_dot_nt(bf(p_mat[c]), stb) + q_mat[c])
        st = st * w_tot[c * C:c * C + 1, :] + _dot(stb, bf(m_mat[c])) + n_mat[c]
    st_ref[...] = st

    o = jnp.concatenate(outs, axis=0)
    inv_d = 1.0 / HEAD_DIM
    mean = _split_dot_right(o, ones, 1) * inv_d
    cen = o - mean
    var = _split_dot_right(cen * cen, ones, 1) * inv_d
    o = cen * lax.rsqrt(var + GN_EPS) * lng_ref[...] + lnb_ref[...]
    y_ref[...] = ((o + bonus) * g).astype(BF16)


def _rwkv(p, mu, w0, w2, a0, a2, g2, kkp, ka, rk, lng, lnb, *, batch, seq, tb):
    T, cols = p.shape
    w = w0.shape[1]
    nb = seq // tb
    vec = _resident((1, w))
    low = _resident((cols - 3 * w, w))
    return pl.pallas_call(
        _rwkv_kernel,
        out_shape=jax.ShapeDtypeStruct((T, w), BF16),
        grid=(batch, nb),
        in_specs=[
            pl.BlockSpec((tb, cols), lambda b, j: (b * nb + j, 0)),
            _resident((1, cols)), vec, low, vec, low, low, vec, vec, vec, vec, vec,
        ],
        out_specs=pl.BlockSpec((tb, w), lambda b, j: (b * nb + j, 0)),
        scratch_shapes=[pltpu.VMEM((SUBLANES, cols), F32), pltpu.VMEM((w, w), F32)],
        compiler_params=pltpu.CompilerParams(
            dimension_semantics=("parallel", "arbitrary"), vmem_limit_bytes=VMEM_LIMIT),
        name="rwkv",
    )(p, mu, w0, w2, a0, a2, g2, kkp, ka, rk, lng, lnb)


def _out_proj_kernel(ya_ref, yb_ref, yc_ref, x_ref, wa_ref, wb_ref, wc_ref, fg_ref, n2_ref,
                     wr_ref, br_ref, x1_ref, h2_ref, comb_ref, *, sub):
    for r0 in range(0, x_ref.shape[0], sub):
        rows = slice(r0, r0 + sub)
        ybn = (_rms(yb_ref[rows, :].astype(F32)) * fg_ref[...]).astype(BF16)
        x1 = (x_ref[rows, :] + _dot(ya_ref[rows, :], wa_ref[...]) + _dot(ybn, wb_ref[...])
              + _dot(yc_ref[rows, :], wc_ref[...]))
        x1_ref[rows, :] = x1
        h2 = _rms(x1) * n2_ref[...]
        h_hi = h2.astype(BF16)
        h2_ref[rows, :] = h_hi

        h_lo = (h2 - h_hi.astype(F32)).astype(BF16)
        hw = _dot(h_hi, wr_ref[...])
        logits = (hw[:, 0:LANES] + hw[:, LANES:2 * LANES] + _dot(h_lo, wr_ref[:, 0:LANES])
                  + br_ref[...])
        comb_ref[rows, :] = _route(logits)


def _route(logits):
    lane = lax.broadcasted_iota(jnp.int32, logits.shape, 1)
    lane_f = lane.astype(F32)
    far = float(LANES)

    def first_argmax(vals, vmax):
        return jnp.min(jnp.where(vals == vmax, lane_f, far), axis=-1, keepdims=True)

    gl = jnp.where(lane < N_GROUPS, logits, NEG_BIG)
    gmax = jnp.max(gl, axis=-1, keepdims=True)
    gidx = first_argmax(gl, gmax)
    g_p = 1.0 / jnp.sum(jnp.exp(gl - gmax), axis=-1, keepdims=True)

    e_group = ((lane - N_GROUPS) >> 2).astype(F32)
    el = jnp.where((lane >= N_GROUPS) & (e_group == gidx), logits, NEG_BIG)
    emax = jnp.max(el, axis=-1, keepdims=True)
    esum = jnp.sum(jnp.exp(el - emax), axis=-1, keepdims=True)
    i1 = first_argmax(el, emax)
    el2 = jnp.where(lane_f == i1, NEG_BIG, el)
    emax2 = jnp.max(el2, axis=-1, keepdims=True)
    i2 = first_argmax(el2, emax2)
    p1 = 1.0 / esum
    p2 = jnp.exp(emax2 - emax) / esum
    den = p1 + p2
    return (jnp.where(lane_f == i1, g_p * (p1 / den), 0.0)
            + jnp.where(lane_f == i2, g_p * (p2 / den), 0.0)
            + jnp.where(lane_f == gidx, 1.0, 0.0))


def _out_proj(ya, yb, yc, x2, wa, wb, wc, fg, n2, wr, br, *, tm):
    T, D = x2.shape
    row = lambda i: (i, 0)
    return pl.pallas_call(
        functools.partial(_out_proj_kernel, sub=tm // 2),
        out_shape=(jax.ShapeDtypeStruct((T, D), F32), jax.ShapeDtypeStruct((T, D), BF16),
                   jax.ShapeDtypeStruct((T, LANES), F32)),
        grid=(T // tm,),
        in_specs=[
            pl.BlockSpec((tm, ya.shape[1]), row),
            pl.BlockSpec((tm, yb.shape[1]), row),
            pl.BlockSpec((tm, yc.shape[1]), row),
            pl.BlockSpec((tm, D), row),
            _resident(wa.shape), _resident(wb.shape), _resident(wc.shape),
            _resident(fg.shape), _resident(n2.shape), _resident(wr.shape), _resident(br.shape),
        ],
        out_specs=(pl.BlockSpec((tm, D), row), pl.BlockSpec((tm, D), row),
                   pl.BlockSpec((tm, LANES), row)),
        compiler_params=pltpu.CompilerParams(
            dimension_semantics=("parallel",), vmem_limit_bytes=VMEM_LIMIT),
        name="out_proj",
    )(ya, yb, yc, x2, wa, wb, wc, fg, n2, wr, br)


def _moe_kernel(h_ref, route_ref, x_ref, wg_ref, wu_ref, wd_ref, o_ref, hs_ref, rs_ref, ys_ref):
    tm = h_ref.shape[0]
    n_groups, _, gf = wg_ref.shape
    f = gf // EXPERTS_PER_GROUP
    route = route_ref[...]
    lane = lax.broadcasted_iota(jnp.int32, (tm, LANES), 1)
    onehot = jnp.where(lane < n_groups, route, 0.0)
    rr = lax.broadcasted_iota(jnp.int32, (tm, tm), 0)
    cc = lax.broadcasted_iota(jnp.int32, (tm, tm), 1)
    tri = jnp.where(cc <= rr, 1.0, 0.0).astype(BF16)
    cum = _dot(tri, onehot.astype(BF16))
    tot = cum[tm - 1:tm, :]
    start = jnp.zeros_like(tot)
    for d in range(1, n_groups):
        start = start + jnp.where(lane[0:1] >= d, pltpu.roll(tot, d, axis=1), 0.0)
    end = start + tot
    pos = jnp.sum(jnp.where(onehot > 0.0, start + cum - 1.0, 0.0), axis=-1, keepdims=True)
    perm_t = jnp.where(pos == cc.astype(F32), 1.0, 0.0).astype(BF16)

    hs_ref[...] = _dot_tn(perm_t, h_ref[...]).astype(BF16)
    rem = route
    sorted_route = None
    for _ in range(2):
        hi = rem.astype(BF16)
        t = _dot_tn(perm_t, hi)
        sorted_route = t if sorted_route is None else sorted_route + t
        rem = rem - hi.astype(F32)
    rs_ref[...] = sorted_route
    ys_ref[...] = jnp.zeros_like(ys_ref)

    for g in range(n_groups):
        g_start = start[0, g]
        g_end = end[0, g]
        for r0 in range(0, tm, MOE_BLOCK):

            @pl.when((g_start < r0 + MOE_BLOCK) & (g_end > r0))
            def _(g=g, r0=r0):
                rows = slice(r0, r0 + MOE_BLOCK)
                hb = hs_ref[rows, :]
                rw = rs_ref[rows, :]
                gate = _dot(hb, wg_ref[g])
                hid = gate * jax.nn.sigmoid(gate) * _dot(hb, wu_ref[g])
                e0 = N_GROUPS + g * EXPERTS_PER_GROUP
                weights = jnp.concatenate(
                    [jnp.broadcast_to(rw[:, e0 + e:e0 + e + 1], (MOE_BLOCK, f))
                     for e in range(EXPERTS_PER_GROUP)], axis=1)
                ys_ref[rows, :] += _dot((hid * weights).astype(BF16), wd_ref[g])

    o_ref[...] = x_ref[...] + _dot(perm_t, ys_ref[...].astype(BF16))


def _moe(h2, route, x1, wg, wu, wd, *, tm):
    T, D = x1.shape
    row = lambda i: (i, 0)
    return pl.pallas_call(
        _moe_kernel,
        out_shape=jax.ShapeDtypeStruct((T, D), F32),
        grid=(T // tm,),
        in_specs=[
            pl.BlockSpec((tm, D), row),
            pl.BlockSpec((tm, LANES), row),
            pl.BlockSpec((tm, D), row),
            _resident(wg.shape),
            _resident(wu.shape),
            _resident(wd.shape),
        ],
        out_specs=pl.BlockSpec((tm, D), row),
        scratch_shapes=[pltpu.VMEM((tm, D), BF16), pltpu.VMEM((tm, LANES), F32),
                        pltpu.VMEM((tm, D), F32)],
        compiler_params=pltpu.CompilerParams(
            dimension_semantics=("parallel",), vmem_limit_bytes=VMEM_LIMIT),
        name="moe",
    )(h2, route, x1, wg, wu, wd)


def _block_diag(wh):
    n_heads, d, _ = wh.shape
    eye = jnp.eye(n_heads, dtype=wh.dtype)
    return jnp.einsum("hij,hg->higj", wh, eye).reshape(n_heads * d, n_heads * d)


def _row(vec):
    return vec.reshape(1, -1).astype(F32)


def _pad_rows(mat, start, total):
    return jnp.zeros((total, mat.shape[1]), mat.dtype).at[start:start + mat.shape[0]].set(mat)


def kernel(x, norm1_g, w_in, conv_w, conv_b, lru_wa, lru_ba, lru_wx, lru_bx, lru_lambda,
           lru_norm_g, fox_fb, fox_qnorm_g, fox_knorm_g, fox_norm_g, rwkv_mu, rwkv_w0,
           rwkv_w2, rwkv_a0, rwkv_a2, rwkv_g2, rwkv_kk, rwkv_ka, rwkv_rk, rwkv_ln_g,
           rwkv_ln_b, w_out, norm2_g, router_gw, router_gb, router_ew, router_eb,
           exp_w_gate, exp_w_up, exp_w_down):
    batch, seq, d_model = x.shape
    depth = w_in.shape[0]
    lru_w = conv_w.shape[2]
    fox_heads = fox_fb.shape[1]
    fox_w = fox_heads * HEAD_DIM
    rwkv_w = rwkv_w0.shape[1]
    rwkv_cols = rwkv_mu.shape[1]
    d_rank, a_rank, g_rank = rwkv_w2.shape[1], rwkv_a2.shape[1], rwkv_g2.shape[1]
    low = d_rank + a_rank + g_rank
    n_exp, _, d_exp = exp_w_gate.shape[1:]
    assert low == LANES and rwkv_cols == 3 * rwkv_w + low
    assert fox_heads <= LANES and n_exp == N_EXPERTS

    o_fox = 2 * lru_w
    o_fl = o_fox + 3 * fox_w
    o_rwkv = o_fl + fox_heads
    seg = (0, o_fox, o_fox + fox_w, o_fox + 2 * fox_w, o_fl, o_fl + rwkv_cols,
           o_fl + rwkv_cols + LANES)

    x2 = x.reshape(batch * seq, d_model)
    for l in range(depth):
        wl = w_in[l]
        n_fl = FOX_BIAS_PIECES * fox_heads
        w_perm = jnp.concatenate(
            [wl[:, 0:o_fl], wl[:, o_rwkv:o_rwkv + rwkv_cols],
             jnp.repeat(wl[:, o_fl:o_rwkv], FOX_BIAS_PIECES, axis=1),
             jnp.zeros((d_model, LANES - n_fl), wl.dtype)], axis=1).astype(BF16)
        qg = _row(jnp.tile(fox_qnorm_g[l], fox_heads) * (HEAD_DIM ** -0.5 * LOG2E))
        kg = _row(jnp.tile(fox_knorm_g[l], fox_heads))
        fb = _row(jnp.pad(jnp.repeat(fox_fb[l], FOX_BIAS_PIECES), (0, LANES - n_fl)))
        u, q, k_aug, v, p = _in_proj(x2, _row(norm1_g[l]), w_perm, qg, kg, fb,
                                    seq=seq, tm=512, seg=seg)

        wg = jnp.concatenate([_block_diag(lru_wa[l]), _block_diag(lru_wx[l])], axis=1).astype(BF16)
        bg = _row(jnp.concatenate([lru_ba[l], lru_bx[l]]))
        ya = _lru(u, conv_w[l], _row(conv_b[l]), wg, bg, _row(lru_lambda[l]),
                  _row(lru_norm_g[l]), batch=batch, seq=seq, tb=512)

        yb = _fox(q, k_aug, v, batch=batch, seq=seq, tq=512)

        yc = _rwkv(p, _row(rwkv_mu[l]), _row(rwkv_w0[l]),
                   _pad_rows(rwkv_w2[l], 0, low).astype(BF16), _row(rwkv_a0[l]),
                   _pad_rows(rwkv_a2[l], d_rank, low).astype(BF16),
                   _pad_rows(rwkv_g2[l], d_rank + a_rank, low).astype(BF16),
                   _row(rwkv_kk[l]), _row(rwkv_ka[l]), _row(rwkv_rk[l]),
                   _row(rwkv_ln_g[l]), _row(rwkv_ln_b[l]), batch=batch, seq=seq, tb=512)

        wo = w_out[l].astype(BF16)
        wr = jnp.concatenate(
            [router_gw[l], router_ew[l],
             jnp.zeros((d_model, LANES - N_GROUPS - n_exp), F32)], axis=1)
        wr_hi = wr.astype(BF16)
        wr = jnp.concatenate([wr_hi, (wr - wr_hi.astype(F32)).astype(BF16)], axis=1)
        br = _row(jnp.pad(jnp.concatenate([router_gb[l], router_eb[l]]),
                          (0, LANES - N_GROUPS - n_exp)))
        x1, h2, comb = _out_proj(
            ya, yb, yc, x2, wo[0:lru_w], wo[lru_w:lru_w + fox_w], wo[lru_w + fox_w:],
            _row(fox_norm_g[l]), _row(norm2_g[l]), wr, br, tm=512)

        def by_group(w):
            w = w.astype(BF16).reshape(N_GROUPS, EXPERTS_PER_GROUP, d_model, d_exp)
            return w.transpose(0, 2, 1, 3).reshape(N_GROUPS, d_model, EXPERTS_PER_GROUP * d_exp)

        wd = exp_w_down[l].astype(BF16).reshape(N_GROUPS, EXPERTS_PER_GROUP * d_exp, d_model)
        x2 = _moe(h2, comb, x1, by_group(exp_w_gate[l]), by_group(exp_w_up[l]), wd, tm=512)
    return x2.reshape(batch, seq, d_model)
```

```python
import functools

import jax
import jax.numpy as jnp
from jax import lax
from jax.experimental import pallas as pl
from jax.experimental.pallas import tpu as pltpu

F32 = jnp.float32
BF16 = jnp.bfloat16

NORM_EPS = 1e-6
GN_EPS = 64e-5
LRU_C = 8.0
HEAD_DIM = 64
HEAD_SHIFT = 6
LANES = 128
SUBLANES = 8
CHUNK = 64
CONV_WIDTH = 4
N_GROUPS = 4
EXPERTS_PER_GROUP = 4
N_EXPERTS = N_GROUPS * EXPERTS_PER_GROUP
MOE_BLOCK = 128
NEG_BIG = -1e30
LOG2E = 1.4426950408889634
FOX_BIAS_PIECES = 3
FOX_DEN_ROWS = 16
VMEM_LIMIT = 56 * 1024 * 1024


def _dot(a, b):
    return jnp.dot(a, b, preferred_element_type=F32)


def _dot_nt(a, b):
    return lax.dot_general(a, b, (((1,), (1,)), ((), ())), preferred_element_type=F32)


def _dot_tn(a, b):
    return lax.dot_general(a, b, (((0,), (0,)), ((), ())), preferred_element_type=F32)


def _split_dot_right(x, ones, parts):
    acc = None
    rem = x
    for _ in range(parts):
        hi = rem.astype(BF16)
        t = _dot(hi, ones)
        acc = t if acc is None else acc + t
        rem = rem - hi.astype(F32)
    return acc


def _split_dot_left(ones, x, parts):
    acc = None
    rem = x
    for _ in range(parts):
        hi = rem.astype(BF16)
        t = _dot(ones, hi)
        acc = t if acc is None else acc + t
        rem = rem - hi.astype(F32)
    return acc


def _head_ones(n):
    r = lax.broadcasted_iota(jnp.int32, (n, n), 0) >> HEAD_SHIFT
    c = lax.broadcasted_iota(jnp.int32, (n, n), 1) >> HEAD_SHIFT
    return jnp.where(r == c, 1.0, 0.0).astype(BF16)


def _softplus(z):
    return jnp.maximum(z, 0.0) + jnp.log(1.0 + jnp.exp(-jnp.abs(z)))


def _rms(x, eps=NORM_EPS):
    return x * lax.rsqrt(jnp.mean(x * x, axis=-1, keepdims=True) + eps)


def _resident(shape):
    zeros = (0,) * len(shape)
    return pl.BlockSpec(shape, lambda *_: zeros, pipeline_mode=pl.Buffered(1))


def _in_proj_kernel(x_ref, g_ref, w_ref, qg_ref, kg_ref, fb_ref,
                    lru_ref, q_ref, k_ref, v_ref, p_ref, carry_ref,
                    *, tiles_per_seq, seg, sub):
    i = pl.program_id(0)

    @pl.when(i % tiles_per_seq == 0)
    def _():
        carry_ref[...] = jnp.zeros_like(carry_ref)

    fox = seg[2] - seg[1]
    n_heads = fox // HEAD_DIM
    ones = _head_ones(fox)
    inv_d = 1.0 / HEAD_DIM
    rr = lax.broadcasted_iota(jnp.int32, (sub, sub), 0)
    cc = lax.broadcasted_iota(jnp.int32, (sub, sub), 1)
    tri = jnp.where(cc <= rr, 1.0, 0.0).astype(BF16)
    lane = lax.broadcasted_iota(jnp.int32, (sub, LANES), 1)
    feat = lane < HEAD_DIM
    bias_lanes = (lane >= HEAD_DIM) & (lane < HEAD_DIM + FOX_BIAS_PIECES)
    q_bias = jnp.where(bias_lanes, 1.0, 0.0)
    piece_id = lax.rem(lane, FOX_BIAS_PIECES)
    carry = carry_ref[0:1, :]

    def head_norm(t):
        return t * lax.rsqrt(_dot((t * t).astype(BF16), ones) * inv_d + NORM_EPS)

    for r0 in range(0, x_ref.shape[0], sub):
        rows = slice(r0, r0 + sub)
        hb = (_rms(x_ref[rows, :]) * g_ref[...]).astype(BF16)
        lru_ref[rows, :] = _dot(hb, w_ref[:, seg[0]:seg[1]])
        q = head_norm(_dot(hb, w_ref[:, seg[1]:seg[2]])) * qg_ref[...]
        k = head_norm(_dot(hb, w_ref[:, seg[2]:seg[3]])) * kg_ref[...]
        v_ref[:, rows] = _dot(hb, w_ref[:, seg[3]:seg[4]]).T.astype(BF16)
        p_ref[rows, :] = _dot(hb, w_ref[:, seg[4]:seg[5]])

        lf = -_softplus(-(_dot(hb, w_ref[:, seg[5]:seg[6]]) + fb_ref[...]))
        c = _split_dot_left(tri, lf, 3) + carry
        carry = c[sub - 1:sub, :]

        rem = c * (-LOG2E)
        pieces = None
        for piece in range(FOX_BIAS_PIECES):
            hi = rem.astype(BF16).astype(F32)
            pieces = hi if pieces is None else jnp.where(piece_id == piece, hi, pieces)
            rem = rem - hi

        for h in range(n_heads):
            src = slice((h // 2) * LANES, (h // 2 + 1) * LANES)
            dst = slice(h * LANES, (h + 1) * LANES)
            qh, kh = q[:, src], k[:, src]
            if h % 2:
                qh = pltpu.roll(qh, HEAD_DIM, axis=1)
                kh = pltpu.roll(kh, HEAD_DIM, axis=1)
            k_bias = pltpu.roll(pieces, HEAD_DIM - FOX_BIAS_PIECES * h, axis=1)
            q_ref[rows, dst] = jnp.where(feat, qh, q_bias).astype(BF16)
            k_ref[rows, dst] = jnp.where(feat, kh, jnp.where(bias_lanes, k_bias, 0.0)).astype(BF16)

    carry_ref[...] = jnp.broadcast_to(carry, carry_ref.shape)


def _in_proj(x2, g, w, qg, kg, fb, *, seq, tm, seg):
    T, D = x2.shape
    n_out = w.shape[1]
    widths = [seg[j + 1] - seg[j] for j in range(6)]
    row = lambda i: (i, 0)
    col = lambda i: (0, i)
    outs = [
        ((T, widths[0]), (tm, widths[0]), row, F32),
        ((T, 2 * widths[1]), (tm, 2 * widths[1]), row, BF16),
        ((T, 2 * widths[2]), (tm, 2 * widths[2]), row, BF16),
        ((widths[3], T), (widths[3], tm), col, BF16),
        ((T, widths[4]), (tm, widths[4]), row, F32),
    ]
    return pl.pallas_call(
        functools.partial(_in_proj_kernel, tiles_per_seq=seq // tm, seg=seg, sub=tm // 2),
        out_shape=tuple(jax.ShapeDtypeStruct(shape, dt) for shape, _, _, dt in outs),
        grid=(T // tm,),
        in_specs=[
            pl.BlockSpec((tm, D), row),
            _resident((1, D)),
            _resident((D, n_out)),
            _resident((1, widths[1])),
            _resident((1, widths[2])),
            _resident((1, widths[5])),
        ],
        out_specs=tuple(pl.BlockSpec(blk, imap) for _, blk, imap, _ in outs),
        scratch_shapes=[pltpu.VMEM((SUBLANES, widths[5]), F32)],
        compiler_params=pltpu.CompilerParams(
            dimension_semantics=("arbitrary",), vmem_limit_bytes=VMEM_LIMIT),
        name="in_proj",
    )(x2, g, w, qg, kg, fb)


def _shift_rows(x, d, fill, row):
    return jnp.where(row >= d, pltpu.roll(x, d, axis=0), fill)


def _lru_kernel(u_ref, cw_ref, cb_ref, wg_ref, bg_ref, lam_ref, ng_ref,
                y_ref, buf_ref, h_ref):
    j = pl.program_id(1)
    tb = u_ref.shape[0]
    w = u_ref.shape[1] // 2
    pad = SUBLANES

    @pl.when(j == 0)
    def _():
        buf_ref[0:pad, :] = jnp.zeros((pad, w), F32)
        h_ref[...] = jnp.zeros_like(h_ref)

    xa = u_ref[:, 0:w]
    ga = u_ref[:, w:2 * w]
    buf_ref[pad:pad + tb, :] = xa
    xc = cb_ref[...] + cw_ref[CONV_WIDTH - 1:CONV_WIDTH, :] * xa
    for d in range(1, CONV_WIDTH):
        xc = xc + cw_ref[CONV_WIDTH - 1 - d:CONV_WIDTH - d, :] * buf_ref[pad - d:pad - d + tb, :]
    buf_ref[0:pad, :] = xa[tb - pad:tb, :]

    gates = _dot(xc.astype(BF16), wg_ref[...]) + bg_ref[...]
    r = jax.nn.sigmoid(gates[:, 0:w])
    i = jax.nn.sigmoid(gates[:, w:2 * w])
    log_a = (-LRU_C) * r * _softplus(-lam_ref[...])
    a = jnp.exp(log_a)
    b = jnp.sqrt(jnp.tanh(-log_a) * (1.0 + a * a)) * (i * xc)

    row = lax.broadcasted_iota(jnp.int32, (tb, w), 0)
    d = 1
    while d < tb:
        a_sh = _shift_rows(a, d, 1.0, row)
        b_sh = _shift_rows(b, d, 0.0, row)
        b = a * b_sh + b
        a = a * a_sh
        d *= 2
    h = b + a * h_ref[0:1, :]
    h_ref[...] = jnp.broadcast_to(h[tb - 1:tb, :], h_ref.shape)

    y = jax.nn.gelu(ga) * h
    y_ref[...] = (_rms(y) * ng_ref[...]).astype(BF16)


def _lru(u, cw, cb, wg, bg, lam, ng, *, batch, seq, tb):
    T, w2 = u.shape
    w = w2 // 2
    nb = seq // tb
    return pl.pallas_call(
        _lru_kernel,
        out_shape=jax.ShapeDtypeStruct((T, w), BF16),
        grid=(batch, nb),
        in_specs=[
            pl.BlockSpec((tb, w2), lambda b, j: (b * nb + j, 0)),
            _resident((CONV_WIDTH, w)),
            _resident((1, w)),
            _resident((w, w2)),
            _resident((1, w2)),
            _resident((1, w)),
            _resident((1, w)),
        ],
        out_specs=pl.BlockSpec((tb, w), lambda b, j: (b * nb + j, 0)),
        scratch_shapes=[pltpu.VMEM((tb + SUBLANES, w), F32), pltpu.VMEM((SUBLANES, w), F32)],
        compiler_params=pltpu.CompilerParams(
            dimension_semantics=("parallel", "arbitrary"), vmem_limit_bytes=VMEM_LIMIT),
        name="lru",
    )(u, cw, cb, wg, bg, lam, ng)


def _fox_kernel(q_ref, k_ref, v_ref, o_ref, s_buf, mx_buf, m_ref, acc_ref, fin_ref, *, tq):
    n_heads = q_ref.shape[1] // LANES
    nq = q_ref.shape[0] // tq
    causal = (lax.broadcasted_iota(jnp.int32, (tq, tq), 0)
              <= lax.broadcasted_iota(jnp.int32, (tq, tq), 1))
    items = [(i, j) for i in range(nq) for j in range(i + 1)]

    def scores(t, u):
        i, j = items[t]
        hl = slice(u * LANES, (u + 1) * LANES)
        s = _dot_nt(k_ref[j * tq:(j + 1) * tq, hl], q_ref[i * tq:(i + 1) * tq, hl])
        if i == j:
            s = jnp.where(causal, s, NEG_BIG)
        s_buf[t % 2, u] = s
        mx_buf[t % 2, u] = jnp.max(s, axis=0, keepdims=True)

    def consume(t, u):
        i, j = items[t]
        slot = t % 2
        vt = jnp.concatenate([v_ref[u * HEAD_DIM:(u + 1) * HEAD_DIM, j * tq:(j + 1) * tq],
                              jnp.ones((FOX_DEN_ROWS, tq), BF16)], axis=0)
        if j == 0:
            m_new = mx_buf[slot, u]
            acc = _dot(vt, jnp.exp2(s_buf[slot, u] - m_new).astype(BF16))
        else:
            m = m_ref[u]
            m_new = jnp.maximum(m, mx_buf[slot, u])
            acc = (jnp.exp2(m - m_new) * acc_ref[u]
                   + _dot(vt, jnp.exp2(s_buf[slot, u] - m_new).astype(BF16)))
        if j < i:
            m_ref[u] = m_new
            acc_ref[u] = acc
        else:
            fin_ref[u * HEAD_DIM:(u + 1) * HEAD_DIM, :] = acc[0:HEAD_DIM] / acc[HEAD_DIM:HEAD_DIM + 1]
            if u == n_heads - 1:
                o_ref[i * tq:(i + 1) * tq, :] = fin_ref[...].T.astype(BF16)

    for u in range(n_heads):
        scores(0, u)
    for t in range(len(items)):
        for u in range(n_heads):
            if t + 1 < len(items):
                scores(t + 1, u)
            consume(t, u)


def _fox(q_aug, k_aug, v_t, *, batch, seq, tq):
    T, wide = q_aug.shape
    heads_per_step = 2
    blk = heads_per_step * LANES
    out_w = heads_per_step * HEAD_DIM
    steps = wide // blk
    return pl.pallas_call(
        functools.partial(_fox_kernel, tq=tq),
        out_shape=jax.ShapeDtypeStruct((T, steps * out_w), BF16),
        grid=(batch, steps),
        in_specs=[
            pl.BlockSpec((seq, blk), lambda b, h: (b, h)),
            pl.BlockSpec((seq, blk), lambda b, h: (b, h)),
            pl.BlockSpec((out_w, seq), lambda b, h: (h, b)),
        ],
        out_specs=pl.BlockSpec((seq, out_w), lambda b, h: (b, h)),
        scratch_shapes=[
            pltpu.VMEM((2, heads_per_step, tq, tq), F32),
            pltpu.VMEM((2, heads_per_step, 1, tq), F32),
            pltpu.VMEM((heads_per_step, 1, tq), F32),
            pltpu.VMEM((heads_per_step, HEAD_DIM + FOX_DEN_ROWS, tq), F32),
            pltpu.VMEM((out_w, tq), F32),
        ],
        compiler_params=pltpu.CompilerParams(
            dimension_semantics=("parallel", "parallel"), vmem_limit_bytes=VMEM_LIMIT),
        name="fox",
    )(q_aug, k_aug, v_t)


def _stack_heads(x, lane_head, n_heads):
    zero = jnp.zeros_like(x)
    return jnp.concatenate([jnp.where(lane_head == h, x, zero) for h in range(n_heads)], axis=0)


def _rwkv_kernel(p_ref, mu_ref, w0_ref, w2_ref, a0_ref, a2_ref, g2_ref, kk_ref, ka_ref,
                 rk_ref, lng_ref, lnb_ref, y_ref, prev_ref, st_ref):
    j = pl.program_id(1)
    tb, cols = p_ref.shape
    w = y_ref.shape[1]
    n_heads = w // HEAD_DIM
    C = CHUNK
    chunk_shift = CHUNK.bit_length() - 1

    @pl.when(j == 0)
    def _():
        prev_ref[...] = jnp.zeros_like(prev_ref)
        st_ref[...] = jnp.zeros_like(st_ref)

    p = p_ref[...]
    row = lax.broadcasted_iota(jnp.int32, (tb, cols), 0)
    prev = jnp.where(row == 0, prev_ref[0:1, :], pltpu.roll(p, 1, axis=0))
    prev_ref[...] = jnp.broadcast_to(p[tb - 1:tb, :], prev_ref.shape)
    ps = p + (prev - p) * mu_ref[...]
    r = ps[:, 0:w]
    k = ps[:, w:2 * w]
    v = ps[:, 2 * w:3 * w]
    lr = ps[:, 3 * w:cols]

    wpre = w0_ref[...] + _dot(jnp.tanh(lr).astype(BF16), w2_ref[...])
    lw = -jnp.exp(-_softplus(-wpre) - 0.5)
    a = jax.nn.sigmoid(a0_ref[...] + _dot(lr.astype(BF16), a2_ref[...]))
    g = _dot(jax.nn.sigmoid(lr).astype(BF16), g2_ref[...])

    ones = _head_ones(w)
    kk = k * kk_ref[...]
    kk = kk / jnp.maximum(jnp.sqrt(_split_dot_right(kk * kk, ones, 1)), 1e-12)
    k2 = k * (1.0 + (a - 1.0) * ka_ref[...])
    b = kk * a
    bonus = _split_dot_right(r * k2 * rk_ref[...], ones, 1) * v

    rr = lax.broadcasted_iota(jnp.int32, (tb, tb), 0)
    cc = lax.broadcasted_iota(jnp.int32, (tb, tb), 1)
    same = (rr >> chunk_shift) == (cc >> chunk_shift)
    tri = jnp.where(same & (cc <= rr), 1.0, 0.0).astype(BF16)
    tot = jnp.where(same, 1.0, 0.0).astype(BF16)
    LW = _split_dot_left(tri, lw, 3)
    LT = _split_dot_left(tot, lw, 3)
    e_out = jnp.exp(-LW)
    e_end = jnp.exp(LT - LW)
    at = (-kk) * jnp.exp(LW - lw)
    rt = r * jnp.exp(LW)
    bt = b * e_out
    kt = k2 * e_out
    b_end = b * e_end
    k_end = k2 * e_end
    w_tot = jnp.exp(LT)

    lane_head = lax.broadcasted_iota(jnp.int32, (C, w), 1) >> HEAD_SHIFT
    t_idx = lax.broadcasted_iota(jnp.int32, (C, w), 0)
    j_idx = lax.broadcasted_iota(jnp.int32, (C, w), 1) & (HEAD_DIM - 1)
    strict = j_idx < t_idx
    incl = j_idx <= t_idx
    eye = jnp.where(j_idx == t_idx, 1.0, 0.0)
    sq_r = lax.broadcasted_iota(jnp.int32, (w, w), 0) >> HEAD_SHIFT
    sq_c = lax.broadcasted_iota(jnp.int32, (w, w), 1) >> HEAD_SHIFT
    block_diag = sq_r == sq_c
    stack = functools.partial(_stack_heads, lane_head=lane_head, n_heads=n_heads)

    chunks = range(tb // C)
    rows_of = [slice(c * C, (c + 1) * C) for c in chunks]
    bf = lambda t: t.astype(BF16)
    cat0 = lambda *ts: jnp.concatenate(ts, axis=0)
    cat1 = lambda *ts: jnp.concatenate(ts, axis=1)

    A = [_dot_nt(bf(cat0(at[sl], rt[sl])), cat0(stack(bf(bt[sl])), stack(bf(kt[sl]))))
         for sl in rows_of]
    a_ab = [jnp.where(strict, a[0:C, 0:w], 0.0) for a in A]
    a_ak = [jnp.where(strict, a[0:C, w:2 * w], 0.0) for a in A]
    a_rb = [jnp.where(incl, a[C:2 * C, 0:w], 0.0) for a in A]
    a_rk = [jnp.where(incl, a[C:2 * C, w:2 * w], 0.0) for a in A]

    pw = [_dot(bf(l), stack(bf(l))) for l in a_ab]
    tinv = [eye + l for l in a_ab]
    sz = 2
    while 2 * sz < C:
        both = [_dot(bf(cat0(t, p)), stack(bf(p))) for t, p in zip(tinv, pw)]
        tinv = [t + r[0:C] for t, r in zip(tinv, both)]
        pw = [r[C:2 * C] for r in both]
        sz *= 2
    tinv = [t + _dot(bf(t), stack(bf(p))) for t, p in zip(tinv, pw)]

    yo = [_dot(bf(cat0(ak, ark)), stack(bf(v[sl]))) for ak, ark, sl in zip(a_ak, a_rk, rows_of)]
    gu = [_dot(bf(t), cat1(stack(bf(at[sl])), stack(bf(y[0:C]))))
          for t, y, sl in zip(tinv, yo, rows_of)]
    pq = [_dot(bf(rb), cat1(stack(bf(x[:, 0:w])), stack(bf(x[:, w:2 * w]))))
          for rb, x in zip(a_rb, gu)]
    p_mat = [rt[sl] + x[:, 0:w] for sl, x in zip(rows_of, pq)]
    q_mat = [y[C:2 * C] + x[:, w:2 * w] for y, x in zip(yo, pq)]
    m_mat = [jnp.where(block_diag, _dot_tn(bf(x[:, 0:w]), bf(b_end[sl])), 0.0)
             for x, sl in zip(gu, rows_of)]
    n_mat = [jnp.where(block_diag,
                       _dot_tn(bf(cat0(x[:, w:2 * w], v[sl])), bf(cat0(b_end[sl], k_end[sl]))), 0.0)
             for x, sl in zip(gu, rows_of)]

    st = st_ref[...]
    outs = []
    for c in chunks:
        stb = bf(st)
        outs.append(_dot_nt(bf(p_mat[c]), stb) + q_mat[c])
        st = st * w_tot[c * C:c * C + 1, :] + _dot(stb, bf(m_mat[c])) + n_mat[c]
    st_ref[...] = st

    o = jnp.concatenate(outs, axis=0)
    inv_d = 1.0 / HEAD_DIM
    mean = _split_dot_right(o, ones, 1) * inv_d
    cen = o - mean
    var = _split_dot_right(cen * cen, ones, 1) * inv_d
    o = cen * lax.rsqrt(var + GN_EPS) * lng_ref[...] + lnb_ref[...]
    y_ref[...] = ((o + bonus) * g).astype(BF16)


def _rwkv(p, mu, w0, w2, a0, a2, g2, kkp, ka, rk, lng, lnb, *, batch, seq, tb):
    T, cols = p.shape
    w = w0.shape[1]
    nb = seq // tb
    vec = _resident((1, w))
    low = _resident((cols - 3 * w, w))
    return pl.pallas_call(
        _rwkv_kernel,
        out_shape=jax.ShapeDtypeStruct((T, w), BF16),
        grid=(batch, nb),
        in_specs=[
            pl.BlockSpec((tb, cols), lambda b, j: (b * nb + j, 0)),
            _resident((1, cols)), vec, low, vec, low, low, vec, vec, vec, vec, vec,
        ],
        out_specs=pl.BlockSpec((tb, w), lambda b, j: (b * nb + j, 0)),
        scratch_shapes=[pltpu.VMEM((SUBLANES, cols), F32), pltpu.VMEM((w, w), F32)],
        compiler_params=pltpu.CompilerParams(
            dimension_semantics=("parallel", "arbitrary"), vmem_limit_bytes=VMEM_LIMIT),
        name="rwkv",
    )(p, mu, w0, w2, a0, a2, g2, kkp, ka, rk, lng, lnb)


def _out_proj_kernel(ya_ref, yb_ref, yc_ref, x_ref, wa_ref, wb_ref, wc_ref, fg_ref, n2_ref,
                     wr_ref, br_ref, x1_ref, h2_ref, comb_ref, *, sub):
    for r0 in range(0, x_ref.shape[0], sub):
        rows = slice(r0, r0 + sub)
        ybn = (_rms(yb_ref[rows, :].astype(F32)) * fg_ref[...]).astype(BF16)
        x1 = (x_ref[rows, :] + _dot(ya_ref[rows, :], wa_ref[...]) + _dot(ybn, wb_ref[...])
              + _dot(yc_ref[rows, :], wc_ref[...]))
        x1_ref[rows, :] = x1
        h2 = _rms(x1) * n2_ref[...]
        h_hi = h2.astype(BF16)
        h2_ref[rows, :] = h_hi

        h_lo = (h2 - h_hi.astype(F32)).astype(BF16)
        hw = _dot(h_hi, wr_ref[...])
        logits = (hw[:, 0:LANES] + hw[:, LANES:2 * LANES] + _dot(h_lo, wr_ref[:, 0:LANES])
                  + br_ref[...])
        comb_ref[rows, :] = _route(logits)


def _route(logits):
    lane = lax.broadcasted_iota(jnp.int32, logits.shape, 1)
    lane_f = lane.astype(F32)
    far = float(LANES)

    def first_argmax(vals, vmax):
        return jnp.min(jnp.where(vals == vmax, lane_f, far), axis=-1, keepdims=True)

    gl = jnp.where(lane < N_GROUPS, logits, NEG_BIG)
    gmax = jnp.max(gl, axis=-1, keepdims=True)
    gidx = first_argmax(gl, gmax)
    g_p = 1.0 / jnp.sum(jnp.exp(gl - gmax), axis=-1, keepdims=True)

    e_group = ((lane - N_GROUPS) >> 2).astype(F32)
    el = jnp.where((lane >= N_GROUPS) & (e_group == gidx), logits, NEG_BIG)
    emax = jnp.max(el, axis=-1, keepdims=True)
    esum = jnp.sum(jnp.exp(el - emax), axis=-1, keepdims=True)
    i1 = first_argmax(el, emax)
    el2 = jnp.where(lane_f == i1, NEG_BIG, el)
    emax2 = jnp.max(el2, axis=-1, keepdims=True)
    i2 = first_argmax(el2, emax2)
    p1 = 1.0 / esum
    p2 = jnp.exp(emax2 - emax) / esum
    den = p1 + p2
    return (jnp.where(lane_f == i1, g_p * (p1 / den), 0.0)
            + jnp.where(lane_f == i2, g_p * (p2 / den), 0.0)
            + jnp.where(lane_f == gidx, 1.0, 0.0))


def _out_proj(ya, yb, yc, x2, wa, wb, wc, fg, n2, wr, br, *, tm):
    T, D = x2.shape
    row = lambda i: (i, 0)
    return pl.pallas_call(
        functools.partial(_out_proj_kernel, sub=tm // 2),
        out_shape=(jax.ShapeDtypeStruct((T, D), F32), jax.ShapeDtypeStruct((T, D), BF16),
                   jax.ShapeDtypeStruct((T, LANES), F32)),
        grid=(T // tm,),
        in_specs=[
            pl.BlockSpec((tm, ya.shape[1]), row),
            pl.BlockSpec((tm, yb.shape[1]), row),
            pl.BlockSpec((tm, yc.shape[1]), row),
            pl.BlockSpec((tm, D), row),
            _resident(wa.shape), _resident(wb.shape), _resident(wc.shape),
            _resident(fg.shape), _resident(n2.shape), _resident(wr.shape), _resident(br.shape),
        ],
        out_specs=(pl.BlockSpec((tm, D), row), pl.BlockSpec((tm, D), row),
                   pl.BlockSpec((tm, LANES), row)),
        compiler_params=pltpu.CompilerParams(
            dimension_semantics=("parallel",), vmem_limit_bytes=VMEM_LIMIT),
        name="out_proj",
    )(ya, yb, yc, x2, wa, wb, wc, fg, n2, wr, br)


def _moe_kernel(h_ref, route_ref, x_ref, wg_ref, wu_ref, wd_ref, o_ref, hs_ref, rs_ref, ys_ref):
    tm = h_ref.shape[0]
    n_groups = wd_ref.shape[0]
    route = route_ref[...]
    lane = lax.broadcasted_iota(jnp.int32, (tm, LANES), 1)
    onehot = jnp.where(lane < n_groups, route, 0.0)
    rr = lax.broadcasted_iota(jnp.int32, (tm, tm), 0)
    cc = lax.broadcasted_iota(jnp.int32, (tm, tm), 1)
    tri = jnp.where(cc <= rr, 1.0, 0.0).astype(BF16)
    cum = _dot(tri, onehot.astype(BF16))
    tot = cum[tm - 1:tm, :]
    start = jnp.zeros_like(tot)
    for d in range(1, n_groups):
        start = start + jnp.where(lane[0:1] >= d, pltpu.roll(tot, d, axis=1), 0.0)
    end = start + tot
    pos = jnp.sum(jnp.where(onehot > 0.0, start + cum - 1.0, 0.0), axis=-1, keepdims=True)
    perm_t = jnp.where(pos == cc.astype(F32), 1.0, 0.0).astype(BF16)

    hs_ref[...] = _dot_tn(perm_t, h_ref[...]).astype(BF16)
    rem = route
    sorted_route = None
    for _ in range(2):
        hi = rem.astype(BF16)
        t = _dot_tn(perm_t, hi)
        sorted_route = t if sorted_route is None else sorted_route + t
        rem = rem - hi.astype(F32)
    rs_ref[...] = sorted_route
    ys_ref[...] = jnp.zeros_like(ys_ref)

    for g in range(n_groups):
        g_start = start[0, g]
        g_end = end[0, g]
        for r0 in range(0, tm, MOE_BLOCK):

            @pl.when((g_start < r0 + MOE_BLOCK) & (g_end > r0))
            def _(g=g, r0=r0):
                rows = slice(r0, r0 + MOE_BLOCK)
                hb = hs_ref[rows, :]
                rw = rs_ref[rows, :]
                hids = []
                for e in range(g * EXPERTS_PER_GROUP, (g + 1) * EXPERTS_PER_GROUP):
                    gate = _dot(hb, wg_ref[e])
                    hid = gate * jax.nn.sigmoid(gate) * _dot(hb, wu_ref[e])
                    hids.append((hid * rw[:, N_GROUPS + e:N_GROUPS + e + 1]).astype(BF16))
                ys_ref[rows, :] += _dot(jnp.concatenate(hids, axis=1), wd_ref[g])

    o_ref[...] = x_ref[...] + _dot(perm_t, ys_ref[...].astype(BF16))


def _moe(h2, route, x1, wg, wu, wd, *, tm):
    T, D = x1.shape
    row = lambda i: (i, 0)
    return pl.pallas_call(
        _moe_kernel,
        out_shape=jax.ShapeDtypeStruct((T, D), F32),
        grid=(T // tm,),
        in_specs=[
            pl.BlockSpec((tm, D), row),
            pl.BlockSpec((tm, LANES), row),
            pl.BlockSpec((tm, D), row),
            _resident(wg.shape),
            _resident(wu.shape),
            _resident(wd.shape),
        ],
        out_specs=pl.BlockSpec((tm, D), row),
        scratch_shapes=[pltpu.VMEM((tm, D), BF16), pltpu.VMEM((tm, LANES), F32),
                        pltpu.VMEM((tm, D), F32)],
        compiler_params=pltpu.CompilerParams(
            dimension_semantics=("parallel",), vmem_limit_bytes=VMEM_LIMIT),
        name="moe",
    )(h2, route, x1, wg, wu, wd)


def _block_diag(wh):
    n_heads, d, _ = wh.shape
    eye = jnp.eye(n_heads, dtype=wh.dtype)
    return jnp.einsum("hij,hg->higj", wh, eye).reshape(n_heads * d, n_heads * d)


def _row(vec):
    return vec.reshape(1, -1).astype(F32)


def _pad_rows(mat, start, total):
    return jnp.zeros((total, mat.shape[1]), mat.dtype).at[start:start + mat.shape[0]].set(mat)


def kernel(x, norm1_g, w_in, conv_w, conv_b, lru_wa, lru_ba, lru_wx, lru_bx, lru_lambda,
           lru_norm_g, fox_fb, fox_qnorm_g, fox_knorm_g, fox_norm_g, rwkv_mu, rwkv_w0,
           rwkv_w2, rwkv_a0, rwkv_a2, rwkv_g2, rwkv_kk, rwkv_ka, rwkv_rk, rwkv_ln_g,
           rwkv_ln_b, w_out, norm2_g, router_gw, router_gb, router_ew, router_eb,
           exp_w_gate, exp_w_up, exp_w_down):
    batch, seq, d_model = x.shape
    depth = w_in.shape[0]
    lru_w = conv_w.shape[2]
    fox_heads = fox_fb.shape[1]
    fox_w = fox_heads * HEAD_DIM
    rwkv_w = rwkv_w0.shape[1]
    rwkv_cols = rwkv_mu.shape[1]
    d_rank, a_rank, g_rank = rwkv_w2.shape[1], rwkv_a2.shape[1], rwkv_g2.shape[1]
    low = d_rank + a_rank + g_rank
    n_exp, _, d_exp = exp_w_gate.shape[1:]
    assert low == LANES and rwkv_cols == 3 * rwkv_w + low
    assert fox_heads <= LANES and n_exp == N_EXPERTS

    o_fox = 2 * lru_w
    o_fl = o_fox + 3 * fox_w
    o_rwkv = o_fl + fox_heads
    seg = (0, o_fox, o_fox + fox_w, o_fox + 2 * fox_w, o_fl, o_fl + rwkv_cols,
           o_fl + rwkv_cols + LANES)

    x2 = x.reshape(batch * seq, d_model)
    for l in range(depth):
        wl = w_in[l]
        n_fl = FOX_BIAS_PIECES * fox_heads
        w_perm = jnp.concatenate(
            [wl[:, 0:o_fl], wl[:, o_rwkv:o_rwkv + rwkv_cols],
             jnp.repeat(wl[:, o_fl:o_rwkv], FOX_BIAS_PIECES, axis=1),
             jnp.zeros((d_model, LANES - n_fl), wl.dtype)], axis=1).astype(BF16)
        qg = _row(jnp.tile(fox_qnorm_g[l], fox_heads) * (HEAD_DIM ** -0.5 * LOG2E))
        kg = _row(jnp.tile(fox_knorm_g[l], fox_heads))
        fb = _row(jnp.pad(jnp.repeat(fox_fb[l], FOX_BIAS_PIECES), (0, LANES - n_fl)))
        u, q, k_aug, v, p = _in_proj(x2, _row(norm1_g[l]), w_perm, qg, kg, fb,
                                    seq=seq, tm=512, seg=seg)

        wg = jnp.concatenate([_block_diag(lru_wa[l]), _block_diag(lru_wx[l])], axis=1).astype(BF16)
        bg = _row(jnp.concatenate([lru_ba[l], lru_bx[l]]))
        ya = _lru(u, conv_w[l], _row(conv_b[l]), wg, bg, _row(lru_lambda[l]),
                  _row(lru_norm_g[l]), batch=batch, seq=seq, tb=512)

        yb = _fox(q, k_aug, v, batch=batch, seq=seq, tq=512)

        yc = _rwkv(p, _row(rwkv_mu[l]), _row(rwkv_w0[l]),
                   _pad_rows(rwkv_w2[l], 0, low).astype(BF16), _row(rwkv_a0[l]),
                   _pad_rows(rwkv_a2[l], d_rank, low).astype(BF16),
                   _pad_rows(rwkv_g2[l], d_rank + a_rank, low).astype(BF16),
                   _row(rwkv_kk[l]), _row(rwkv_ka[l]), _row(rwkv_rk[l]),
                   _row(rwkv_ln_g[l]), _row(rwkv_ln_b[l]), batch=batch, seq=seq, tb=512)

        wo = w_out[l].astype(BF16)
        wr = jnp.concatenate(
            [router_gw[l], router_ew[l],
             jnp.zeros((d_model, LANES - N_GROUPS - n_exp), F32)], axis=1)
        wr_hi = wr.astype(BF16)
        wr = jnp.concatenate([wr_hi, (wr - wr_hi.astype(F32)).astype(BF16)], axis=1)
        br = _row(jnp.pad(jnp.concatenate([router_gb[l], router_eb[l]]),
                          (0, LANES - N_GROUPS - n_exp)))
        x1, h2, comb = _out_proj(
            ya, yb, yc, x2, wo[0:lru_w], wo[lru_w:lru_w + fox_w], wo[lru_w + fox_w:],
            _row(fox_norm_g[l]), _row(norm2_g[l]), wr, br, tm=512)

        wd = exp_w_down[l].astype(BF16).reshape(N_GROUPS, EXPERTS_PER_GROUP * d_exp, d_model)
        x2 = _moe(h2, comb, x1, exp_w_gate[l].astype(BF16), exp_w_up[l].astype(BF16), wd, tm=512)
    return x2.reshape(batch, seq, d_model)
```

```python
import functools

import jax
import jax.numpy as jnp
from jax import lax
from jax.experimental import pallas as pl
from jax.experimental.pallas import tpu as pltpu

F32 = jnp.float32
BF16 = jnp.bfloat16

NORM_EPS = 1e-6
GN_EPS = 64e-5
LRU_C = 8.0
HEAD_DIM = 64
HEAD_SHIFT = 6
LANES = 128
SUBLANES = 8
CHUNK = 64
CONV_WIDTH = 4
N_GROUPS = 4
EXPERTS_PER_GROUP = 4
N_EXPERTS = N_GROUPS * EXPERTS_PER_GROUP
MOE_BLOCK = 128
NEG_BIG = -1e30
LOG2E = 1.4426950408889634
FOX_BIAS_PIECES = 3
FOX_DEN_ROWS = 16
VMEM_LIMIT = 56 * 1024 * 1024


def _dot(a, b):
    return jnp.dot(a, b, preferred_element_type=F32)


def _dot_nt(a, b):
    return lax.dot_general(a, b, (((1,), (1,)), ((), ())), preferred_element_type=F32)


def _dot_tn(a, b):
    return lax.dot_general(a, b, (((0,), (0,)), ((), ())), preferred_element_type=F32)


def _split_dot_right(x, ones, parts):
    acc = None
    rem = x
    for _ in range(parts):
        hi = rem.astype(BF16)
        t = _dot(hi, ones)
        acc = t if acc is None else acc + t
        rem = rem - hi.astype(F32)
    return acc


def _split_dot_left(ones, x, parts):
    acc = None
    rem = x
    for _ in range(parts):
        hi = rem.astype(BF16)
        t = _dot(ones, hi)
        acc = t if acc is None else acc + t
        rem = rem - hi.astype(F32)
    return acc


def _head_ones(n):
    r = lax.broadcasted_iota(jnp.int32, (n, n), 0) >> HEAD_SHIFT
    c = lax.broadcasted_iota(jnp.int32, (n, n), 1) >> HEAD_SHIFT
    return jnp.where(r == c, 1.0, 0.0).astype(BF16)


def _softplus(z):
    return jnp.maximum(z, 0.0) + jnp.log(1.0 + jnp.exp(-jnp.abs(z)))


def _rms(x, eps=NORM_EPS):
    return x * lax.rsqrt(jnp.mean(x * x, axis=-1, keepdims=True) + eps)


def _resident(shape):
    zeros = (0,) * len(shape)
    return pl.BlockSpec(shape, lambda *_: zeros, pipeline_mode=pl.Buffered(1))


def _in_proj_kernel(x_ref, g_ref, w_ref, qg_ref, kg_ref, fb_ref,
                    lru_ref, q_ref, k_ref, v_ref, p_ref, carry_ref,
                    *, tiles_per_seq, seg, sub):
    i = pl.program_id(0)

    @pl.when(i % tiles_per_seq == 0)
    def _():
        carry_ref[...] = jnp.zeros_like(carry_ref)

    fox = seg[2] - seg[1]
    n_heads = fox // HEAD_DIM
    ones = _head_ones(fox)
    inv_d = 1.0 / HEAD_DIM
    rr = lax.broadcasted_iota(jnp.int32, (sub, sub), 0)
    cc = lax.broadcasted_iota(jnp.int32, (sub, sub), 1)
    tri = jnp.where(cc <= rr, 1.0, 0.0).astype(BF16)
    lane = lax.broadcasted_iota(jnp.int32, (sub, LANES), 1)
    feat = lane < HEAD_DIM
    bias_lanes = (lane >= HEAD_DIM) & (lane < HEAD_DIM + FOX_BIAS_PIECES)
    q_bias = jnp.where(bias_lanes, 1.0, 0.0)
    piece_id = lax.rem(lane, FOX_BIAS_PIECES)
    carry = carry_ref[0:1, :]

    def head_norm(t):
        return t * lax.rsqrt(_dot((t * t).astype(BF16), ones) * inv_d + NORM_EPS)

    for r0 in range(0, x_ref.shape[0], sub):
        rows = slice(r0, r0 + sub)
        hb = (_rms(x_ref[rows, :]) * g_ref[...]).astype(BF16)
        lru_ref[rows, :] = _dot(hb, w_ref[:, seg[0]:seg[1]])
        q = head_norm(_dot(hb, w_ref[:, seg[1]:seg[2]])) * qg_ref[...]
        k = head_norm(_dot(hb, w_ref[:, seg[2]:seg[3]])) * kg_ref[...]
        v_ref[:, rows] = _dot(hb, w_ref[:, seg[3]:seg[4]]).T.astype(BF16)
        p_ref[rows, :] = _dot(hb, w_ref[:, seg[4]:seg[5]])

        lf = -_softplus(-(_dot(hb, w_ref[:, seg[5]:seg[6]]) + fb_ref[...]))
        c = _split_dot_left(tri, lf, 3) + carry
        carry = c[sub - 1:sub, :]

        rem = c * (-LOG2E)
        pieces = None
        for piece in range(FOX_BIAS_PIECES):
            hi = rem.astype(BF16).astype(F32)
            pieces = hi if pieces is None else jnp.where(piece_id == piece, hi, pieces)
            rem = rem - hi

        for h in range(n_heads):
            src = slice((h // 2) * LANES, (h // 2 + 1) * LANES)
            dst = slice(h * LANES, (h + 1) * LANES)
            qh, kh = q[:, src], k[:, src]
            if h % 2:
                qh = pltpu.roll(qh, HEAD_DIM, axis=1)
                kh = pltpu.roll(kh, HEAD_DIM, axis=1)
            k_bias = pltpu.roll(pieces, HEAD_DIM - FOX_BIAS_PIECES * h, axis=1)
            q_ref[dst, rows] = jnp.where(feat, qh, q_bias).T.astype(BF16)
            k_ref[rows, dst] = jnp.where(feat, kh, jnp.where(bias_lanes, k_bias, 0.0)).astype(BF16)

    carry_ref[...] = jnp.broadcast_to(carry, carry_ref.shape)


def _in_proj(x2, g, w, qg, kg, fb, *, seq, tm, seg):
    T, D = x2.shape
    n_out = w.shape[1]
    widths = [seg[j + 1] - seg[j] for j in range(6)]
    row = lambda i: (i, 0)
    col = lambda i: (0, i)
    outs = [
        ((T, widths[0]), (tm, widths[0]), row, F32),
        ((2 * widths[1], T), (2 * widths[1], tm), col, BF16),
        ((T, 2 * widths[2]), (tm, 2 * widths[2]), row, BF16),
        ((widths[3], T), (widths[3], tm), col, BF16),
        ((T, widths[4]), (tm, widths[4]), row, F32),
    ]
    return pl.pallas_call(
        functools.partial(_in_proj_kernel, tiles_per_seq=seq // tm, seg=seg, sub=tm // 2),
        out_shape=tuple(jax.ShapeDtypeStruct(shape, dt) for shape, _, _, dt in outs),
        grid=(T // tm,),
        in_specs=[
            pl.BlockSpec((tm, D), row),
            _resident((1, D)),
            _resident((D, n_out)),
            _resident((1, widths[1])),
            _resident((1, widths[2])),
            _resident((1, widths[5])),
        ],
        out_specs=tuple(pl.BlockSpec(blk, imap) for _, blk, imap, _ in outs),
        scratch_shapes=[pltpu.VMEM((SUBLANES, widths[5]), F32)],
        compiler_params=pltpu.CompilerParams(
            dimension_semantics=("arbitrary",), vmem_limit_bytes=VMEM_LIMIT),
        name="in_proj",
    )(x2, g, w, qg, kg, fb)


def _shift_rows(x, d, fill, row):
    return jnp.where(row >= d, pltpu.roll(x, d, axis=0), fill)


def _lru_kernel(u_ref, cw_ref, cb_ref, wg_ref, bg_ref, lam_ref, ng_ref,
                y_ref, buf_ref, h_ref):
    j = pl.program_id(1)
    tb = u_ref.shape[0]
    w = u_ref.shape[1] // 2
    pad = SUBLANES

    @pl.when(j == 0)
    def _():
        buf_ref[0:pad, :] = jnp.zeros((pad, w), F32)
        h_ref[...] = jnp.zeros_like(h_ref)

    xa = u_ref[:, 0:w]
    ga = u_ref[:, w:2 * w]
    buf_ref[pad:pad + tb, :] = xa
    xc = cb_ref[...] + cw_ref[CONV_WIDTH - 1:CONV_WIDTH, :] * xa
    for d in range(1, CONV_WIDTH):
        xc = xc + cw_ref[CONV_WIDTH - 1 - d:CONV_WIDTH - d, :] * buf_ref[pad - d:pad - d + tb, :]
    buf_ref[0:pad, :] = xa[tb - pad:tb, :]

    gates = _dot(xc.astype(BF16), wg_ref[...]) + bg_ref[...]
    r = jax.nn.sigmoid(gates[:, 0:w])
    i = jax.nn.sigmoid(gates[:, w:2 * w])
    log_a = (-LRU_C) * r * _softplus(-lam_ref[...])
    a = jnp.exp(log_a)
    b = jnp.sqrt(jnp.tanh(-log_a) * (1.0 + a * a)) * (i * xc)

    row = lax.broadcasted_iota(jnp.int32, (tb, w), 0)
    d = 1
    while d < tb:
        a_sh = _shift_rows(a, d, 1.0, row)
        b_sh = _shift_rows(b, d, 0.0, row)
        b = a * b_sh + b
        a = a * a_sh
        d *= 2
    h = b + a * h_ref[0:1, :]
    h_ref[...] = jnp.broadcast_to(h[tb - 1:tb, :], h_ref.shape)

    y = jax.nn.gelu(ga) * h
    y_ref[...] = (_rms(y) * ng_ref[...]).astype(BF16)


def _lru(u, cw, cb, wg, bg, lam, ng, *, batch, seq, tb):
    T, w2 = u.shape
    w = w2 // 2
    nb = seq // tb
    return pl.pallas_call(
        _lru_kernel,
        out_shape=jax.ShapeDtypeStruct((T, w), BF16),
        grid=(batch, nb),
        in_specs=[
            pl.BlockSpec((tb, w2), lambda b, j: (b * nb + j, 0)),
            _resident((CONV_WIDTH, w)),
            _resident((1, w)),
            _resident((w, w2)),
            _resident((1, w2)),
            _resident((1, w)),
            _resident((1, w)),
        ],
        out_specs=pl.BlockSpec((tb, w), lambda b, j: (b * nb + j, 0)),
        scratch_shapes=[pltpu.VMEM((tb + SUBLANES, w), F32), pltpu.VMEM((SUBLANES, w), F32)],
        compiler_params=pltpu.CompilerParams(
            dimension_semantics=("parallel", "arbitrary"), vmem_limit_bytes=VMEM_LIMIT),
        name="lru",
    )(u, cw, cb, wg, bg, lam, ng)


def _fox_kernel(q_ref, k_ref, v_ref, o_ref, s_buf, mx_buf, p_buf, m_ref, acc_ref, fin_ref, *, tq):
    n_heads = k_ref.shape[1] // LANES
    nq = k_ref.shape[0] // tq
    causal = (lax.broadcasted_iota(jnp.int32, (tq, tq), 0)
              <= lax.broadcasted_iota(jnp.int32, (tq, tq), 1))
    items = [(i, j) for i in range(nq) for j in range(i + 1)]

    def scores(t, u):
        i, j = items[t]
        hl = slice(u * LANES, (u + 1) * LANES)
        s = _dot(k_ref[j * tq:(j + 1) * tq, hl], q_ref[hl, i * tq:(i + 1) * tq])
        if i == j:
            s = jnp.where(causal, s, NEG_BIG)
        s_buf[t % 2, u] = s
        mx_buf[t % 2, u] = jnp.max(s, axis=0, keepdims=True)

    def consume(t, u):
        i, j = items[t]
        slot = t % 2
        vt = jnp.concatenate([v_ref[u * HEAD_DIM:(u + 1) * HEAD_DIM, j * tq:(j + 1) * tq],
                              jnp.ones((FOX_DEN_ROWS, tq), BF16)], axis=0)
        m_new = mx_buf[slot, u] if j == 0 else jnp.maximum(m_ref[u], mx_buf[slot, u])
        p_buf[u] = jnp.exp2(s_buf[slot, u] - m_new).astype(BF16)
        acc = _dot(vt, p_buf[u])
        if j > 0:
            acc = jnp.exp2(m_ref[u] - m_new) * acc_ref[u] + acc
        if j < i:
            m_ref[u] = m_new
            acc_ref[u] = acc
        else:
            fin_ref[u * HEAD_DIM:(u + 1) * HEAD_DIM, :] = acc[0:HEAD_DIM] / acc[HEAD_DIM:HEAD_DIM + 1]
            if u == n_heads - 1:
                o_ref[i * tq:(i + 1) * tq, :] = fin_ref[...].T.astype(BF16)

    for u in range(n_heads):
        scores(0, u)
    for t in range(len(items)):
        for u in range(n_heads):
            if t + 1 < len(items):
                scores(t + 1, u)
            consume(t, u)


def _fox(q_aug, k_aug, v_t, *, batch, seq, tq):
    T, wide = k_aug.shape
    heads_per_step = 2
    blk = heads_per_step * LANES
    out_w = heads_per_step * HEAD_DIM
    steps = wide // blk
    return pl.pallas_call(
        functools.partial(_fox_kernel, tq=tq),
        out_shape=jax.ShapeDtypeStruct((T, steps * out_w), BF16),
        grid=(batch, steps),
        in_specs=[
            pl.BlockSpec((blk, seq), lambda b, h: (h, b)),
            pl.BlockSpec((seq, blk), lambda b, h: (b, h)),
            pl.BlockSpec((out_w, seq), lambda b, h: (h, b)),
        ],
        out_specs=pl.BlockSpec((seq, out_w), lambda b, h: (b, h)),
        scratch_shapes=[
            pltpu.VMEM((2, heads_per_step, tq, tq), F32),
            pltpu.VMEM((2, heads_per_step, 1, tq), F32),
            pltpu.VMEM((heads_per_step, tq, tq), BF16),
            pltpu.VMEM((heads_per_step, 1, tq), F32),
            pltpu.VMEM((heads_per_step, HEAD_DIM + FOX_DEN_ROWS, tq), F32),
            pltpu.VMEM((out_w, tq), F32),
        ],
        compiler_params=pltpu.CompilerParams(
            dimension_semantics=("parallel", "parallel"), vmem_limit_bytes=VMEM_LIMIT),
        name="fox",
    )(q_aug, k_aug, v_t)


def _stack_heads(x, lane_head, n_heads):
    zero = jnp.zeros_like(x)
    return jnp.concatenate([jnp.where(lane_head == h, x, zero) for h in range(n_heads)], axis=0)


def _rwkv_kernel(p_ref, mu_ref, w0_ref, w2_ref, a0_ref, a2_ref, g2_ref, kk_ref, ka_ref,
                 rk_ref, lng_ref, lnb_ref, y_ref, prev_ref, st_ref):
    j = pl.program_id(1)
    tb, cols = p_ref.shape
    w = y_ref.shape[1]
    n_heads = w // HEAD_DIM
    C = CHUNK
    chunk_shift = CHUNK.bit_length() - 1

    @pl.when(j == 0)
    def _():
        prev_ref[...] = jnp.zeros_like(prev_ref)
        st_ref[...] = jnp.zeros_like(st_ref)

    p = p_ref[...]
    row = lax.broadcasted_iota(jnp.int32, (tb, cols), 0)
    prev = jnp.where(row == 0, prev_ref[0:1, :], pltpu.roll(p, 1, axis=0))
    prev_ref[...] = jnp.broadcast_to(p[tb - 1:tb, :], prev_ref.shape)
    ps = p + (prev - p) * mu_ref[...]
    r = ps[:, 0:w]
    k = ps[:, w:2 * w]
    v = ps[:, 2 * w:3 * w]
    lr = ps[:, 3 * w:cols]

    wpre = w0_ref[...] + _dot(jnp.tanh(lr).astype(BF16), w2_ref[...])
    lw = -jnp.exp(-_softplus(-wpre) - 0.5)
    a = jax.nn.sigmoid(a0_ref[...] + _dot(lr.astype(BF16), a2_ref[...]))
    g = _dot(jax.nn.sigmoid(lr).astype(BF16), g2_ref[...])

    ones = _head_ones(w)
    kk = k * kk_ref[...]
    kk = kk / jnp.maximum(jnp.sqrt(_split_dot_right(kk * kk, ones, 1)), 1e-12)
    k2 = k * (1.0 + (a - 1.0) * ka_ref[...])
    b = kk * a
    bonus = _split_dot_right(r * k2 * rk_ref[...], ones, 1) * v

    rr = lax.broadcasted_iota(jnp.int32, (tb, tb), 0)
    cc = lax.broadcasted_iota(jnp.int32, (tb, tb), 1)
    same = (rr >> chunk_shift) == (cc >> chunk_shift)
    tri = jnp.where(same & (cc <= rr), 1.0, 0.0).astype(BF16)
    tot = jnp.where(same, 1.0, 0.0).astype(BF16)
    LW = _split_dot_left(tri, lw, 3)
    LT = _split_dot_left(tot, lw, 3)
    e_out = jnp.exp(-LW)
    e_end = jnp.exp(LT - LW)
    at = (-kk) * jnp.exp(LW - lw)
    rt = r * jnp.exp(LW)
    bt = b * e_out
    kt = k2 * e_out
    b_end = b * e_end
    k_end = k2 * e_end
    w_tot = jnp.exp(LT)

    lane_head = lax.broadcasted_iota(jnp.int32, (C, w), 1) >> HEAD_SHIFT
    t_idx = lax.broadcasted_iota(jnp.int32, (C, w), 0)
    j_idx = lax.broadcasted_iota(jnp.int32, (C, w), 1) & (HEAD_DIM - 1)
    strict = j_idx < t_idx
    incl = j_idx <= t_idx
    eye = jnp.where(j_idx == t_idx, 1.0, 0.0)
    sq_r = lax.broadcasted_iota(jnp.int32, (w, w), 0) >> HEAD_SHIFT
    sq_c = lax.broadcasted_iota(jnp.int32, (w, w), 1) >> HEAD_SHIFT
    block_diag = sq_r == sq_c
    stack = functools.partial(_stack_heads, lane_head=lane_head, n_heads=n_heads)

    chunks = range(tb // C)
    rows_of = [slice(c * C, (c + 1) * C) for c in chunks]
    bf = lambda t: t.astype(BF16)
    cat0 = lambda *ts: jnp.concatenate(ts, axis=0)
    cat1 = lambda *ts: jnp.concatenate(ts, axis=1)

    A = [_dot_nt(bf(cat0(at[sl], rt[sl])), cat0(stack(bf(bt[sl])), stack(bf(kt[sl]))))
         for sl in rows_of]
    a_ab = [jnp.where(strict, a[0:C, 0:w], 0.0) for a in A]
    a_ak = [jnp.where(strict, a[0:C, w:2 * w], 0.0) for a in A]
    a_rb = [jnp.where(incl, a[C:2 * C, 0:w], 0.0) for a in A]
    a_rk = [jnp.where(incl, a[C:2 * C, w:2 * w], 0.0) for a in A]

    pw = [_dot(bf(l), stack(bf(l))) for l in a_ab]
    tinv = [eye + l for l in a_ab]
    sz = 2
    while 2 * sz < C:
        both = [_dot(bf(cat0(t, p)), stack(bf(p))) for t, p in zip(tinv, pw)]
        tinv = [t + r[0:C] for t, r in zip(tinv, both)]
        pw = [r[C:2 * C] for r in both]
        sz *= 2
    tinv = [t + _dot(bf(t), stack(bf(p))) for t, p in zip(tinv, pw)]

    yo = [_dot(bf(cat0(ak, ark)), stack(bf(v[sl]))) for ak, ark, sl in zip(a_ak, a_rk, rows_of)]
    gu = [_dot(bf(t), cat1(stack(bf(at[sl])), stack(bf(y[0:C]))))
          for t, y, sl in zip(tinv, yo, rows_of)]
    pq = [_dot(bf(rb), cat1(stack(bf(x[:, 0:w])), stack(bf(x[:, w:2 * w]))))
          for rb, x in zip(a_rb, gu)]
    p_mat = [rt[sl] + x[:, 0:w] for sl, x in zip(rows_of, pq)]
    q_mat = [y[C:2 * C] + x[:, w:2 * w] for y, x in zip(yo, pq)]
    m_mat = [jnp.where(block_diag, _dot_tn(bf(x[:, 0:w]), bf(b_end[sl])), 0.0)
             for x, sl in zip(gu, rows_of)]
    n_mat = [jnp.where(block_diag,
                       _dot_tn(bf(cat0(x[:, w:2 * w], v[sl])), bf(cat0(b_end[sl], k_end[sl]))), 0.0)
             for x, sl in zip(gu, rows_of)]

    st = st_ref[...]
    outs = []
    for c in chunks:
        stb = bf(st)
        outs.append(_dot_nt(bf(p_mat[c]), stb) + q_mat[c])
        st = st * w_tot[c * C:c * C + 1, :] + _dot(stb, bf(m_mat[c])) + n_mat[c]
    st_ref[...] = st

    o = jnp.concatenate(outs, axis=0)
    inv_d = 1.0 / HEAD_DIM
    mean = _split_dot_right(o, ones, 1) * inv_d
    cen = o - mean
    var = _split_dot_right(cen * cen, ones, 1) * inv_d
    o = cen * lax.rsqrt(var + GN_EPS) * lng_ref[...] + lnb_ref[...]
    y_ref[...] = ((o + bonus) * g).astype(BF16)


def _rwkv(p, mu, w0, w2, a0, a2, g2, kkp, ka, rk, lng, lnb, *, batch, seq, tb):
    T, cols = p.shape
    w = w0.shape[1]
    nb = seq // tb
    vec = _resident((1, w))
    low = _resident((cols - 3 * w, w))
    return pl.pallas_call(
        _rwkv_kernel,
        out_shape=jax.ShapeDtypeStruct((T, w), BF16),
        grid=(batch, nb),
        in_specs=[
            pl.BlockSpec((tb, cols), lambda b, j: (b * nb + j, 0)),
            _resident((1, cols)), vec, low, vec, low, low, vec, vec, vec, vec, vec,
        ],
        out_specs=pl.BlockSpec((tb, w), lambda b, j: (b * nb + j, 0)),
        scratch_shapes=[pltpu.VMEM((SUBLANES, cols), F32), pltpu.VMEM((w, w), F32)],
        compiler_params=pltpu.CompilerParams(
            dimension_semantics=("parallel", "arbitrary"), vmem_limit_bytes=VMEM_LIMIT),
        name="rwkv",
    )(p, mu, w0, w2, a0, a2, g2, kkp, ka, rk, lng, lnb)


def _out_proj_kernel(ya_ref, yb_ref, yc_ref, x_ref, wa_ref, wb_ref, wc_ref, fg_ref, n2_ref,
                     wr_ref, br_ref, x1_ref, h2_ref, comb_ref, *, sub):
    for r0 in range(0, x_ref.shape[0], sub):
        rows = slice(r0, r0 + sub)
        ybn = (_rms(yb_ref[rows, :].astype(F32)) * fg_ref[...]).astype(BF16)
        x1 = (x_ref[rows, :] + _dot(ya_ref[rows, :], wa_ref[...]) + _dot(ybn, wb_ref[...])
              + _dot(yc_ref[rows, :], wc_ref[...]))
        x1_ref[rows, :] = x1
        h2 = _rms(x1) * n2_ref[...]
        h_hi = h2.astype(BF16)
        h2_ref[rows, :] = h_hi

        h_lo = (h2 - h_hi.astype(F32)).astype(BF16)
        hw = _dot(h_hi, wr_ref[...])
        logits = (hw[:, 0:LANES] + hw[:, LANES:2 * LANES] + _dot(h_lo, wr_ref[:, 0:LANES])
                  + br_ref[...])
        comb_ref[rows, :] = _route(logits)


def _route(logits):
    lane = lax.broadcasted_iota(jnp.int32, logits.shape, 1)
    lane_f = lane.astype(F32)
    far = float(LANES)

    def first_argmax(vals, vmax):
        return jnp.min(jnp.where(vals == vmax, lane_f, far), axis=-1, keepdims=True)

    gl = jnp.where(lane < N_GROUPS, logits, NEG_BIG)
    gmax = jnp.max(gl, axis=-1, keepdims=True)
    gidx = first_argmax(gl, gmax)
    g_p = 1.0 / jnp.sum(jnp.exp(gl - gmax), axis=-1, keepdims=True)

    e_group = ((lane - N_GROUPS) >> 2).astype(F32)
    el = jnp.where((lane >= N_GROUPS) & (e_group == gidx), logits, NEG_BIG)
    emax = jnp.max(el, axis=-1, keepdims=True)
    esum = jnp.sum(jnp.exp(el - emax), axis=-1, keepdims=True)
    i1 = first_argmax(el, emax)
    el2 = jnp.where(lane_f == i1, NEG_BIG, el)
    emax2 = jnp.max(el2, axis=-1, keepdims=True)
    i2 = first_argmax(el2, emax2)
    p1 = 1.0 / esum
    p2 = jnp.exp(emax2 - emax) / esum
    den = p1 + p2
    return (jnp.where(lane_f == i1, g_p * (p1 / den), 0.0)
            + jnp.where(lane_f == i2, g_p * (p2 / den), 0.0)
            + jnp.where(lane_f == gidx, 1.0, 0.0))


def _out_proj(ya, yb, yc, x2, wa, wb, wc, fg, n2, wr, br, *, tm):
    T, D = x2.shape
    row = lambda i: (i, 0)
    return pl.pallas_call(
        functools.partial(_out_proj_kernel, sub=tm // 2),
        out_shape=(jax.ShapeDtypeStruct((T, D), F32), jax.ShapeDtypeStruct((T, D), BF16),
                   jax.ShapeDtypeStruct((T, LANES), F32)),
        grid=(T // tm,),
        in_specs=[
            pl.BlockSpec((tm, ya.shape[1]), row),
            pl.BlockSpec((tm, yb.shape[1]), row),
            pl.BlockSpec((tm, yc.shape[1]), row),
            pl.BlockSpec((tm, D), row),
            _resident(wa.shape), _resident(wb.shape), _resident(wc.shape),
            _resident(fg.shape), _resident(n2.shape), _resident(wr.shape), _resident(br.shape),
        ],
        out_specs=(pl.BlockSpec((tm, D), row), pl.BlockSpec((tm, D), row),
                   pl.BlockSpec((tm, LANES), row)),
        compiler_params=pltpu.CompilerParams(
            dimension_semantics=("parallel",), vmem_limit_bytes=VMEM_LIMIT),
        name="out_proj",
    )(ya, yb, yc, x2, wa, wb, wc, fg, n2, wr, br)


def _moe_kernel(h_ref, route_ref, x_ref, wg_ref, wu_ref, wd_ref, o_ref, hs_ref, rs_ref, ys_ref):
    tm = h_ref.shape[0]
    n_groups = wd_ref.shape[0]
    route = route_ref[...]
    lane = lax.broadcasted_iota(jnp.int32, (tm, LANES), 1)
    onehot = jnp.where(lane < n_groups, route, 0.0)
    rr = lax.broadcasted_iota(jnp.int32, (tm, tm), 0)
    cc = lax.broadcasted_iota(jnp.int32, (tm, tm), 1)
    tri = jnp.where(cc <= rr, 1.0, 0.0).astype(BF16)
    cum = _dot(tri, onehot.astype(BF16))
    tot = cum[tm - 1:tm, :]
    start = jnp.zeros_like(tot)
    for d in range(1, n_groups):
        start = start + jnp.where(lane[0:1] >= d, pltpu.roll(tot, d, axis=1), 0.0)
    end = start + tot
    pos = jnp.sum(jnp.where(onehot > 0.0, start + cum - 1.0, 0.0), axis=-1, keepdims=True)
    perm_t = jnp.where(pos == cc.astype(F32), 1.0, 0.0).astype(BF16)

    hs_ref[...] = _dot_tn(perm_t, h_ref[...]).astype(BF16)
    rem = route
    sorted_route = None
    for _ in range(2):
        hi = rem.astype(BF16)
        t = _dot_tn(perm_t, hi)
        sorted_route = t if sorted_route is None else sorted_route + t
        rem = rem - hi.astype(F32)
    rs_ref[...] = sorted_route
    ys_ref[...] = jnp.zeros_like(ys_ref)

    for g in range(n_groups):
        g_start = start[0, g]
        g_end = end[0, g]
        for r0 in range(0, tm, MOE_BLOCK):

            @pl.when((g_start < r0 + MOE_BLOCK) & (g_end > r0))
            def _(g=g, r0=r0):
                rows = slice(r0, r0 + MOE_BLOCK)
                hb = hs_ref[rows, :]
                rw = rs_ref[rows, :]
                hids = []
                for e in range(g * EXPERTS_PER_GROUP, (g + 1) * EXPERTS_PER_GROUP):
                    gate = _dot(hb, wg_ref[e])
                    hid = gate * jax.nn.sigmoid(gate) * _dot(hb, wu_ref[e])
                    hids.append((hid * rw[:, N_GROUPS + e:N_GROUPS + e + 1]).astype(BF16))
                ys_ref[rows, :] += _dot(jnp.concatenate(hids, axis=1), wd_ref[g])

    o_ref[...] = x_ref[...] + _dot(perm_t, ys_ref[...].astype(BF16))


def _moe(h2, route, x1, wg, wu, wd, *, tm):
    T, D = x1.shape
    row = lambda i: (i, 0)
    return pl.pallas_call(
        _moe_kernel,
        out_shape=jax.ShapeDtypeStruct((T, D), F32),
        grid=(T // tm,),
        in_specs=[
            pl.BlockSpec((tm, D), row),
            pl.BlockSpec((tm, LANES), row),
            pl.BlockSpec((tm, D), row),
            _resident(wg.shape),
            _resident(wu.shape),
            _resident(wd.shape),
        ],
        out_specs=pl.BlockSpec((tm, D), row),
        scratch_shapes=[pltpu.VMEM((tm, D), BF16), pltpu.VMEM((tm, LANES), F32),
                        pltpu.VMEM((tm, D), F32)],
        compiler_params=pltpu.CompilerParams(
            dimension_semantics=("parallel",), vmem_limit_bytes=VMEM_LIMIT),
        name="moe",
    )(h2, route, x1, wg, wu, wd)


def _block_diag(wh):
    n_heads, d, _ = wh.shape
    eye = jnp.eye(n_heads, dtype=wh.dtype)
    return jnp.einsum("hij,hg->higj", wh, eye).reshape(n_heads * d, n_heads * d)


def _row(vec):
    return vec.reshape(1, -1).astype(F32)


def _pad_rows(mat, start, total):
    return jnp.zeros((total, mat.shape[1]), mat.dtype).at[start:start + mat.shape[0]].set(mat)


def kernel(x, norm1_g, w_in, conv_w, conv_b, lru_wa, lru_ba, lru_wx, lru_bx, lru_lambda,
           lru_norm_g, fox_fb, fox_qnorm_g, fox_knorm_g, fox_norm_g, rwkv_mu, rwkv_w0,
           rwkv_w2, rwkv_a0, rwkv_a2, rwkv_g2, rwkv_kk, rwkv_ka, rwkv_rk, rwkv_ln_g,
           rwkv_ln_b, w_out, norm2_g, router_gw, router_gb, router_ew, router_eb,
           exp_w_gate, exp_w_up, exp_w_down):
    batch, seq, d_model = x.shape
    depth = w_in.shape[0]
    lru_w = conv_w.shape[2]
    fox_heads = fox_fb.shape[1]
    fox_w = fox_heads * HEAD_DIM
    rwkv_w = rwkv_w0.shape[1]
    rwkv_cols = rwkv_mu.shape[1]
    d_rank, a_rank, g_rank = rwkv_w2.shape[1], rwkv_a2.shape[1], rwkv_g2.shape[1]
    low = d_rank + a_rank + g_rank
    n_exp, _, d_exp = exp_w_gate.shape[1:]
    assert low == LANES and rwkv_cols == 3 * rwkv_w + low
    assert fox_heads <= LANES and n_exp == N_EXPERTS

    o_fox = 2 * lru_w
    o_fl = o_fox + 3 * fox_w
    o_rwkv = o_fl + fox_heads
    seg = (0, o_fox, o_fox + fox_w, o_fox + 2 * fox_w, o_fl, o_fl + rwkv_cols,
           o_fl + rwkv_cols + LANES)

    x2 = x.reshape(batch * seq, d_model)
    for l in range(depth):
        wl = w_in[l]
        n_fl = FOX_BIAS_PIECES * fox_heads
        w_perm = jnp.concatenate(
            [wl[:, 0:o_fl], wl[:, o_rwkv:o_rwkv + rwkv_cols],
             jnp.repeat(wl[:, o_fl:o_rwkv], FOX_BIAS_PIECES, axis=1),
             jnp.zeros((d_model, LANES - n_fl), wl.dtype)], axis=1).astype(BF16)
        qg = _row(jnp.tile(fox_qnorm_g[l], fox_heads) * (HEAD_DIM ** -0.5 * LOG2E))
        kg = _row(jnp.tile(fox_knorm_g[l], fox_heads))
        fb = _row(jnp.pad(jnp.repeat(fox_fb[l], FOX_BIAS_PIECES), (0, LANES - n_fl)))
        u, q, k_aug, v, p = _in_proj(x2, _row(norm1_g[l]), w_perm, qg, kg, fb,
                                    seq=seq, tm=512, seg=seg)

        wg = jnp.concatenate([_block_diag(lru_wa[l]), _block_diag(lru_wx[l])], axis=1).astype(BF16)
        bg = _row(jnp.concatenate([lru_ba[l], lru_bx[l]]))
        ya = _lru(u, conv_w[l], _row(conv_b[l]), wg, bg, _row(lru_lambda[l]),
                  _row(lru_norm_g[l]), batch=batch, seq=seq, tb=512)

        yb = _fox(q, k_aug, v, batch=batch, seq=seq, tq=512)

        yc = _rwkv(p, _row(rwkv_mu[l]), _row(rwkv_w0[l]),
                   _pad_rows(rwkv_w2[l], 0, low).astype(BF16), _row(rwkv_a0[l]),
                   _pad_rows(rwkv_a2[l], d_rank, low).astype(BF16),
                   _pad_rows(rwkv_g2[l], d_rank + a_rank, low).astype(BF16),
                   _row(rwkv_kk[l]), _row(rwkv_ka[l]), _row(rwkv_rk[l]),
                   _row(rwkv_ln_g[l]), _row(rwkv_ln_b[l]), batch=batch, seq=seq, tb=512)

        wo = w_out[l].astype(BF16)
        wr = jnp.concatenate(
            [router_gw[l], router_ew[l],
             jnp.zeros((d_model, LANES - N_GROUPS - n_exp), F32)], axis=1)
        wr_hi = wr.astype(BF16)
        wr = jnp.concatenate([wr_hi, (wr - wr_hi.astype(F32)).astype(BF16)], axis=1)
        br = _row(jnp.pad(jnp.concatenate([router_gb[l], router_eb[l]]),
                          (0, LANES - N_GROUPS - n_exp)))
        x1, h2, comb = _out_proj(
            ya, yb, yc, x2, wo[0:lru_w], wo[lru_w:lru_w + fox_w], wo[lru_w + fox_w:],
            _row(fox_norm_g[l]), _row(norm2_g[l]), wr, br, tm=512)

        wd = exp_w_down[l].astype(BF16).reshape(N_GROUPS, EXPERTS_PER_GROUP * d_exp, d_model)
        x2 = _moe(h2, comb, x1, exp_w_gate[l].astype(BF16), exp_w_up[l].astype(BF16), wd, tm=512)
    return x2.reshape(batch, seq, d_model)
```

```python
import functools

import jax
import jax.numpy as jnp
from jax import lax
from jax.experimental import pallas as pl
from jax.experimental.pallas import tpu as pltpu

F32 = jnp.float32
BF16 = jnp.bfloat16

NORM_EPS = 1e-6
GN_EPS = 64e-5
LRU_C = 8.0
HEAD_DIM = 64
HEAD_SHIFT = 6
LANES = 128
SUBLANES = 8
CHUNK = 64
CONV_WIDTH = 4
N_GROUPS = 4
EXPERTS_PER_GROUP = 4
N_EXPERTS = N_GROUPS * EXPERTS_PER_GROUP
MOE_BLOCK = 128
NEG_BIG = -1e30
LOG2E = 1.4426950408889634
FOX_BIAS_PIECES = 3
FOX_DEN_ROWS = 16
VMEM_LIMIT = 56 * 1024 * 1024


def _dot(a, b):
    return jnp.dot(a, b, preferred_element_type=F32)


def _dot_nt(a, b):
    return lax.dot_general(a, b, (((1,), (1,)), ((), ())), preferred_element_type=F32)


def _dot_tn(a, b):
    return lax.dot_general(a, b, (((0,), (0,)), ((), ())), preferred_element_type=F32)


def _split_dot_right(x, ones, parts):
    acc = None
    rem = x
    for _ in range(parts):
        hi = rem.astype(BF16)
        t = _dot(hi, ones)
        acc = t if acc is None else acc + t
        rem = rem - hi.astype(F32)
    return acc


def _split_dot_left(ones, x, parts):
    acc = None
    rem = x
    for _ in range(parts):
        hi = rem.astype(BF16)
        t = _dot(ones, hi)
        acc = t if acc is None else acc + t
        rem = rem - hi.astype(F32)
    return acc


def _head_ones(n):
    r = lax.broadcasted_iota(jnp.int32, (n, n), 0) >> HEAD_SHIFT
    c = lax.broadcasted_iota(jnp.int32, (n, n), 1) >> HEAD_SHIFT
    return jnp.where(r == c, 1.0, 0.0).astype(BF16)


def _softplus(z):
    return jnp.maximum(z, 0.0) + jnp.log(1.0 + jnp.exp(-jnp.abs(z)))


def _rms(x, eps=NORM_EPS):
    return x * lax.rsqrt(jnp.mean(x * x, axis=-1, keepdims=True) + eps)


def _resident(shape):
    zeros = (0,) * len(shape)
    return pl.BlockSpec(shape, lambda *_: zeros, pipeline_mode=pl.Buffered(1))


def _in_proj_kernel(x_ref, g_ref, w_ref, qg_ref, kg_ref, fb_ref,
                    cw_ref, cb_ref, wg_ref, bg_ref, lam_ref, ng_ref,
                    ya_ref, q_ref, k_ref, v_ref, p_ref,
                    carry_ref, tail_ref, hst_ref, buf_ref,
                    *, tiles_per_seq, seg, sub):
    i = pl.program_id(0)

    @pl.when(i % tiles_per_seq == 0)
    def _():
        carry_ref[...] = jnp.zeros_like(carry_ref)
        tail_ref[...] = jnp.zeros_like(tail_ref)
        hst_ref[...] = jnp.zeros_like(hst_ref)

    fox = seg[2] - seg[1]
    n_heads = fox // HEAD_DIM
    ones = _head_ones(fox)
    inv_d = 1.0 / HEAD_DIM
    rr = lax.broadcasted_iota(jnp.int32, (sub, sub), 0)
    cc = lax.broadcasted_iota(jnp.int32, (sub, sub), 1)
    tri = jnp.where(cc <= rr, 1.0, 0.0).astype(BF16)
    lane = lax.broadcasted_iota(jnp.int32, (sub, LANES), 1)
    feat = lane < HEAD_DIM
    bias_lanes = (lane >= HEAD_DIM) & (lane < HEAD_DIM + FOX_BIAS_PIECES)
    q_bias = jnp.where(bias_lanes, 1.0, 0.0)
    piece_id = lax.rem(lane, FOX_BIAS_PIECES)
    carry = carry_ref[0:1, :]
    tail = tail_ref[...]
    h_lru = hst_ref[0:1, :]
    lru_w = (seg[1] - seg[0]) // 2

    def head_norm(t):
        return t * lax.rsqrt(_dot((t * t).astype(BF16), ones) * inv_d + NORM_EPS)

    sub_rows = [slice(r0, r0 + sub) for r0 in range(0, x_ref.shape[0], sub)]
    state = {"carry": carry, "tail": tail, "h": h_lru}

    def project_steps(rows, out):
        hb = []

        def first():
            hb.append((_rms(x_ref[rows, :]) * g_ref[...]).astype(BF16))
            out[0] = _dot(hb[0], w_ref[:, seg[0]:seg[1]])

        def segment(n):
            out[n] = _dot(hb[0], w_ref[:, seg[n]:seg[n + 1]])

        def rwkv_cols():
            p_ref[rows, :] = _dot(hb[0], w_ref[:, seg[4]:seg[5]])

        return [first, functools.partial(segment, 5), functools.partial(segment, 1),
                functools.partial(segment, 2), functools.partial(segment, 3), rwkv_cols]

    def finish_steps(idx, rows, out):
        local = {}

        def lru():
            u = out[0]
            ya_ref[rows, :], state["tail"], state["h"] = _lru_block(
                u[:, 0:lru_w], u[:, lru_w:2 * lru_w], state["tail"], state["h"], buf_ref.at[idx],
                cw_ref, cb_ref, wg_ref, bg_ref, lam_ref, ng_ref)

        def forget():
            lf = -_softplus(-(out[5] + fb_ref[...]))
            c = _split_dot_left(tri, lf, 3) + state["carry"]
            state["carry"] = c[sub - 1:sub, :]
            rem = c * (-LOG2E)
            pieces = None
            for piece in range(FOX_BIAS_PIECES):
                hi = rem.astype(BF16).astype(F32)
                pieces = hi if pieces is None else jnp.where(piece_id == piece, hi, pieces)
                rem = rem - hi
            local["pieces"] = pieces

        def spread(t, h):
            th = t[:, (h // 2) * LANES:(h // 2 + 1) * LANES]
            return pltpu.roll(th, HEAD_DIM, axis=1) if h % 2 else th

        def queries():
            q = head_norm(out[1]) * qg_ref[...]
            for h in range(n_heads):
                q_ref[h * LANES:(h + 1) * LANES, rows] = (
                    jnp.where(feat, spread(q, h), q_bias).T.astype(BF16))

        def keys():
            k = head_norm(out[2]) * kg_ref[...]
            for h in range(n_heads):
                k_bias = pltpu.roll(local["pieces"], HEAD_DIM - FOX_BIAS_PIECES * h, axis=1)
                k_ref[rows, h * LANES:(h + 1) * LANES] = jnp.where(
                    feat, spread(k, h), jnp.where(bias_lanes, k_bias, 0.0)).astype(BF16)

        def values():
            v_ref[:, rows] = out[3].T.astype(BF16)

        return [lru, forget, queries, keys, values]

    outs = [dict() for _ in sub_rows]
    for step in project_steps(sub_rows[0], outs[0]):
        step()
    for idx, rows in enumerate(sub_rows):
        ahead = project_steps(sub_rows[idx + 1], outs[idx + 1]) if idx + 1 < len(sub_rows) else []
        finish = finish_steps(idx, rows, outs[idx])
        for n in range(max(len(ahead), len(finish))):
            if n < len(ahead):
                ahead[n]()
            if n < len(finish):
                finish[n]()

    carry_ref[...] = jnp.broadcast_to(state["carry"], carry_ref.shape)
    tail_ref[...] = state["tail"]
    hst_ref[...] = jnp.broadcast_to(state["h"], hst_ref.shape)


def _in_proj(x2, g, w, qg, kg, fb, lru_params, *, seq, tm, seg):
    T, D = x2.shape
    n_out = w.shape[1]
    widths = [seg[j + 1] - seg[j] for j in range(6)]
    row = lambda i: (i, 0)
    col = lambda i: (0, i)
    outs = [
        ((T, widths[0] // 2), (tm, widths[0] // 2), row, BF16),
        ((2 * widths[1], T), (2 * widths[1], tm), col, BF16),
        ((T, 2 * widths[2]), (tm, 2 * widths[2]), row, BF16),
        ((widths[3], T), (widths[3], tm), col, BF16),
        ((T, widths[4]), (tm, widths[4]), row, F32),
    ]
    return pl.pallas_call(
        functools.partial(_in_proj_kernel, tiles_per_seq=seq // tm, seg=seg, sub=tm // 2),
        out_shape=tuple(jax.ShapeDtypeStruct(shape, dt) for shape, _, _, dt in outs),
        grid=(T // tm,),
        in_specs=[
            pl.BlockSpec((tm, D), row),
            _resident((1, D)),
            _resident((D, n_out)),
            _resident((1, widths[1])),
            _resident((1, widths[2])),
            _resident((1, widths[5])),
        ] + [_resident(t.shape) for t in lru_params],
        out_specs=tuple(pl.BlockSpec(blk, imap) for _, blk, imap, _ in outs),
        scratch_shapes=[pltpu.VMEM((SUBLANES, widths[5]), F32),
                        pltpu.VMEM((SUBLANES, widths[0] // 2), F32),
                        pltpu.VMEM((SUBLANES, widths[0] // 2), F32),
                        pltpu.VMEM((2, tm // 2 + SUBLANES, widths[0] // 2), F32)],
        compiler_params=pltpu.CompilerParams(
            dimension_semantics=("arbitrary",), vmem_limit_bytes=VMEM_LIMIT),
        name="in_proj",
    )(x2, g, w, qg, kg, fb, *lru_params)


def _shift_rows(x, d, fill, row):
    return jnp.where(row >= d, pltpu.roll(x, d, axis=0), fill)


def _lru_block(xa, ga, tail, h_prev, buf_ref, cw_ref, cb_ref, wg_ref, bg_ref, lam_ref, ng_ref):
    tb, w = xa.shape
    pad = SUBLANES
    buf_ref[0:pad, :] = tail
    buf_ref[pad:pad + tb, :] = xa
    xc = cb_ref[...] + cw_ref[CONV_WIDTH - 1:CONV_WIDTH, :] * xa
    for d in range(1, CONV_WIDTH):
        xc = xc + cw_ref[CONV_WIDTH - 1 - d:CONV_WIDTH - d, :] * buf_ref[pad - d:pad - d + tb, :]

    gates = _dot(xc.astype(BF16), wg_ref[...]) + bg_ref[...]
    r = jax.nn.sigmoid(gates[:, 0:w])
    i = jax.nn.sigmoid(gates[:, w:2 * w])
    log_a = (-LRU_C) * r * _softplus(-lam_ref[...])
    a = jnp.exp(log_a)
    b = jnp.sqrt(1.0 - a * a) * (i * xc)

    row = lax.broadcasted_iota(jnp.int32, (tb, w), 0)
    d = 1
    while d < tb:
        a_sh = _shift_rows(a, d, 1.0, row)
        b_sh = _shift_rows(b, d, 0.0, row)
        b = a * b_sh + b
        a = a * a_sh
        d *= 2
    h = b + a * h_prev
    y = jax.nn.gelu(ga) * h
    return (_rms(y) * ng_ref[...]).astype(BF16), xa[tb - pad:tb, :], h[tb - 1:tb, :]


def _fox_kernel(q_ref, k_ref, v_ref, o_ref, s_buf, mx_buf, p_buf, m_ref, acc_ref, fin_ref, *, tq):
    n_heads = k_ref.shape[1] // LANES
    nq = k_ref.shape[0] // tq
    causal = (lax.broadcasted_iota(jnp.int32, (tq, tq), 0)
              <= lax.broadcasted_iota(jnp.int32, (tq, tq), 1))
    items = [(i, j) for i in range(nq) for j in range(i + 1)]

    def scores(t, u):
        i, j = items[t]
        hl = slice(u * LANES, (u + 1) * LANES)
        s = _dot(k_ref[j * tq:(j + 1) * tq, hl], q_ref[hl, i * tq:(i + 1) * tq])
        if i == j:
            s = jnp.where(causal, s, NEG_BIG)
        s_buf[t % 2, u] = s
        mx_buf[t % 2, u] = jnp.max(s, axis=0, keepdims=True)

    def consume(t, u):
        i, j = items[t]
        slot = t % 2
        vt = jnp.concatenate([v_ref[u * HEAD_DIM:(u + 1) * HEAD_DIM, j * tq:(j + 1) * tq],
                              jnp.ones((FOX_DEN_ROWS, tq), BF16)], axis=0)
        m_new = mx_buf[slot, u] if j == 0 else jnp.maximum(m_ref[u], mx_buf[slot, u])
        p_buf[u] = jnp.exp2(s_buf[slot, u] - m_new).astype(BF16)
        acc = _dot(vt, p_buf[u])
        if j > 0:
            acc = jnp.exp2(m_ref[u] - m_new) * acc_ref[u] + acc
        if j < i:
            m_ref[u] = m_new
            acc_ref[u] = acc
        else:
            fin_ref[u * HEAD_DIM:(u + 1) * HEAD_DIM, :] = acc[0:HEAD_DIM] / acc[HEAD_DIM:HEAD_DIM + 1]
            if u == n_heads - 1:
                o_ref[i * tq:(i + 1) * tq, :] = fin_ref[...].T.astype(BF16)

    for u in range(n_heads):
        scores(0, u)
    for t in range(len(items)):
        for u in range(n_heads):
            if t + 1 < len(items):
                scores(t + 1, u)
            consume(t, u)


def _fox(q_aug, k_aug, v_t, *, batch, seq, tq):
    T, wide = k_aug.shape
    heads_per_step = 2
    blk = heads_per_step * LANES
    out_w = heads_per_step * HEAD_DIM
    steps = wide // blk
    return pl.pallas_call(
        functools.partial(_fox_kernel, tq=tq),
        out_shape=jax.ShapeDtypeStruct((T, steps * out_w), BF16),
        grid=(batch, steps),
        in_specs=[
            pl.BlockSpec((blk, seq), lambda b, h: (h, b)),
            pl.BlockSpec((seq, blk), lambda b, h: (b, h)),
            pl.BlockSpec((out_w, seq), lambda b, h: (h, b)),
        ],
        out_specs=pl.BlockSpec((seq, out_w), lambda b, h: (b, h)),
        scratch_shapes=[
            pltpu.VMEM((2, heads_per_step, tq, tq), F32),
            pltpu.VMEM((2, heads_per_step, 1, tq), F32),
            pltpu.VMEM((heads_per_step, tq, tq), BF16),
            pltpu.VMEM((heads_per_step, 1, tq), F32),
            pltpu.VMEM((heads_per_step, HEAD_DIM + FOX_DEN_ROWS, tq), F32),
            pltpu.VMEM((out_w, tq), F32),
        ],
        compiler_params=pltpu.CompilerParams(
            dimension_semantics=("parallel", "parallel"), vmem_limit_bytes=VMEM_LIMIT),
        name="fox",
    )(q_aug, k_aug, v_t)


def _stack_heads(x, lane_head, n_heads):
    zero = jnp.zeros_like(x)
    return jnp.concatenate([jnp.where(lane_head == h, x, zero) for h in range(n_heads)], axis=0)


def _rwkv_kernel(p_ref, mu_ref, w0_ref, w2_ref, a0_ref, a2_ref, g2_ref, kk_ref, ka_ref,
                 rk_ref, lng_ref, lnb_ref, y_ref, prev_ref, st_ref):
    j = pl.program_id(1)
    tb, cols = p_ref.shape
    w = y_ref.shape[1]
    n_heads = w // HEAD_DIM
    C = CHUNK
    chunk_shift = CHUNK.bit_length() - 1

    @pl.when(j == 0)
    def _():
        prev_ref[...] = jnp.zeros_like(prev_ref)
        st_ref[...] = jnp.zeros_like(st_ref)

    p = p_ref[...]
    row = lax.broadcasted_iota(jnp.int32, (tb, cols), 0)
    prev = jnp.where(row == 0, prev_ref[0:1, :], pltpu.roll(p, 1, axis=0))
    prev_ref[...] = jnp.broadcast_to(p[tb - 1:tb, :], prev_ref.shape)
    ps = p + (prev - p) * mu_ref[...]
    r = ps[:, 0:w]
    k = ps[:, w:2 * w]
    v = ps[:, 2 * w:3 * w]
    lr = ps[:, 3 * w:cols]

    wpre = w0_ref[...] + _dot(jnp.tanh(lr).astype(BF16), w2_ref[...])
    lw = -jnp.exp(-_softplus(-wpre) - 0.5)
    a = jax.nn.sigmoid(a0_ref[...] + _dot(lr.astype(BF16), a2_ref[...]))
    g = _dot(jax.nn.sigmoid(lr).astype(BF16), g2_ref[...])

    ones = _head_ones(w)
    kk = k * kk_ref[...]
    kk = kk / jnp.maximum(jnp.sqrt(_split_dot_right(kk * kk, ones, 1)), 1e-12)
    k2 = k * (1.0 + (a - 1.0) * ka_ref[...])
    b = kk * a
    bonus = _split_dot_right(r * k2 * rk_ref[...], ones, 1) * v

    rr = lax.broadcasted_iota(jnp.int32, (tb, tb), 0)
    cc = lax.broadcasted_iota(jnp.int32, (tb, tb), 1)
    same = (rr >> chunk_shift) == (cc >> chunk_shift)
    tri = jnp.where(same & (cc <= rr), 1.0, 0.0).astype(BF16)
    tot = jnp.where(same, 1.0, 0.0).astype(BF16)
    LW = _split_dot_left(tri, lw, 3)
    LT = _split_dot_left(tot, lw, 3)
    e_out = jnp.exp(-LW)
    e_end = jnp.exp(LT - LW)
    at = (-kk) * jnp.exp(LW - lw)
    rt = r * jnp.exp(LW)
    bt = b * e_out
    kt = k2 * e_out
    b_end = b * e_end
    k_end = k2 * e_end
    w_tot = jnp.exp(LT)

    lane_head = lax.broadcasted_iota(jnp.int32, (C, w), 1) >> HEAD_SHIFT
    t_idx = lax.broadcasted_iota(jnp.int32, (C, w), 0)
    j_idx = lax.broadcasted_iota(jnp.int32, (C, w), 1) & (HEAD_DIM - 1)
    strict = j_idx < t_idx
    incl = j_idx <= t_idx
    eye = jnp.where(j_idx == t_idx, 1.0, 0.0)
    sq_r = lax.broadcasted_iota(jnp.int32, (w, w), 0) >> HEAD_SHIFT
    sq_c = lax.broadcasted_iota(jnp.int32, (w, w), 1) >> HEAD_SHIFT
    block_diag = sq_r == sq_c
    stack = functools.partial(_stack_heads, lane_head=lane_head, n_heads=n_heads)

    chunks = range(tb // C)
    rows_of = [slice(c * C, (c + 1) * C) for c in chunks]
    bf = lambda t: t.astype(BF16)
    cat0 = lambda *ts: jnp.concatenate(ts, axis=0)
    cat1 = lambda *ts: jnp.concatenate(ts, axis=1)

    A = [_dot_nt(bf(cat0(at[sl], rt[sl])), cat0(stack(bf(bt[sl])), stack(bf(kt[sl]))))
         for sl in rows_of]
    a_ab = [jnp.where(strict, a[0:C, 0:w], 0.0) for a in A]
    a_ak = [jnp.where(strict, a[0:C, w:2 * w], 0.0) for a in A]
    a_rb = [jnp.where(incl, a[C:2 * C, 0:w], 0.0) for a in A]
    a_rk = [jnp.where(incl, a[C:2 * C, w:2 * w], 0.0) for a in A]

    pw = [_dot(bf(l), stack(bf(l))) for l in a_ab]
    tinv = [eye + l for l in a_ab]
    sz = 2
    while 2 * sz < C:
        both = [_dot(bf(cat0(t, p)), stack(bf(p))) for t, p in zip(tinv, pw)]
        tinv = [t + r[0:C] for t, r in zip(tinv, both)]
        pw = [r[C:2 * C] for r in both]
        sz *= 2
    tinv = [t + _dot(bf(t), stack(bf(p))) for t, p in zip(tinv, pw)]

    yo = [_dot(bf(cat0(ak, ark)), stack(bf(v[sl]))) for ak, ark, sl in zip(a_ak, a_rk, rows_of)]
    gu = [_dot(bf(t), cat1(stack(bf(at[sl])), stack(bf(y[0:C]))))
          for t, y, sl in zip(tinv, yo, rows_of)]
    pq = [_dot(bf(rb), cat1(stack(bf(x[:, 0:w])), stack(bf(x[:, w:2 * w]))))
          for rb, x in zip(a_rb, gu)]
    p_mat = [rt[sl] + x[:, 0:w] for sl, x in zip(rows_of, pq)]
    q_mat = [y[C:2 * C] + x[:, w:2 * w] for y, x in zip(yo, pq)]
    m_mat = [jnp.where(block_diag, _dot_tn(bf(x[:, 0:w]), bf(b_end[sl])), 0.0)
             for x, sl in zip(gu, rows_of)]
    n_mat = [jnp.where(block_diag,
                       _dot_tn(bf(cat0(x[:, w:2 * w], v[sl])), bf(cat0(b_end[sl], k_end[sl]))), 0.0)
             for x, sl in zip(gu, rows_of)]

    st = st_ref[...]
    outs = []
    for c in chunks:
        stb = bf(st)
        outs.append(_dot_nt(bf(p_mat[c]), stb) + q_mat[c])
        st = st * w_tot[c * C:c * C + 1, :] + _dot(stb, bf(m_mat[c])) + n_mat[c]
    st_ref[...] = st

    o = jnp.concatenate(outs, axis=0)
    inv_d = 1.0 / HEAD_DIM
    mean = _split_dot_right(o, ones, 1) * inv_d
    cen = o - mean
    var = _split_dot_right(cen * cen, ones, 1) * inv_d
    o = cen * lax.rsqrt(var + GN_EPS) * lng_ref[...] + lnb_ref[...]
    y_ref[...] = ((o + bonus) * g).astype(BF16)


def _rwkv(p, mu, w0, w2, a0, a2, g2, kkp, ka, rk, lng, lnb, *, batch, seq, tb):
    T, cols = p.shape
    w = w0.shape[1]
    nb = seq // tb
    vec = _resident((1, w))
    low = _resident((cols - 3 * w, w))
    return pl.pallas_call(
        _rwkv_kernel,
        out_shape=jax.ShapeDtypeStruct((T, w), BF16),
        grid=(batch, nb),
        in_specs=[
            pl.BlockSpec((tb, cols), lambda b, j: (b * nb + j, 0)),
            _resident((1, cols)), vec, low, vec, low, low, vec, vec, vec, vec, vec,
        ],
        out_specs=pl.BlockSpec((tb, w), lambda b, j: (b * nb + j, 0)),
        scratch_shapes=[pltpu.VMEM((SUBLANES, cols), F32), pltpu.VMEM((w, w), F32)],
        compiler_params=pltpu.CompilerParams(
            dimension_semantics=("parallel", "arbitrary"), vmem_limit_bytes=VMEM_LIMIT),
        name="rwkv",
    )(p, mu, w0, w2, a0, a2, g2, kkp, ka, rk, lng, lnb)


def _out_proj_kernel(ya_ref, yb_ref, yc_ref, x_ref, wa_ref, wb_ref, wc_ref, fg_ref, n2_ref,
                     wr_ref, br_ref, x1_ref, h2_ref, comb_ref, *, sub):
    sub_rows = [slice(r0, r0 + sub) for r0 in range(0, x_ref.shape[0], sub)]
    mixed = []
    for rows in sub_rows:
        ybn = (_rms(yb_ref[rows, :].astype(F32)) * fg_ref[...]).astype(BF16)
        mixed.append(_dot(ya_ref[rows, :], wa_ref[...]) + _dot(ybn, wb_ref[...])
                     + _dot(yc_ref[rows, :], wc_ref[...]))
    for rows, y in zip(sub_rows, mixed):
        x1 = x_ref[rows, :] + y
        x1_ref[rows, :] = x1
        h2 = _rms(x1) * n2_ref[...]
        h_hi = h2.astype(BF16)
        h2_ref[rows, :] = h_hi

        h_lo = (h2 - h_hi.astype(F32)).astype(BF16)
        hw = _dot(h_hi, wr_ref[...])
        logits = (hw[:, 0:LANES] + hw[:, LANES:2 * LANES] + _dot(h_lo, wr_ref[:, 0:LANES])
                  + br_ref[...])
        comb_ref[rows, :] = _route(logits)


def _route(logits):
    lane = lax.broadcasted_iota(jnp.int32, logits.shape, 1)
    lane_f = lane.astype(F32)
    far = float(LANES)

    def first_argmax(vals, vmax):
        return jnp.min(jnp.where(vals == vmax, lane_f, far), axis=-1, keepdims=True)

    gl = jnp.where(lane < N_GROUPS, logits, NEG_BIG)
    gmax = jnp.max(gl, axis=-1, keepdims=True)
    gidx = first_argmax(gl, gmax)
    g_p = 1.0 / jnp.sum(jnp.exp(gl - gmax), axis=-1, keepdims=True)

    e_group = ((lane - N_GROUPS) >> 2).astype(F32)
    el = jnp.where((lane >= N_GROUPS) & (e_group == gidx), logits, NEG_BIG)
    emax = jnp.max(el, axis=-1, keepdims=True)
    esum = jnp.sum(jnp.exp(el - emax), axis=-1, keepdims=True)
    i1 = first_argmax(el, emax)
    el2 = jnp.where(lane_f == i1, NEG_BIG, el)
    emax2 = jnp.max(el2, axis=-1, keepdims=True)
    i2 = first_argmax(el2, emax2)
    p1 = 1.0 / esum
    p2 = jnp.exp(emax2 - emax) / esum
    den = p1 + p2
    return (jnp.where(lane_f == i1, g_p * (p1 / den), 0.0)
            + jnp.where(lane_f == i2, g_p * (p2 / den), 0.0)
            + jnp.where(lane_f == gidx, 1.0, 0.0))


def _out_proj(ya, yb, yc, x2, wa, wb, wc, fg, n2, wr, br, *, tm):
    T, D = x2.shape
    row = lambda i: (i, 0)
    return pl.pallas_call(
        functools.partial(_out_proj_kernel, sub=tm // 2),
        out_shape=(jax.ShapeDtypeStruct((T, D), F32), jax.ShapeDtypeStruct((T, D), BF16),
                   jax.ShapeDtypeStruct((T, LANES), F32)),
        grid=(T // tm,),
        in_specs=[
            pl.BlockSpec((tm, ya.shape[1]), row),
            pl.BlockSpec((tm, yb.shape[1]), row),
            pl.BlockSpec((tm, yc.shape[1]), row),
            pl.BlockSpec((tm, D), row),
            _resident(wa.shape), _resident(wb.shape), _resident(wc.shape),
            _resident(fg.shape), _resident(n2.shape), _resident(wr.shape), _resident(br.shape),
        ],
        out_specs=(pl.BlockSpec((tm, D), row), pl.BlockSpec((tm, D), row),
                   pl.BlockSpec((tm, LANES), row)),
        compiler_params=pltpu.CompilerParams(
            dimension_semantics=("parallel",), vmem_limit_bytes=VMEM_LIMIT),
        name="out_proj",
    )(ya, yb, yc, x2, wa, wb, wc, fg, n2, wr, br)


def _moe_kernel(h_ref, route_ref, x_ref, wg_ref, wu_ref, wd_ref, o_ref, hs_ref, rs_ref, ys_ref):
    tm = h_ref.shape[0]
    n_groups = wd_ref.shape[0]
    route = route_ref[...]
    lane = lax.broadcasted_iota(jnp.int32, (tm, LANES), 1)
    onehot = jnp.where(lane < n_groups, route, 0.0)
    rr = lax.broadcasted_iota(jnp.int32, (tm, tm), 0)
    cc = lax.broadcasted_iota(jnp.int32, (tm, tm), 1)
    tri = jnp.where(cc <= rr, 1.0, 0.0).astype(BF16)
    cum = _dot(tri, onehot.astype(BF16))
    tot = cum[tm - 1:tm, :]
    start = jnp.zeros_like(tot)
    for d in range(1, n_groups):
        start = start + jnp.where(lane[0:1] >= d, pltpu.roll(tot, d, axis=1), 0.0)
    end = start + tot
    pos = jnp.sum(jnp.where(onehot > 0.0, start + cum - 1.0, 0.0), axis=-1, keepdims=True)
    perm_t = jnp.where(pos == cc.astype(F32), 1.0, 0.0).astype(BF16)

    hs_ref[...] = _dot_tn(perm_t, h_ref[...]).astype(BF16)
    rem = route
    sorted_route = None
    for _ in range(2):
        hi = rem.astype(BF16)
        t = _dot_tn(perm_t, hi)
        sorted_route = t if sorted_route is None else sorted_route + t
        rem = rem - hi.astype(F32)
    rs_ref[...] = sorted_route
    ys_ref[...] = jnp.zeros_like(ys_ref)

    for g in range(n_groups):
        g_start = start[0, g]
        g_end = end[0, g]
        for r0 in range(0, tm, MOE_BLOCK):

            @pl.when((g_start < r0 + MOE_BLOCK) & (g_end > r0))
            def _(g=g, r0=r0):
                rows = slice(r0, r0 + MOE_BLOCK)
                hb = hs_ref[rows, :]
                rw = rs_ref[rows, :]
                hids = []
                for e in range(g * EXPERTS_PER_GROUP, (g + 1) * EXPERTS_PER_GROUP):
                    gate = _dot(hb, wg_ref[e])
                    hid = gate * jax.nn.sigmoid(gate) * _dot(hb, wu_ref[e])
                    hids.append((hid * rw[:, N_GROUPS + e:N_GROUPS + e + 1]).astype(BF16))
                ys_ref[rows, :] += _dot(jnp.concatenate(hids, axis=1), wd_ref[g])

    o_ref[...] = x_ref[...] + _dot(perm_t, ys_ref[...].astype(BF16))


def _moe(h2, route, x1, wg, wu, wd, *, tm):
    T, D = x1.shape
    row = lambda i: (i, 0)
    return pl.pallas_call(
        _moe_kernel,
        out_shape=jax.ShapeDtypeStruct((T, D), F32),
        grid=(T // tm,),
        in_specs=[
            pl.BlockSpec((tm, D), row),
            pl.BlockSpec((tm, LANES), row),
            pl.BlockSpec((tm, D), row),
            _resident(wg.shape),
            _resident(wu.shape),
            _resident(wd.shape),
        ],
        out_specs=pl.BlockSpec((tm, D), row),
        scratch_shapes=[pltpu.VMEM((tm, D), BF16), pltpu.VMEM((tm, LANES), F32),
                        pltpu.VMEM((tm, D), F32)],
        compiler_params=pltpu.CompilerParams(
            dimension_semantics=("parallel",), vmem_limit_bytes=VMEM_LIMIT),
        name="moe",
    )(h2, route, x1, wg, wu, wd)


def _block_diag(wh):
    n_heads, d, _ = wh.shape
    eye = jnp.eye(n_heads, dtype=wh.dtype)
    return jnp.einsum("hij,hg->higj", wh, eye).reshape(n_heads * d, n_heads * d)


def _row(vec):
    return vec.reshape(1, -1).astype(F32)


def _pad_rows(mat, start, total):
    return jnp.zeros((total, mat.shape[1]), mat.dtype).at[start:start + mat.shape[0]].set(mat)


def kernel(x, norm1_g, w_in, conv_w, conv_b, lru_wa, lru_ba, lru_wx, lru_bx, lru_lambda,
           lru_norm_g, fox_fb, fox_qnorm_g, fox_knorm_g, fox_norm_g, rwkv_mu, rwkv_w0,
           rwkv_w2, rwkv_a0, rwkv_a2, rwkv_g2, rwkv_kk, rwkv_ka, rwkv_rk, rwkv_ln_g,
           rwkv_ln_b, w_out, norm2_g, router_gw, router_gb, router_ew, router_eb,
           exp_w_gate, exp_w_up, exp_w_down):
    batch, seq, d_model = x.shape
    depth = w_in.shape[0]
    lru_w = conv_w.shape[2]
    fox_heads = fox_fb.shape[1]
    fox_w = fox_heads * HEAD_DIM
    rwkv_w = rwkv_w0.shape[1]
    rwkv_cols = rwkv_mu.shape[1]
    d_rank, a_rank, g_rank = rwkv_w2.shape[1], rwkv_a2.shape[1], rwkv_g2.shape[1]
    low = d_rank + a_rank + g_rank
    n_exp, _, d_exp = exp_w_gate.shape[1:]
    assert low == LANES and rwkv_cols == 3 * rwkv_w + low
    assert fox_heads <= LANES and n_exp == N_EXPERTS

    o_fox = 2 * lru_w
    o_fl = o_fox + 3 * fox_w
    o_rwkv = o_fl + fox_heads
    seg = (0, o_fox, o_fox + fox_w, o_fox + 2 * fox_w, o_fl, o_fl + rwkv_cols,
           o_fl + rwkv_cols + LANES)

    x2 = x.reshape(batch * seq, d_model)
    for l in range(depth):
        wl = w_in[l]
        n_fl = FOX_BIAS_PIECES * fox_heads
        w_perm = jnp.concatenate(
            [wl[:, 0:o_fl], wl[:, o_rwkv:o_rwkv + rwkv_cols],
             jnp.repeat(wl[:, o_fl:o_rwkv], FOX_BIAS_PIECES, axis=1),
             jnp.zeros((d_model, LANES - n_fl), wl.dtype)], axis=1).astype(BF16)
        qg = _row(jnp.tile(fox_qnorm_g[l], fox_heads) * (HEAD_DIM ** -0.5 * LOG2E))
        kg = _row(jnp.tile(fox_knorm_g[l], fox_heads))
        fb = _row(jnp.pad(jnp.repeat(fox_fb[l], FOX_BIAS_PIECES), (0, LANES - n_fl)))
        wg = jnp.concatenate([_block_diag(lru_wa[l]), _block_diag(lru_wx[l])], axis=1).astype(BF16)
        bg = _row(jnp.concatenate([lru_ba[l], lru_bx[l]]))
        lru_params = (conv_w[l], _row(conv_b[l]), wg, bg, _row(lru_lambda[l]), _row(lru_norm_g[l]))
        ya, q, k_aug, v, p = _in_proj(x2, _row(norm1_g[l]), w_perm, qg, kg, fb, lru_params,
                                     seq=seq, tm=512, seg=seg)

        yb = _fox(q, k_aug, v, batch=batch, seq=seq, tq=512)

        yc = _rwkv(p, _row(rwkv_mu[l]), _row(rwkv_w0[l]),
                   _pad_rows(rwkv_w2[l], 0, low).astype(BF16), _row(rwkv_a0[l]),
                   _pad_rows(rwkv_a2[l], d_rank, low).astype(BF16),
                   _pad_rows(rwkv_g2[l], d_rank + a_rank, low).astype(BF16),
                   _row(rwkv_kk[l]), _row(rwkv_ka[l]), _row(rwkv_rk[l]),
                   _row(rwkv_ln_g[l]), _row(rwkv_ln_b[l]), batch=batch, seq=seq, tb=512)

        wo = w_out[l].astype(BF16)
        wr = jnp.concatenate(
            [router_gw[l], router_ew[l],
             jnp.zeros((d_model, LANES - N_GROUPS - n_exp), F32)], axis=1)
        wr_hi = wr.astype(BF16)
        wr = jnp.concatenate([wr_hi, (wr - wr_hi.astype(F32)).astype(BF16)], axis=1)
        br = _row(jnp.pad(jnp.concatenate([router_gb[l], router_eb[l]]),
                          (0, LANES - N_GROUPS - n_exp)))
        x1, h2, comb = _out_proj(
            ya, yb, yc, x2, wo[0:lru_w], wo[lru_w:lru_w + fox_w], wo[lru_w + fox_w:],
            _row(fox_norm_g[l]), _row(norm2_g[l]), wr, br, tm=512)

        wd = exp_w_down[l].astype(BF16).reshape(N_GROUPS, EXPERTS_PER_GROUP * d_exp, d_model)
        x2 = _moe(h2, comb, x1, exp_w_gate[l].astype(BF16), exp_w_up[l].astype(BF16), wd, tm=512)
    return x2.reshape(batch, seq, d_model)
```

```python
import functools

import jax
import jax.numpy as jnp
from jax import lax
from jax.experimental import pallas as pl
from jax.experimental.pallas import tpu as pltpu

F32 = jnp.float32
BF16 = jnp.bfloat16

NORM_EPS = 1e-6
GN_EPS = 64e-5
LRU_C = 8.0
HEAD_DIM = 64
HEAD_SHIFT = 6
LANES = 128
SUBLANES = 8
CHUNK = 64
RWKV_CUM_ROWS = 256
CONV_WIDTH = 4
N_GROUPS = 4
EXPERTS_PER_GROUP = 4
N_EXPERTS = N_GROUPS * EXPERTS_PER_GROUP
MOE_BLOCK = 128
NEG_BIG = -1e30
LOG2E = 1.4426950408889634
FOX_BIAS_PIECES = 3
FOX_DEN_ROWS = 16
VMEM_LIMIT = 56 * 1024 * 1024


def _dot(a, b):
    return jnp.dot(a, b, preferred_element_type=F32)


def _dot_nt(a, b):
    return lax.dot_general(a, b, (((1,), (1,)), ((), ())), preferred_element_type=F32)


def _dot_tn(a, b):
    return lax.dot_general(a, b, (((0,), (0,)), ((), ())), preferred_element_type=F32)


def _split_dot_right(x, ones, parts):
    acc = None
    rem = x
    for _ in range(parts):
        hi = rem.astype(BF16)
        t = _dot(hi, ones)
        acc = t if acc is None else acc + t
        rem = rem - hi.astype(F32)
    return acc


def _split_dot_left(ones, x, parts):
    acc = None
    rem = x
    for _ in range(parts):
        hi = rem.astype(BF16)
        t = _dot(ones, hi)
        acc = t if acc is None else acc + t
        rem = rem - hi.astype(F32)
    return acc


def _head_ones(n):
    r = lax.broadcasted_iota(jnp.int32, (n, n), 0) >> HEAD_SHIFT
    c = lax.broadcasted_iota(jnp.int32, (n, n), 1) >> HEAD_SHIFT
    return jnp.where(r == c, 1.0, 0.0).astype(BF16)


def _softplus(z):
    return jnp.maximum(z, 0.0) + jnp.log(1.0 + jnp.exp(-jnp.abs(z)))


def _rms(x, eps=NORM_EPS):
    return x * lax.rsqrt(jnp.mean(x * x, axis=-1, keepdims=True) + eps)


def _resident(shape):
    zeros = (0,) * len(shape)
    return pl.BlockSpec(shape, lambda *_: zeros, pipeline_mode=pl.Buffered(1))


def _in_proj_kernel(x_ref, g_ref, w_ref, qg_ref, kg_ref, fb_ref,
                    cw_ref, cb_ref, wg_ref, bg_ref, lam_ref, ng_ref,
                    ya_ref, q_ref, k_ref, v_ref, p_ref,
                    carry_ref, tail_ref, hst_ref, buf_ref,
                    *, tiles_per_seq, seg, sub):
    i = pl.program_id(0)

    @pl.when(i % tiles_per_seq == 0)
    def _():
        carry_ref[...] = jnp.zeros_like(carry_ref)
        tail_ref[...] = jnp.zeros_like(tail_ref)
        hst_ref[...] = jnp.zeros_like(hst_ref)

    fox = seg[2] - seg[1]
    n_heads = fox // HEAD_DIM
    ones = _head_ones(fox)
    inv_d = 1.0 / HEAD_DIM
    rr = lax.broadcasted_iota(jnp.int32, (sub, sub), 0)
    cc = lax.broadcasted_iota(jnp.int32, (sub, sub), 1)
    tri = jnp.where(cc <= rr, 1.0, 0.0).astype(BF16)
    lane = lax.broadcasted_iota(jnp.int32, (sub, LANES), 1)
    feat = lane < HEAD_DIM
    bias_lanes = (lane >= HEAD_DIM) & (lane < HEAD_DIM + FOX_BIAS_PIECES)
    q_bias = jnp.where(bias_lanes, 1.0, 0.0)
    piece_id = lax.rem(lane, FOX_BIAS_PIECES)
    carry = carry_ref[0:1, :]
    tail = tail_ref[...]
    h_lru = hst_ref[0:1, :]
    lru_w = (seg[1] - seg[0]) // 2

    def head_norm(t):
        return t * lax.rsqrt(_dot((t * t).astype(BF16), ones) * inv_d + NORM_EPS)

    sub_rows = [slice(r0, r0 + sub) for r0 in range(0, x_ref.shape[0], sub)]
    state = {"carry": carry, "tail": tail, "h": h_lru}

    def project_steps(rows, out):
        hb = []

        def first():
            hb.append((_rms(x_ref[rows, :]) * g_ref[...]).astype(BF16))
            out[0] = _dot(hb[0], w_ref[:, seg[0]:seg[1]])

        def segment(n):
            out[n] = _dot(hb[0], w_ref[:, seg[n]:seg[n + 1]])

        def rwkv_cols():
            p_ref[rows, :] = _dot(hb[0], w_ref[:, seg[4]:seg[5]])

        return [first, functools.partial(segment, 5), functools.partial(segment, 1),
                functools.partial(segment, 2), functools.partial(segment, 3), rwkv_cols]

    def finish_steps(idx, rows, out):
        local = {}

        def lru():
            u = out[0]
            ya_ref[rows, :], state["tail"], state["h"] = _lru_block(
                u[:, 0:lru_w], u[:, lru_w:2 * lru_w], state["tail"], state["h"], buf_ref.at[idx],
                cw_ref, cb_ref, wg_ref, bg_ref, lam_ref, ng_ref)

        def forget():
            lf = -_softplus(-(out[5] + fb_ref[...]))
            c = _split_dot_left(tri, lf, 3) + state["carry"]
            state["carry"] = c[sub - 1:sub, :]
            rem = c * (-LOG2E)
            pieces = None
            for piece in range(FOX_BIAS_PIECES):
                hi = rem.astype(BF16).astype(F32)
                pieces = hi if pieces is None else jnp.where(piece_id == piece, hi, pieces)
                rem = rem - hi
            local["pieces"] = pieces

        def spread(t, h):
            th = t[:, (h // 2) * LANES:(h // 2 + 1) * LANES]
            return pltpu.roll(th, HEAD_DIM, axis=1) if h % 2 else th

        def queries():
            q = head_norm(out[1]) * qg_ref[...]
            for h in range(n_heads):
                q_ref[h * LANES:(h + 1) * LANES, rows] = (
                    jnp.where(feat, spread(q, h), q_bias).T.astype(BF16))

        def keys():
            k = head_norm(out[2]) * kg_ref[...]
            for h in range(n_heads):
                k_bias = pltpu.roll(local["pieces"], HEAD_DIM - FOX_BIAS_PIECES * h, axis=1)
                k_ref[rows, h * LANES:(h + 1) * LANES] = jnp.where(
                    feat, spread(k, h), jnp.where(bias_lanes, k_bias, 0.0)).astype(BF16)

        def values():
            v_ref[:, rows] = out[3].T.astype(BF16)

        return [lru, forget, queries, keys, values]

    outs = [dict() for _ in sub_rows]
    for step in project_steps(sub_rows[0], outs[0]):
        step()
    for idx, rows in enumerate(sub_rows):
        ahead = project_steps(sub_rows[idx + 1], outs[idx + 1]) if idx + 1 < len(sub_rows) else []
        finish = finish_steps(idx, rows, outs[idx])
        for n in range(max(len(ahead), len(finish))):
            if n < len(ahead):
                ahead[n]()
            if n < len(finish):
                finish[n]()

    carry_ref[...] = jnp.broadcast_to(state["carry"], carry_ref.shape)
    tail_ref[...] = state["tail"]
    hst_ref[...] = jnp.broadcast_to(state["h"], hst_ref.shape)


def _in_proj(x2, g, w, qg, kg, fb, lru_params, *, seq, tm, seg):
    T, D = x2.shape
    n_out = w.shape[1]
    widths = [seg[j + 1] - seg[j] for j in range(6)]
    row = lambda i: (i, 0)
    col = lambda i: (0, i)
    outs = [
        ((T, widths[0] // 2), (tm, widths[0] // 2), row, BF16),
        ((2 * widths[1], T), (2 * widths[1], tm), col, BF16),
        ((T, 2 * widths[2]), (tm, 2 * widths[2]), row, BF16),
        ((widths[3], T), (widths[3], tm), col, BF16),
        ((T, widths[4]), (tm, widths[4]), row, F32),
    ]
    return pl.pallas_call(
        functools.partial(_in_proj_kernel, tiles_per_seq=seq // tm, seg=seg, sub=tm // 2),
        out_shape=tuple(jax.ShapeDtypeStruct(shape, dt) for shape, _, _, dt in outs),
        grid=(T // tm,),
        in_specs=[
            pl.BlockSpec((tm, D), row),
            _resident((1, D)),
            _resident((D, n_out)),
            _resident((1, widths[1])),
            _resident((1, widths[2])),
            _resident((1, widths[5])),
        ] + [_resident(t.shape) for t in lru_params],
        out_specs=tuple(pl.BlockSpec(blk, imap) for _, blk, imap, _ in outs),
        scratch_shapes=[pltpu.VMEM((SUBLANES, widths[5]), F32),
                        pltpu.VMEM((SUBLANES, widths[0] // 2), F32),
                        pltpu.VMEM((SUBLANES, widths[0] // 2), F32),
                        pltpu.VMEM((2, tm // 2 + SUBLANES, widths[0] // 2), F32)],
        compiler_params=pltpu.CompilerParams(
            dimension_semantics=("arbitrary",), vmem_limit_bytes=VMEM_LIMIT),
        name="in_proj",
    )(x2, g, w, qg, kg, fb, *lru_params)


def _shift_rows(x, d, fill, row):
    return jnp.where(row >= d, pltpu.roll(x, d, axis=0), fill)


def _lru_block(xa, ga, tail, h_prev, buf_ref, cw_ref, cb_ref, wg_ref, bg_ref, lam_ref, ng_ref):
    tb, w = xa.shape
    pad = SUBLANES
    buf_ref[0:pad, :] = tail
    buf_ref[pad:pad + tb, :] = xa
    xc = cb_ref[...] + cw_ref[CONV_WIDTH - 1:CONV_WIDTH, :] * xa
    for d in range(1, CONV_WIDTH):
        xc = xc + cw_ref[CONV_WIDTH - 1 - d:CONV_WIDTH - d, :] * buf_ref[pad - d:pad - d + tb, :]

    gates = _dot(xc.astype(BF16), wg_ref[...]) + bg_ref[...]
    r = jax.nn.sigmoid(gates[:, 0:w])
    i = jax.nn.sigmoid(gates[:, w:2 * w])
    log_a = (-LRU_C) * r * _softplus(-lam_ref[...])
    a = jnp.exp(log_a)
    b = jnp.sqrt(1.0 - a * a) * (i * xc)

    row = lax.broadcasted_iota(jnp.int32, (tb, w), 0)
    d = 1
    while d < tb:
        a_sh = _shift_rows(a, d, 1.0, row)
        b_sh = _shift_rows(b, d, 0.0, row)
        b = a * b_sh + b
        a = a * a_sh
        d *= 2
    h = b + a * h_prev
    y = jax.nn.gelu(ga) * h
    return (_rms(y) * ng_ref[...]).astype(BF16), xa[tb - pad:tb, :], h[tb - 1:tb, :]


def _fox_kernel(q_ref, k_ref, v_ref, o_ref, s_buf, mx_buf, p_buf, m_ref, acc_ref, fin_ref, *, tq):
    n_heads = k_ref.shape[1] // LANES
    nq = k_ref.shape[0] // tq
    causal = (lax.broadcasted_iota(jnp.int32, (tq, tq), 0)
              <= lax.broadcasted_iota(jnp.int32, (tq, tq), 1))
    items = [(i, j) for i in range(nq) for j in range(i + 1)]

    def scores(t, u):
        i, j = items[t]
        hl = slice(u * LANES, (u + 1) * LANES)
        s = _dot(k_ref[j * tq:(j + 1) * tq, hl], q_ref[hl, i * tq:(i + 1) * tq])
        if i == j:
            s = jnp.where(causal, s, NEG_BIG)
        s_buf[t % 2, u] = s
        mx_buf[t % 2, u] = jnp.max(s, axis=0, keepdims=True)

    def consume(t, u):
        i, j = items[t]
        slot = t % 2
        vt = jnp.concatenate([v_ref[u * HEAD_DIM:(u + 1) * HEAD_DIM, j * tq:(j + 1) * tq],
                              jnp.ones((FOX_DEN_ROWS, tq), BF16)], axis=0)
        m_new = mx_buf[slot, u] if j == 0 else jnp.maximum(m_ref[u], mx_buf[slot, u])
        p_buf[u] = jnp.exp2(s_buf[slot, u] - m_new).astype(BF16)
        acc = _dot(vt, p_buf[u])
        if j > 0:
            acc = jnp.exp2(m_ref[u] - m_new) * acc_ref[u] + acc
        if j < i:
            m_ref[u] = m_new
            acc_ref[u] = acc
        else:
            fin_ref[u * HEAD_DIM:(u + 1) * HEAD_DIM, :] = acc[0:HEAD_DIM] / acc[HEAD_DIM:HEAD_DIM + 1]
            if u == n_heads - 1:
                o_ref[i * tq:(i + 1) * tq, :] = fin_ref[...].T.astype(BF16)

    for u in range(n_heads):
        scores(0, u)
    for t in range(len(items)):
        for u in range(n_heads):
            if t + 1 < len(items):
                scores(t + 1, u)
            consume(t, u)


def _fox(q_aug, k_aug, v_t, *, batch, seq, tq):
    T, wide = k_aug.shape
    heads_per_step = 2
    blk = heads_per_step * LANES
    out_w = heads_per_step * HEAD_DIM
    steps = wide // blk
    return pl.pallas_call(
        functools.partial(_fox_kernel, tq=tq),
        out_shape=jax.ShapeDtypeStruct((T, steps * out_w), BF16),
        grid=(batch, steps),
        in_specs=[
            pl.BlockSpec((blk, seq), lambda b, h: (h, b)),
            pl.BlockSpec((seq, blk), lambda b, h: (b, h)),
            pl.BlockSpec((out_w, seq), lambda b, h: (h, b)),
        ],
        out_specs=pl.BlockSpec((seq, out_w), lambda b, h: (b, h)),
        scratch_shapes=[
            pltpu.VMEM((2, heads_per_step, tq, tq), F32),
            pltpu.VMEM((2, heads_per_step, 1, tq), F32),
            pltpu.VMEM((heads_per_step, tq, tq), BF16),
            pltpu.VMEM((heads_per_step, 1, tq), F32),
            pltpu.VMEM((heads_per_step, HEAD_DIM + FOX_DEN_ROWS, tq), F32),
            pltpu.VMEM((out_w, tq), F32),
        ],
        compiler_params=pltpu.CompilerParams(
            dimension_semantics=("parallel", "parallel"), vmem_limit_bytes=VMEM_LIMIT),
        name="fox",
    )(q_aug, k_aug, v_t)


def _stack_heads(x, lane_head, n_heads):
    zero = jnp.zeros_like(x)
    return jnp.concatenate([jnp.where(lane_head == h, x, zero) for h in range(n_heads)], axis=0)


def _rwkv_kernel(p_ref, mu_ref, w0_ref, w2_ref, a0_ref, a2_ref, g2_ref, kk_ref, ka_ref,
                 rk_ref, lng_ref, lnb_ref, y_ref, prev_ref, st_ref):
    j = pl.program_id(1)
    tb, cols = p_ref.shape
    w = y_ref.shape[1]
    n_heads = w // HEAD_DIM
    C = CHUNK
    chunk_shift = CHUNK.bit_length() - 1

    @pl.when(j == 0)
    def _():
        prev_ref[...] = jnp.zeros_like(prev_ref)
        st_ref[...] = jnp.zeros_like(st_ref)

    p = p_ref[...]
    row = lax.broadcasted_iota(jnp.int32, (tb, cols), 0)
    prev = jnp.where(row == 0, prev_ref[0:1, :], pltpu.roll(p, 1, axis=0))
    prev_ref[...] = jnp.broadcast_to(p[tb - 1:tb, :], prev_ref.shape)
    ps = p + (prev - p) * mu_ref[...]
    r = ps[:, 0:w]
    k = ps[:, w:2 * w]
    v = ps[:, 2 * w:3 * w]
    lr = ps[:, 3 * w:cols]

    wpre = w0_ref[...] + _dot(jnp.tanh(lr).astype(BF16), w2_ref[...])
    lw = -jnp.exp(-_softplus(-wpre) - 0.5)
    a = jax.nn.sigmoid(a0_ref[...] + _dot(lr.astype(BF16), a2_ref[...]))
    g = _dot(jax.nn.sigmoid(lr).astype(BF16), g2_ref[...])

    ones = _head_ones(w)
    kk = k * kk_ref[...]
    kk = kk * lax.rsqrt(jnp.maximum(_split_dot_right(kk * kk, ones, 1), 1e-24))
    k2 = k * (1.0 + (a - 1.0) * ka_ref[...])
    b = kk * a
    bonus = _split_dot_right(r * k2 * rk_ref[...], ones, 1) * v

    span = min(RWKV_CUM_ROWS, tb)
    rr = lax.broadcasted_iota(jnp.int32, (span, span), 0)
    cc = lax.broadcasted_iota(jnp.int32, (span, span), 1)
    tri = jnp.where(((rr >> chunk_shift) == (cc >> chunk_shift)) & (cc <= rr), 1.0, 0.0).astype(BF16)
    LW = jnp.concatenate(
        [_split_dot_left(tri, lw[r0:r0 + span], 2) for r0 in range(0, tb, span)], axis=0)
    LT = jnp.concatenate(
        [jnp.broadcast_to(LW[c * C + C - 1:c * C + C, :], (C, w)) for c in range(tb // C)], axis=0)
    e_out = jnp.exp(-LW)
    e_end = jnp.exp(LT - LW)
    at = (-kk) * jnp.exp(LW - lw)
    rt = r * jnp.exp(LW)
    bt = b * e_out
    kt = k2 * e_out
    b_end = b * e_end
    k_end = k2 * e_end
    w_tot = jnp.exp(LT)

    lane_head = lax.broadcasted_iota(jnp.int32, (C, w), 1) >> HEAD_SHIFT
    t_idx = lax.broadcasted_iota(jnp.int32, (C, w), 0)
    j_idx = lax.broadcasted_iota(jnp.int32, (C, w), 1) & (HEAD_DIM - 1)
    strict = j_idx < t_idx
    incl = j_idx <= t_idx
    eye = jnp.where(j_idx == t_idx, 1.0, 0.0)
    sq_r = lax.broadcasted_iota(jnp.int32, (w, w), 0) >> HEAD_SHIFT
    sq_c = lax.broadcasted_iota(jnp.int32, (w, w), 1) >> HEAD_SHIFT
    block_diag = sq_r == sq_c
    stack = functools.partial(_stack_heads, lane_head=lane_head, n_heads=n_heads)

    chunks = range(tb // C)
    rows_of = [slice(c * C, (c + 1) * C) for c in chunks]
    bf = lambda t: t.astype(BF16)
    cat0 = lambda *ts: jnp.concatenate(ts, axis=0)
    cat1 = lambda *ts: jnp.concatenate(ts, axis=1)

    A = [_dot_nt(bf(cat0(at[sl], rt[sl])), cat0(stack(bf(bt[sl])), stack(bf(kt[sl]))))
         for sl in rows_of]
    a_ab = [jnp.where(strict, a[0:C, 0:w], 0.0) for a in A]
    a_ak = [jnp.where(strict, a[0:C, w:2 * w], 0.0) for a in A]
    a_rb = [jnp.where(incl, a[C:2 * C, 0:w], 0.0) for a in A]
    a_rk = [jnp.where(incl, a[C:2 * C, w:2 * w], 0.0) for a in A]

    pw = [_dot(bf(l), stack(bf(l))) for l in a_ab]
    tinv = [eye + l for l in a_ab]
    sz = 2
    while 2 * sz < C:
        both = [_dot(bf(cat0(t, p)), stack(bf(p))) for t, p in zip(tinv, pw)]
        tinv = [t + r[0:C] for t, r in zip(tinv, both)]
        pw = [r[C:2 * C] for r in both]
        sz *= 2
    tinv = [t + _dot(bf(t), stack(bf(p))) for t, p in zip(tinv, pw)]

    yo = [_dot(bf(cat0(ak, ark)), stack(bf(v[sl]))) for ak, ark, sl in zip(a_ak, a_rk, rows_of)]
    gu = [_dot(bf(t), cat1(stack(bf(at[sl])), stack(bf(y[0:C]))))
          for t, y, sl in zip(tinv, yo, rows_of)]
    pq = [_dot(bf(rb), cat1(stack(bf(x[:, 0:w])), stack(bf(x[:, w:2 * w]))))
          for rb, x in zip(a_rb, gu)]
    p_mat = [rt[sl] + x[:, 0:w] for sl, x in zip(rows_of, pq)]
    q_mat = [y[C:2 * C] + x[:, w:2 * w] for y, x in zip(yo, pq)]
    m_mat = [jnp.where(block_diag, _dot_tn(bf(x[:, 0:w]), bf(b_end[sl])), 0.0)
             for x, sl in zip(gu, rows_of)]
    n_mat = [jnp.where(block_diag,
                       _dot_tn(bf(cat0(x[:, w:2 * w], v[sl])), bf(cat0(b_end[sl], k_end[sl]))), 0.0)
             for x, sl in zip(gu, rows_of)]

    st = st_ref[...]
    outs = []
    for c in chunks:
        stb = bf(st)
        outs.append(_dot_nt(bf(p_mat[c]), stb) + q_mat[c])
        st = st * w_tot[c * C:c * C + 1, :] + _dot(stb, bf(m_mat[c])) + n_mat[c]
    st_ref[...] = st

    o = jnp.concatenate(outs, axis=0)
    inv_d = 1.0 / HEAD_DIM
    mean = _split_dot_right(o, ones, 1) * inv_d
    cen = o - mean
    var = _split_dot_right(cen * cen, ones, 1) * inv_d
    o = cen * lax.rsqrt(var + GN_EPS) * lng_ref[...] + lnb_ref[...]
    y_ref[...] = ((o + bonus) * g).astype(BF16)


def _rwkv(p, mu, w0, w2, a0, a2, g2, kkp, ka, rk, lng, lnb, *, batch, seq, tb):
    T, cols = p.shape
    w = w0.shape[1]
    nb = seq // tb
    vec = _resident((1, w))
    low = _resident((cols - 3 * w, w))
    return pl.pallas_call(
        _rwkv_kernel,
        out_shape=jax.ShapeDtypeStruct((T, w), BF16),
        grid=(batch, nb),
        in_specs=[
            pl.BlockSpec((tb, cols), lambda b, j: (b * nb + j, 0)),
            _resident((1, cols)), vec, low, vec, low, low, vec, vec, vec, vec, vec,
        ],
        out_specs=pl.BlockSpec((tb, w), lambda b, j: (b * nb + j, 0)),
        scratch_shapes=[pltpu.VMEM((SUBLANES, cols), F32), pltpu.VMEM((w, w), F32)],
        compiler_params=pltpu.CompilerParams(
            dimension_semantics=("parallel", "arbitrary"), vmem_limit_bytes=VMEM_LIMIT),
        name="rwkv",
    )(p, mu, w0, w2, a0, a2, g2, kkp, ka, rk, lng, lnb)


def _out_proj_kernel(ya_ref, yb_ref, yc_ref, x_ref, wa_ref, wb_ref, wc_ref, fg_ref, n2_ref,
                     wr_ref, br_ref, x1_ref, h2_ref, comb_ref, *, sub):
    sub_rows = [slice(r0, r0 + sub) for r0 in range(0, x_ref.shape[0], sub)]
    mixed = []
    for rows in sub_rows:
        ybn = (_rms(yb_ref[rows, :].astype(F32)) * fg_ref[...]).astype(BF16)
        mixed.append(_dot(ya_ref[rows, :], wa_ref[...]) + _dot(ybn, wb_ref[...])
                     + _dot(yc_ref[rows, :], wc_ref[...]))
    for rows, y in zip(sub_rows, mixed):
        x1 = x_ref[rows, :] + y
        x1_ref[rows, :] = x1
        h2 = _rms(x1) * n2_ref[...]
        h_hi = h2.astype(BF16)
        h2_ref[rows, :] = h_hi

        h_lo = (h2 - h_hi.astype(F32)).astype(BF16)
        hw = _dot(h_hi, wr_ref[...])
        logits = (hw[:, 0:LANES] + hw[:, LANES:2 * LANES] + _dot(h_lo, wr_ref[:, 0:LANES])
                  + br_ref[...])
        comb_ref[rows, :] = _route(logits)


def _route(logits):
    lane = lax.broadcasted_iota(jnp.int32, logits.shape, 1)
    lane_f = lane.astype(F32)
    far = float(LANES)

    def first_argmax(vals, vmax):
        return jnp.min(jnp.where(vals == vmax, lane_f, far), axis=-1, keepdims=True)

    gl = jnp.where(lane < N_GROUPS, logits, NEG_BIG)
    gmax = jnp.max(gl, axis=-1, keepdims=True)
    gidx = first_argmax(gl, gmax)
    g_p = 1.0 / jnp.sum(jnp.exp(gl - gmax), axis=-1, keepdims=True)

    e_group = ((lane - N_GROUPS) >> 2).astype(F32)
    el = jnp.where((lane >= N_GROUPS) & (e_group == gidx), logits, NEG_BIG)
    emax = jnp.max(el, axis=-1, keepdims=True)
    esum = jnp.sum(jnp.exp(el - emax), axis=-1, keepdims=True)
    i1 = first_argmax(el, emax)
    el2 = jnp.where(lane_f == i1, NEG_BIG, el)
    emax2 = jnp.max(el2, axis=-1, keepdims=True)
    i2 = first_argmax(el2, emax2)
    p1 = 1.0 / esum
    p2 = jnp.exp(emax2 - emax) / esum
    den = p1 + p2
    return (jnp.where(lane_f == i1, g_p * (p1 / den), 0.0)
            + jnp.where(lane_f == i2, g_p * (p2 / den), 0.0)
            + jnp.where(lane_f == gidx, 1.0, 0.0))


def _out_proj(ya, yb, yc, x2, wa, wb, wc, fg, n2, wr, br, *, tm):
    T, D = x2.shape
    row = lambda i: (i, 0)
    return pl.pallas_call(
        functools.partial(_out_proj_kernel, sub=tm // 2),
        out_shape=(jax.ShapeDtypeStruct((T, D), F32), jax.ShapeDtypeStruct((T, D), BF16),
                   jax.ShapeDtypeStruct((T, LANES), F32)),
        grid=(T // tm,),
        in_specs=[
            pl.BlockSpec((tm, ya.shape[1]), row),
            pl.BlockSpec((tm, yb.shape[1]), row),
            pl.BlockSpec((tm, yc.shape[1]), row),
            pl.BlockSpec((tm, D), row),
            _resident(wa.shape), _resident(wb.shape), _resident(wc.shape),
            _resident(fg.shape), _resident(n2.shape), _resident(wr.shape), _resident(br.shape),
        ],
        out_specs=(pl.BlockSpec((tm, D), row), pl.BlockSpec((tm, D), row),
                   pl.BlockSpec((tm, LANES), row)),
        compiler_params=pltpu.CompilerParams(
            dimension_semantics=("parallel",), vmem_limit_bytes=VMEM_LIMIT),
        name="out_proj",
    )(ya, yb, yc, x2, wa, wb, wc, fg, n2, wr, br)


def _moe_kernel(h_ref, route_ref, x_ref, wg_ref, wu_ref, wd_ref, o_ref, hs_ref, rs_ref, ys_ref):
    tm = h_ref.shape[0]
    n_groups = wd_ref.shape[0]
    route = route_ref[...]
    lane = lax.broadcasted_iota(jnp.int32, (tm, LANES), 1)
    onehot = jnp.where(lane < n_groups, route, 0.0)
    rr = lax.broadcasted_iota(jnp.int32, (tm, tm), 0)
    cc = lax.broadcasted_iota(jnp.int32, (tm, tm), 1)
    tri = jnp.where(cc <= rr, 1.0, 0.0).astype(BF16)
    cum = _dot(tri, onehot.astype(BF16))
    tot = cum[tm - 1:tm, :]
    start = jnp.zeros_like(tot)
    for d in range(1, n_groups):
        start = start + jnp.where(lane[0:1] >= d, pltpu.roll(tot, d, axis=1), 0.0)
    end = start + tot
    pos = jnp.sum(jnp.where(onehot > 0.0, start + cum - 1.0, 0.0), axis=-1, keepdims=True)
    perm_t = jnp.where(pos == cc.astype(F32), 1.0, 0.0).astype(BF16)
    pos_row = jnp.broadcast_to(pos, (tm, LANES)).T[0:1, :]
    perm = jnp.where(rr.astype(F32) == pos_row, 1.0, 0.0).astype(BF16)

    hs_ref[...] = _dot(perm, h_ref[...]).astype(BF16)
    rem = route
    sorted_route = None
    for _ in range(2):
        hi = rem.astype(BF16)
        t = _dot(perm, hi)
        sorted_route = t if sorted_route is None else sorted_route + t
        rem = rem - hi.astype(F32)
    rs_ref[...] = sorted_route
    ys_ref[...] = jnp.zeros_like(ys_ref)

    for g in range(n_groups):
        g_start = start[0, g]
        g_end = end[0, g]
        for r0 in range(0, tm, MOE_BLOCK):

            @pl.when((g_start < r0 + MOE_BLOCK) & (g_end > r0))
            def _(g=g, r0=r0):
                rows = slice(r0, r0 + MOE_BLOCK)
                hb = hs_ref[rows, :]
                rw = rs_ref[rows, :]
                hids = []
                for e in range(g * EXPERTS_PER_GROUP, (g + 1) * EXPERTS_PER_GROUP):
                    gate = _dot(hb, wg_ref[e])
                    hid = gate * jax.nn.sigmoid(gate) * _dot(hb, wu_ref[e])
                    hids.append((hid * rw[:, N_GROUPS + e:N_GROUPS + e + 1]).astype(BF16))
                ys_ref[rows, :] += _dot(jnp.concatenate(hids, axis=1), wd_ref[g])

    o_ref[...] = x_ref[...] + _dot(perm_t, ys_ref[...].astype(BF16))


def _moe(h2, route, x1, wg, wu, wd, *, tm):
    T, D = x1.shape
    row = lambda i: (i, 0)
    return pl.pallas_call(
        _moe_kernel,
        out_shape=jax.ShapeDtypeStruct((T, D), F32),
        grid=(T // tm,),
        in_specs=[
            pl.BlockSpec((tm, D), row),
            pl.BlockSpec((tm, LANES), row),
            pl.BlockSpec((tm, D), row),
            _resident(wg.shape),
            _resident(wu.shape),
            _resident(wd.shape),
        ],
        out_specs=pl.BlockSpec((tm, D), row),
        scratch_shapes=[pltpu.VMEM((tm, D), BF16), pltpu.VMEM((tm, LANES), F32),
                        pltpu.VMEM((tm, D), F32)],
        compiler_params=pltpu.CompilerParams(
            dimension_semantics=("parallel",), vmem_limit_bytes=VMEM_LIMIT),
        name="moe",
    )(h2, route, x1, wg, wu, wd)


def _block_diag(wh):
    n_heads, d, _ = wh.shape
    eye = jnp.eye(n_heads, dtype=wh.dtype)
    return jnp.einsum("hij,hg->higj", wh, eye).reshape(n_heads * d, n_heads * d)


def _row(vec):
    return vec.reshape(1, -1).astype(F32)


def _pad_rows(mat, start, total):
    return jnp.zeros((total, mat.shape[1]), mat.dtype).at[start:start + mat.shape[0]].set(mat)


def kernel(x, norm1_g, w_in, conv_w, conv_b, lru_wa, lru_ba, lru_wx, lru_bx, lru_lambda,
           lru_norm_g, fox_fb, fox_qnorm_g, fox_knorm_g, fox_norm_g, rwkv_mu, rwkv_w0,
           rwkv_w2, rwkv_a0, rwkv_a2, rwkv_g2, rwkv_kk, rwkv_ka, rwkv_rk, rwkv_ln_g,
           rwkv_ln_b, w_out, norm2_g, router_gw, router_gb, router_ew, router_eb,
           exp_w_gate, exp_w_up, exp_w_down):
    batch, seq, d_model = x.shape
    depth = w_in.shape[0]
    lru_w = conv_w.shape[2]
    fox_heads = fox_fb.shape[1]
    fox_w = fox_heads * HEAD_DIM
    rwkv_w = rwkv_w0.shape[1]
    rwkv_cols = rwkv_mu.shape[1]
    d_rank, a_rank, g_rank = rwkv_w2.shape[1], rwkv_a2.shape[1], rwkv_g2.shape[1]
    low = d_rank + a_rank + g_rank
    n_exp, _, d_exp = exp_w_gate.shape[1:]
    assert low == LANES and rwkv_cols == 3 * rwkv_w + low
    assert fox_heads <= LANES and n_exp == N_EXPERTS

    o_fox = 2 * lru_w
    o_fl = o_fox + 3 * fox_w
    o_rwkv = o_fl + fox_heads
    seg = (0, o_fox, o_fox + fox_w, o_fox + 2 * fox_w, o_fl, o_fl + rwkv_cols,
           o_fl + rwkv_cols + LANES)

    x2 = x.reshape(batch * seq, d_model)
    for l in range(depth):
        wl = w_in[l]
        n_fl = FOX_BIAS_PIECES * fox_heads
        w_perm = jnp.concatenate(
            [wl[:, 0:o_fl], wl[:, o_rwkv:o_rwkv + rwkv_cols],
             jnp.repeat(wl[:, o_fl:o_rwkv], FOX_BIAS_PIECES, axis=1),
             jnp.zeros((d_model, LANES - n_fl), wl.dtype)], axis=1).astype(BF16)
        qg = _row(jnp.tile(fox_qnorm_g[l], fox_heads) * (HEAD_DIM ** -0.5 * LOG2E))
        kg = _row(jnp.tile(fox_knorm_g[l], fox_heads))
        fb = _row(jnp.pad(jnp.repeat(fox_fb[l], FOX_BIAS_PIECES), (0, LANES - n_fl)))
        wg = jnp.concatenate([_block_diag(lru_wa[l]), _block_diag(lru_wx[l])], axis=1).astype(BF16)
        bg = _row(jnp.concatenate([lru_ba[l], lru_bx[l]]))
        lru_params = (conv_w[l], _row(conv_b[l]), wg, bg, _row(lru_lambda[l]), _row(lru_norm_g[l]))
        ya, q, k_aug, v, p = _in_proj(x2, _row(norm1_g[l]), w_perm, qg, kg, fb, lru_params,
                                     seq=seq, tm=512, seg=seg)

        yb = _fox(q, k_aug, v, batch=batch, seq=seq, tq=512)

        yc = _rwkv(p, _row(rwkv_mu[l]), _row(rwkv_w0[l]),
                   _pad_rows(rwkv_w2[l], 0, low).astype(BF16), _row(rwkv_a0[l]),
                   _pad_rows(rwkv_a2[l], d_rank, low).astype(BF16),
                   _pad_rows(rwkv_g2[l], d_rank + a_rank, low).astype(BF16),
                   _row(rwkv_kk[l]), _row(rwkv_ka[l]), _row(rwkv_rk[l]),
                   _row(rwkv_ln_g[l]), _row(rwkv_ln_b[l]), batch=batch, seq=seq, tb=512)

        wo = w_out[l].astype(BF16)
        wr = jnp.concatenate(
            [router_gw[l], router_ew[l],
             jnp.zeros((d_model, LANES - N_GROUPS - n_exp), F32)], axis=1)
        wr_hi = wr.astype(BF16)
        wr = jnp.concatenate([wr_hi, (wr - wr_hi.astype(F32)).astype(BF16)], axis=1)
        br = _row(jnp.pad(jnp.concatenate([router_gb[l], router_eb[l]]),
                          (0, LANES - N_GROUPS - n_exp)))
        x1, h2, comb = _out_proj(
            ya, yb, yc, x2, wo[0:lru_w], wo[lru_w:lru_w + fox_w], wo[lru_w + fox_w:],
            _row(fox_norm_g[l]), _row(norm2_g[l]), wr, br, tm=512)

        wd = exp_w_down[l].astype(BF16).reshape(N_GROUPS, EXPERTS_PER_GROUP * d_exp, d_model)
        x2 = _moe(h2, comb, x1, exp_w_gate[l].astype(BF16), exp_w_up[l].astype(BF16), wd, tm=512)
    return x2.reshape(batch, seq, d_model)
```

```python
import functools

import jax
import jax.numpy as jnp
from jax import lax
from jax.experimental import pallas as pl
from jax.experimental.pallas import tpu as pltpu

F32 = jnp.float32
BF16 = jnp.bfloat16

NORM_EPS = 1e-6
GN_EPS = 64e-5
LRU_C = 8.0
HEAD_DIM = 64
HEAD_SHIFT = 6
LANES = 128
SUBLANES = 8
CHUNK = 64
TOKEN_TILE = 512
FOX_BLOCK = 512
RWKV_BLOCK = 512
IN_PROJ_SUBTILES = 2
RWKV_CUM_ROWS = 256
CONV_WIDTH = 4
N_GROUPS = 4
EXPERTS_PER_GROUP = 4
N_EXPERTS = N_GROUPS * EXPERTS_PER_GROUP
MOE_BLOCK = 128
NEG_BIG = -1e30
LOG2E = 1.4426950408889634
FOX_BIAS_PIECES = 3
FOX_DEN_ROWS = 16
VMEM_LIMIT = 56 * 1024 * 1024


def _dot(a, b):
    return jnp.dot(a, b, preferred_element_type=F32)


def _dot_nt(a, b):
    return lax.dot_general(a, b, (((1,), (1,)), ((), ())), preferred_element_type=F32)


def _dot_tn(a, b):
    return lax.dot_general(a, b, (((0,), (0,)), ((), ())), preferred_element_type=F32)


def _split_dot_right(x, ones, parts):
    acc = None
    rem = x
    for _ in range(parts):
        hi = rem.astype(BF16)
        t = _dot(hi, ones)
        acc = t if acc is None else acc + t
        rem = rem - hi.astype(F32)
    return acc


def _split_dot_left(ones, x, parts):
    acc = None
    rem = x
    for _ in range(parts):
        hi = rem.astype(BF16)
        t = _dot(ones, hi)
        acc = t if acc is None else acc + t
        rem = rem - hi.astype(F32)
    return acc


def _head_ones(n):
    r = lax.broadcasted_iota(jnp.int32, (n, n), 0) >> HEAD_SHIFT
    c = lax.broadcasted_iota(jnp.int32, (n, n), 1) >> HEAD_SHIFT
    return jnp.where(r == c, 1.0, 0.0).astype(BF16)


def _softplus(z):
    return jnp.maximum(z, 0.0) + jnp.log(1.0 + jnp.exp(-jnp.abs(z)))


def _rms(x, eps=NORM_EPS):
    return x * lax.rsqrt(jnp.mean(x * x, axis=-1, keepdims=True) + eps)


def _resident(shape):
    zeros = (0,) * len(shape)
    return pl.BlockSpec(shape, lambda *_: zeros, pipeline_mode=pl.Buffered(1))


def _in_proj_kernel(x_ref, g_ref, w_ref, qg_ref, kg_ref, fb_ref,
                    cw_ref, cb_ref, wg_ref, bg_ref, lam_ref, ng_ref,
                    ya_ref, q_ref, k_ref, v_ref, p_ref,
                    carry_ref, tail_ref, hst_ref, buf_ref,
                    *, tiles_per_seq, seg, sub):
    i = pl.program_id(0)

    @pl.when(i % tiles_per_seq == 0)
    def _():
        carry_ref[...] = jnp.zeros_like(carry_ref)
        tail_ref[...] = jnp.zeros_like(tail_ref)
        hst_ref[...] = jnp.zeros_like(hst_ref)

    fox = seg[2] - seg[1]
    n_heads = fox // HEAD_DIM
    ones = _head_ones(fox)
    inv_d = 1.0 / HEAD_DIM
    rr = lax.broadcasted_iota(jnp.int32, (sub, sub), 0)
    cc = lax.broadcasted_iota(jnp.int32, (sub, sub), 1)
    tri = jnp.where(cc <= rr, 1.0, 0.0).astype(BF16)
    lane = lax.broadcasted_iota(jnp.int32, (sub, LANES), 1)
    feat = lane < HEAD_DIM
    bias_lanes = (lane >= HEAD_DIM) & (lane < HEAD_DIM + FOX_BIAS_PIECES)
    q_bias = jnp.where(bias_lanes, 1.0, 0.0)
    piece_id = lax.rem(lane, FOX_BIAS_PIECES)
    carry = carry_ref[0:1, :]
    tail = tail_ref[...]
    h_lru = hst_ref[0:1, :]
    lru_w = (seg[1] - seg[0]) // 2

    def head_norm(t):
        return t * lax.rsqrt(_dot((t * t).astype(BF16), ones) * inv_d + NORM_EPS)

    sub_rows = [slice(r0, r0 + sub) for r0 in range(0, x_ref.shape[0], sub)]
    state = {"carry": carry, "tail": tail, "h": h_lru}

    def project_steps(rows, out):
        hb = []

        def first():
            hb.append((_rms(x_ref[rows, :]) * g_ref[...]).astype(BF16))
            out[0] = _dot(hb[0], w_ref[:, seg[0]:seg[1]])

        def segment(n):
            out[n] = _dot(hb[0], w_ref[:, seg[n]:seg[n + 1]])

        def rwkv_cols():
            p_ref[rows, :] = _dot(hb[0], w_ref[:, seg[4]:seg[5]])

        return [first, functools.partial(segment, 5), functools.partial(segment, 1),
                functools.partial(segment, 2), functools.partial(segment, 3), rwkv_cols]

    def finish_steps(idx, rows, out):
        local = {}

        def lru():
            u = out[0]
            ya_ref[rows, :], state["tail"], state["h"] = _lru_block(
                u[:, 0:lru_w], u[:, lru_w:2 * lru_w], state["tail"], state["h"], buf_ref.at[idx],
                cw_ref, cb_ref, wg_ref, bg_ref, lam_ref, ng_ref)

        def forget():
            lf = -_softplus(-(out[5] + fb_ref[...]))
            c = _split_dot_left(tri, lf, 3) + state["carry"]
            state["carry"] = c[sub - 1:sub, :]
            rem = c * (-LOG2E)
            pieces = None
            for piece in range(FOX_BIAS_PIECES):
                hi = rem.astype(BF16).astype(F32)
                pieces = hi if pieces is None else jnp.where(piece_id == piece, hi, pieces)
                rem = rem - hi
            local["pieces"] = pieces

        def spread(t, h):
            th = t[:, (h // 2) * LANES:(h // 2 + 1) * LANES]
            return pltpu.roll(th, HEAD_DIM, axis=1) if h % 2 else th

        def queries():
            q = head_norm(out[1]) * qg_ref[...]
            for h in range(n_heads):
                q_ref[h * LANES:(h + 1) * LANES, rows] = (
                    jnp.where(feat, spread(q, h), q_bias).T.astype(BF16))

        def keys():
            k = head_norm(out[2]) * kg_ref[...]
            for h in range(n_heads):
                k_bias = pltpu.roll(local["pieces"], HEAD_DIM - FOX_BIAS_PIECES * h, axis=1)
                k_ref[rows, h * LANES:(h + 1) * LANES] = jnp.where(
                    feat, spread(k, h), jnp.where(bias_lanes, k_bias, 0.0)).astype(BF16)

        def values():
            v_ref[:, rows] = out[3].T.astype(BF16)

        return [lru, forget, queries, keys, values]

    outs = [dict() for _ in sub_rows]
    for step in project_steps(sub_rows[0], outs[0]):
        step()
    for idx, rows in enumerate(sub_rows):
        ahead = project_steps(sub_rows[idx + 1], outs[idx + 1]) if idx + 1 < len(sub_rows) else []
        finish = finish_steps(idx, rows, outs[idx])
        for n in range(max(len(ahead), len(finish))):
            if n < len(ahead):
                ahead[n]()
            if n < len(finish):
                finish[n]()

    carry_ref[...] = jnp.broadcast_to(state["carry"], carry_ref.shape)
    tail_ref[...] = state["tail"]
    hst_ref[...] = jnp.broadcast_to(state["h"], hst_ref.shape)


def _in_proj(x2, g, w, qg, kg, fb, lru_params, *, seq, tm, seg):
    T, D = x2.shape
    n_out = w.shape[1]
    widths = [seg[j + 1] - seg[j] for j in range(6)]
    row = lambda i: (i, 0)
    col = lambda i: (0, i)
    outs = [
        ((T, widths[0] // 2), (tm, widths[0] // 2), row, BF16),
        ((2 * widths[1], T), (2 * widths[1], tm), col, BF16),
        ((T, 2 * widths[2]), (tm, 2 * widths[2]), row, BF16),
        ((widths[3], T), (widths[3], tm), col, BF16),
        ((T, widths[4]), (tm, widths[4]), row, F32),
    ]
    return pl.pallas_call(
        functools.partial(_in_proj_kernel, tiles_per_seq=seq // tm, seg=seg,
                          sub=tm // IN_PROJ_SUBTILES),
        out_shape=tuple(jax.ShapeDtypeStruct(shape, dt) for shape, _, _, dt in outs),
        grid=(T // tm,),
        in_specs=[
            pl.BlockSpec((tm, D), row),
            _resident((1, D)),
            _resident((D, n_out)),
            _resident((1, widths[1])),
            _resident((1, widths[2])),
            _resident((1, widths[5])),
        ] + [_resident(t.shape) for t in lru_params],
        out_specs=tuple(pl.BlockSpec(blk, imap) for _, blk, imap, _ in outs),
        scratch_shapes=[pltpu.VMEM((SUBLANES, widths[5]), F32),
                        pltpu.VMEM((SUBLANES, widths[0] // 2), F32),
                        pltpu.VMEM((SUBLANES, widths[0] // 2), F32),
                        pltpu.VMEM((IN_PROJ_SUBTILES, tm // IN_PROJ_SUBTILES + SUBLANES,
                                    widths[0] // 2), F32)],
        compiler_params=pltpu.CompilerParams(
            dimension_semantics=("arbitrary",), vmem_limit_bytes=VMEM_LIMIT),
        name="in_proj",
    )(x2, g, w, qg, kg, fb, *lru_params)


def _shift_rows(x, d, fill, row):
    return jnp.where(row >= d, pltpu.roll(x, d, axis=0), fill)


def _lru_block(xa, ga, tail, h_prev, buf_ref, cw_ref, cb_ref, wg_ref, bg_ref, lam_ref, ng_ref):
    tb, w = xa.shape
    pad = SUBLANES
    buf_ref[0:pad, :] = tail
    buf_ref[pad:pad + tb, :] = xa
    xc = cb_ref[...] + cw_ref[CONV_WIDTH - 1:CONV_WIDTH, :] * xa
    for d in range(1, CONV_WIDTH):
        xc = xc + cw_ref[CONV_WIDTH - 1 - d:CONV_WIDTH - d, :] * buf_ref[pad - d:pad - d + tb, :]

    gates = _dot(xc.astype(BF16), wg_ref[...]) + bg_ref[...]
    r = jax.nn.sigmoid(gates[:, 0:w])
    i = jax.nn.sigmoid(gates[:, w:2 * w])
    log_a = (-LRU_C) * r * _softplus(-lam_ref[...])
    a = jnp.exp(log_a)
    b = jnp.sqrt(1.0 - a * a) * (i * xc)

    row = lax.broadcasted_iota(jnp.int32, (tb, w), 0)
    d = 1
    while d < tb:
        a_sh = _shift_rows(a, d, 1.0, row)
        b_sh = _shift_rows(b, d, 0.0, row)
        b = a * b_sh + b
        a = a * a_sh
        d *= 2
    h = b + a * h_prev
    y = jax.nn.gelu(ga) * h
    return (_rms(y) * ng_ref[...]).astype(BF16), xa[tb - pad:tb, :], h[tb - 1:tb, :]


def _fox_kernel(q_ref, k_ref, v_ref, o_ref, s_buf, mx_buf, p_buf, m_ref, acc_ref, fin_ref, *, tq):
    n_heads = k_ref.shape[1] // LANES
    nq = k_ref.shape[0] // tq
    causal = (lax.broadcasted_iota(jnp.int32, (tq, tq), 0)
              <= lax.broadcasted_iota(jnp.int32, (tq, tq), 1))
    items = [(i, j) for i in range(nq) for j in range(i + 1)]

    def scores(t, u):
        i, j = items[t]
        hl = slice(u * LANES, (u + 1) * LANES)
        s = _dot(k_ref[j * tq:(j + 1) * tq, hl], q_ref[hl, i * tq:(i + 1) * tq])
        if i == j:
            s = jnp.where(causal, s, NEG_BIG)
        s_buf[t % 2, u] = s
        mx_buf[t % 2, u] = jnp.max(s, axis=0, keepdims=True)

    def softmax_numerators(t, u):
        _, j = items[t]
        slot = t % 2
        m_new = mx_buf[slot, u] if j == 0 else jnp.maximum(m_ref[u], mx_buf[slot, u])
        p_buf[u] = jnp.exp2(s_buf[slot, u] - m_new).astype(BF16)
        return m_new

    def accumulate(t, u, m_new):
        i, j = items[t]
        vt = jnp.concatenate([v_ref[u * HEAD_DIM:(u + 1) * HEAD_DIM, j * tq:(j + 1) * tq],
                              jnp.ones((FOX_DEN_ROWS, tq), BF16)], axis=0)
        acc = _dot(vt, p_buf[u])
        if j > 0:
            acc = jnp.exp2(m_ref[u] - m_new) * acc_ref[u] + acc
        if j < i:
            m_ref[u] = m_new
            acc_ref[u] = acc
        else:
            fin_ref[u * HEAD_DIM:(u + 1) * HEAD_DIM, :] = acc[0:HEAD_DIM] / acc[HEAD_DIM:HEAD_DIM + 1]
            if u == n_heads - 1:
                o_ref[i * tq:(i + 1) * tq, :] = fin_ref[...].T.astype(BF16)

    for u in range(n_heads):
        scores(0, u)
    for t in range(len(items)):
        for u in range(n_heads):
            m_new = softmax_numerators(t, u)
            if t + 1 < len(items):
                scores(t + 1, u)
            accumulate(t, u, m_new)


def _fox(q_aug, k_aug, v_t, *, batch, seq, tq):
    T, wide = k_aug.shape
    heads_per_step = 2
    blk = heads_per_step * LANES
    out_w = heads_per_step * HEAD_DIM
    steps = wide // blk
    return pl.pallas_call(
        functools.partial(_fox_kernel, tq=tq),
        out_shape=jax.ShapeDtypeStruct((T, steps * out_w), BF16),
        grid=(batch, steps),
        in_specs=[
            pl.BlockSpec((blk, seq), lambda b, h: (h, b)),
            pl.BlockSpec((seq, blk), lambda b, h: (b, h)),
            pl.BlockSpec((out_w, seq), lambda b, h: (h, b)),
        ],
        out_specs=pl.BlockSpec((seq, out_w), lambda b, h: (b, h)),
        scratch_shapes=[
            pltpu.VMEM((2, heads_per_step, tq, tq), F32),
            pltpu.VMEM((2, heads_per_step, 1, tq), F32),
            pltpu.VMEM((heads_per_step, tq, tq), BF16),
            pltpu.VMEM((heads_per_step, 1, tq), F32),
            pltpu.VMEM((heads_per_step, HEAD_DIM + FOX_DEN_ROWS, tq), F32),
            pltpu.VMEM((out_w, tq), F32),
        ],
        compiler_params=pltpu.CompilerParams(
            dimension_semantics=("parallel", "parallel"), vmem_limit_bytes=VMEM_LIMIT),
        name="fox",
    )(q_aug, k_aug, v_t)


def _stack_heads(x, lane_head, n_heads):
    zero = jnp.zeros_like(x)
    return jnp.concatenate([jnp.where(lane_head == h, x, zero) for h in range(n_heads)], axis=0)


def _rwkv_kernel(p_ref, mu_ref, w0_ref, w2_ref, a0_ref, a2_ref, g2_ref, kk_ref, ka_ref,
                 rk_ref, lng_ref, lnb_ref, y_ref, prev_ref, st_ref):
    j = pl.program_id(1)
    tb, cols = p_ref.shape
    w = y_ref.shape[1]
    n_heads = w // HEAD_DIM
    C = CHUNK
    chunk_shift = CHUNK.bit_length() - 1

    @pl.when(j == 0)
    def _():
        prev_ref[...] = jnp.zeros_like(prev_ref)
        st_ref[...] = jnp.zeros_like(st_ref)

    p = p_ref[...]
    row = lax.broadcasted_iota(jnp.int32, (tb, cols), 0)
    prev = jnp.where(row == 0, prev_ref[0:1, :], pltpu.roll(p, 1, axis=0))
    prev_ref[...] = jnp.broadcast_to(p[tb - 1:tb, :], prev_ref.shape)
    ps = p + (prev - p) * mu_ref[...]
    r = ps[:, 0:w]
    k = ps[:, w:2 * w]
    v = ps[:, 2 * w:3 * w]
    lr = ps[:, 3 * w:cols]

    wpre = w0_ref[...] + _dot(jnp.tanh(lr).astype(BF16), w2_ref[...])
    lw = -jnp.exp(-_softplus(-wpre) - 0.5)
    a = jax.nn.sigmoid(a0_ref[...] + _dot(lr.astype(BF16), a2_ref[...]))
    g = _dot(jax.nn.sigmoid(lr).astype(BF16), g2_ref[...])

    ones = _head_ones(w)
    kk = k * kk_ref[...]
    kk = kk * lax.rsqrt(jnp.maximum(_split_dot_right(kk * kk, ones, 1), 1e-24))
    k2 = k * (1.0 + (a - 1.0) * ka_ref[...])
    b = kk * a
    bonus = _split_dot_right(r * k2 * rk_ref[...], ones, 1) * v

    span = min(RWKV_CUM_ROWS, tb)
    rr = lax.broadcasted_iota(jnp.int32, (span, span), 0)
    cc = lax.broadcasted_iota(jnp.int32, (span, span), 1)
    tri = jnp.where(((rr >> chunk_shift) == (cc >> chunk_shift)) & (cc <= rr), 1.0, 0.0).astype(BF16)
    LW = jnp.concatenate(
        [_split_dot_left(tri, lw[r0:r0 + span], 2) for r0 in range(0, tb, span)], axis=0)
    LT = jnp.concatenate(
        [jnp.broadcast_to(LW[c * C + C - 1:c * C + C, :], (C, w)) for c in range(tb // C)], axis=0)
    e_out = jnp.exp(-LW)
    e_end = jnp.exp(LT - LW)
    at = (-kk) * jnp.exp(LW - lw)
    rt = r * jnp.exp(LW)
    bt = b * e_out
    kt = k2 * e_out
    b_end = b * e_end
    k_end = k2 * e_end
    w_tot = jnp.exp(LT)

    lane_head = lax.broadcasted_iota(jnp.int32, (C, w), 1) >> HEAD_SHIFT
    t_idx = lax.broadcasted_iota(jnp.int32, (C, w), 0)
    j_idx = lax.broadcasted_iota(jnp.int32, (C, w), 1) & (HEAD_DIM - 1)
    strict = j_idx < t_idx
    incl = j_idx <= t_idx
    eye = jnp.where(j_idx == t_idx, 1.0, 0.0)
    sq_r = lax.broadcasted_iota(jnp.int32, (w, w), 0) >> HEAD_SHIFT
    sq_c = lax.broadcasted_iota(jnp.int32, (w, w), 1) >> HEAD_SHIFT
    block_diag = sq_r == sq_c
    stack = functools.partial(_stack_heads, lane_head=lane_head, n_heads=n_heads)

    chunks = range(tb // C)
    rows_of = [slice(c * C, (c + 1) * C) for c in chunks]
    bf = lambda t: t.astype(BF16)
    cat0 = lambda *ts: jnp.concatenate(ts, axis=0)
    cat1 = lambda *ts: jnp.concatenate(ts, axis=1)

    A = [_dot_nt(bf(cat0(at[sl], rt[sl])), cat0(stack(bf(bt[sl])), stack(bf(kt[sl]))))
         for sl in rows_of]
    a_ab = [jnp.where(strict, a[0:C, 0:w], 0.0) for a in A]
    a_ak = [jnp.where(strict, a[0:C, w:2 * w], 0.0) for a in A]
    a_rb = [jnp.where(incl, a[C:2 * C, 0:w], 0.0) for a in A]
    a_rk = [jnp.where(incl, a[C:2 * C, w:2 * w], 0.0) for a in A]

    pw = [_dot(bf(l), stack(bf(l))) for l in a_ab]
    tinv = [eye + l for l in a_ab]
    sz = 2
    while 2 * sz < C:
        both = [_dot(bf(cat0(t, p)), stack(bf(p))) for t, p in zip(tinv, pw)]
        tinv = [t + r[0:C] for t, r in zip(tinv, both)]
        pw = [r[C:2 * C] for r in both]
        sz *= 2
    tinv = [t + _dot(bf(t), stack(bf(p))) for t, p in zip(tinv, pw)]

    yo = [_dot(bf(cat0(ak, ark)), stack(bf(v[sl]))) for ak, ark, sl in zip(a_ak, a_rk, rows_of)]
    gu = [_dot(bf(t), cat1(stack(bf(at[sl])), stack(bf(y[0:C]))))
          for t, y, sl in zip(tinv, yo, rows_of)]
    pq = [_dot(bf(rb), cat1(stack(bf(x[:, 0:w])), stack(bf(x[:, w:2 * w]))))
          for rb, x in zip(a_rb, gu)]
    p_mat = [rt[sl] + x[:, 0:w] for sl, x in zip(rows_of, pq)]
    q_mat = [y[C:2 * C] + x[:, w:2 * w] for y, x in zip(yo, pq)]
    m_mat = [jnp.where(block_diag, _dot_tn(bf(x[:, 0:w]), bf(b_end[sl])), 0.0)
             for x, sl in zip(gu, rows_of)]
    n_mat = [jnp.where(block_diag,
                       _dot_tn(bf(cat0(x[:, w:2 * w], v[sl])), bf(cat0(b_end[sl], k_end[sl]))), 0.0)
             for x, sl in zip(gu, rows_of)]

    st = st_ref[...]
    outs = []
    for c in chunks:
        stb = bf(st)
        outs.append(_dot_nt(bf(p_mat[c]), stb) + q_mat[c])
        st = st * w_tot[c * C:c * C + 1, :] + _dot(stb, bf(m_mat[c])) + n_mat[c]
    st_ref[...] = st

    o = jnp.concatenate(outs, axis=0)
    inv_d = 1.0 / HEAD_DIM
    mean = _split_dot_right(o, ones, 1) * inv_d
    cen = o - mean
    var = _split_dot_right(cen * cen, ones, 1) * inv_d
    o = cen * lax.rsqrt(var + GN_EPS) * lng_ref[...] + lnb_ref[...]
    y_ref[...] = ((o + bonus) * g).astype(BF16)


def _rwkv(p, mu, w0, w2, a0, a2, g2, kkp, ka, rk, lng, lnb, *, batch, seq, tb):
    T, cols = p.shape
    w = w0.shape[1]
    nb = seq // tb
    vec = _resident((1, w))
    low = _resident((cols - 3 * w, w))
    return pl.pallas_call(
        _rwkv_kernel,
        out_shape=jax.ShapeDtypeStruct((T, w), BF16),
        grid=(batch, nb),
        in_specs=[
            pl.BlockSpec((tb, cols), lambda b, j: (b * nb + j, 0)),
            _resident((1, cols)), vec, low, vec, low, low, vec, vec, vec, vec, vec,
        ],
        out_specs=pl.BlockSpec((tb, w), lambda b, j: (b * nb + j, 0)),
        scratch_shapes=[pltpu.VMEM((SUBLANES, cols), F32), pltpu.VMEM((w, w), F32)],
        compiler_params=pltpu.CompilerParams(
            dimension_semantics=("parallel", "arbitrary"), vmem_limit_bytes=VMEM_LIMIT),
        name="rwkv",
    )(p, mu, w0, w2, a0, a2, g2, kkp, ka, rk, lng, lnb)


def _out_proj_kernel(ya_ref, yb_ref, yc_ref, x_ref, wa_ref, wb_ref, wc_ref, fg_ref, n2_ref,
                     wr_ref, br_ref, x1_ref, h2_ref, comb_ref, *, sub):
    sub_rows = [slice(r0, r0 + sub) for r0 in range(0, x_ref.shape[0], sub)]
    mixed = []
    for rows in sub_rows:
        ybn = (_rms(yb_ref[rows, :].astype(F32)) * fg_ref[...]).astype(BF16)
        mixed.append(_dot(ya_ref[rows, :], wa_ref[...]) + _dot(ybn, wb_ref[...])
                     + _dot(yc_ref[rows, :], wc_ref[...]))
    for rows, y in zip(sub_rows, mixed):
        x1 = x_ref[rows, :] + y
        x1_ref[rows, :] = x1
        h2 = _rms(x1) * n2_ref[...]
        h_hi = h2.astype(BF16)
        h2_ref[rows, :] = h_hi

        h_lo = (h2 - h_hi.astype(F32)).astype(BF16)
        hw = _dot(h_hi, wr_ref[...])
        logits = (hw[:, 0:LANES] + hw[:, LANES:2 * LANES] + _dot(h_lo, wr_ref[:, 0:LANES])
                  + br_ref[...])
        comb_ref[rows, :] = _route(logits)


def _route(logits):
    lane = lax.broadcasted_iota(jnp.int32, logits.shape, 1)
    lane_f = lane.astype(F32)
    far = float(LANES)

    def first_argmax(vals, vmax):
        return jnp.min(jnp.where(vals == vmax, lane_f, far), axis=-1, keepdims=True)

    gl = jnp.where(lane < N_GROUPS, logits, NEG_BIG)
    gmax = jnp.max(gl, axis=-1, keepdims=True)
    gidx = first_argmax(gl, gmax)
    g_p = 1.0 / jnp.sum(jnp.exp(gl - gmax), axis=-1, keepdims=True)

    e_group = ((lane - N_GROUPS) >> 2).astype(F32)
    el = jnp.where((lane >= N_GROUPS) & (e_group == gidx), logits, NEG_BIG)
    emax = jnp.max(el, axis=-1, keepdims=True)
    esum = jnp.sum(jnp.exp(el - emax), axis=-1, keepdims=True)
    i1 = first_argmax(el, emax)
    el2 = jnp.where(lane_f == i1, NEG_BIG, el)
    emax2 = jnp.max(el2, axis=-1, keepdims=True)
    i2 = first_argmax(el2, emax2)
    p1 = 1.0 / esum
    p2 = jnp.exp(emax2 - emax) / esum
    den = p1 + p2
    return (jnp.where(lane_f == i1, g_p * (p1 / den), 0.0)
            + jnp.where(lane_f == i2, g_p * (p2 / den), 0.0)
            + jnp.where(lane_f == gidx, 1.0, 0.0))


def _out_proj(ya, yb, yc, x2, wa, wb, wc, fg, n2, wr, br, *, tm):
    T, D = x2.shape
    row = lambda i: (i, 0)
    return pl.pallas_call(
        functools.partial(_out_proj_kernel, sub=tm // 2),
        out_shape=(jax.ShapeDtypeStruct((T, D), F32), jax.ShapeDtypeStruct((T, D), BF16),
                   jax.ShapeDtypeStruct((T, LANES), F32)),
        grid=(T // tm,),
        in_specs=[
            pl.BlockSpec((tm, ya.shape[1]), row),
            pl.BlockSpec((tm, yb.shape[1]), row),
            pl.BlockSpec((tm, yc.shape[1]), row),
            pl.BlockSpec((tm, D), row),
            _resident(wa.shape), _resident(wb.shape), _resident(wc.shape),
            _resident(fg.shape), _resident(n2.shape), _resident(wr.shape), _resident(br.shape),
        ],
        out_specs=(pl.BlockSpec((tm, D), row), pl.BlockSpec((tm, D), row),
                   pl.BlockSpec((tm, LANES), row)),
        compiler_params=pltpu.CompilerParams(
            dimension_semantics=("parallel",), vmem_limit_bytes=VMEM_LIMIT),
        name="out_proj",
    )(ya, yb, yc, x2, wa, wb, wc, fg, n2, wr, br)


def _moe_kernel(h_ref, route_ref, x_ref, wg_ref, wu_ref, wd_ref, o_ref, hs_ref, rs_ref, ys_ref):
    tm = h_ref.shape[0]
    n_groups = wd_ref.shape[0]
    route = route_ref[...]
    lane = lax.broadcasted_iota(jnp.int32, (tm, LANES), 1)
    onehot = jnp.where(lane < n_groups, route, 0.0)
    rr = lax.broadcasted_iota(jnp.int32, (tm, tm), 0)
    cc = lax.broadcasted_iota(jnp.int32, (tm, tm), 1)
    tri = jnp.where(cc <= rr, 1.0, 0.0).astype(BF16)
    cum = _dot(tri, onehot.astype(BF16))
    tot = cum[tm - 1:tm, :]
    start = jnp.zeros_like(tot)
    for d in range(1, n_groups):
        start = start + jnp.where(lane[0:1] >= d, pltpu.roll(tot, d, axis=1), 0.0)
    end = start + tot
    pos = jnp.sum(jnp.where(onehot > 0.0, start + cum - 1.0, 0.0), axis=-1, keepdims=True)
    perm_t = jnp.where(pos == cc.astype(F32), 1.0, 0.0).astype(BF16)
    pos_row = jnp.broadcast_to(pos, (tm, LANES)).T[0:1, :]
    perm = jnp.where(rr.astype(F32) == pos_row, 1.0, 0.0).astype(BF16)

    hs_ref[...] = _dot(perm, h_ref[...]).astype(BF16)
    rem = route
    sorted_route = None
    for _ in range(2):
        hi = rem.astype(BF16)
        t = _dot(perm, hi)
        sorted_route = t if sorted_route is None else sorted_route + t
        rem = rem - hi.astype(F32)
    rs_ref[...] = sorted_route
    ys_ref[...] = jnp.zeros_like(ys_ref)

    for g in range(n_groups):
        g_start = start[0, g]
        g_end = end[0, g]
        for r0 in range(0, tm, MOE_BLOCK):

            @pl.when((g_start < r0 + MOE_BLOCK) & (g_end > r0))
            def _(g=g, r0=r0):
                rows = slice(r0, r0 + MOE_BLOCK)
                hb = hs_ref[rows, :]
                rw = rs_ref[rows, :]
                hids = []
                for e in range(g * EXPERTS_PER_GROUP, (g + 1) * EXPERTS_PER_GROUP):
                    gate = _dot(hb, wg_ref[e])
                    hid = gate * jax.nn.sigmoid(gate) * _dot(hb, wu_ref[e])
                    hids.append((hid * rw[:, N_GROUPS + e:N_GROUPS + e + 1]).astype(BF16))
                ys_ref[rows, :] += _dot(jnp.concatenate(hids, axis=1), wd_ref[g])

    o_ref[...] = x_ref[...] + _dot(perm_t, ys_ref[...].astype(BF16))


def _moe(h2, route, x1, wg, wu, wd, *, tm):
    T, D = x1.shape
    row = lambda i: (i, 0)
    return pl.pallas_call(
        _moe_kernel,
        out_shape=jax.ShapeDtypeStruct((T, D), F32),
        grid=(T // tm,),
        in_specs=[
            pl.BlockSpec((tm, D), row),
            pl.BlockSpec((tm, LANES), row),
            pl.BlockSpec((tm, D), row),
            _resident(wg.shape),
            _resident(wu.shape),
            _resident(wd.shape),
        ],
        out_specs=pl.BlockSpec((tm, D), row),
        scratch_shapes=[pltpu.VMEM((tm, D), BF16), pltpu.VMEM((tm, LANES), F32),
                        pltpu.VMEM((tm, D), F32)],
        compiler_params=pltpu.CompilerParams(
            dimension_semantics=("parallel",), vmem_limit_bytes=VMEM_LIMIT),
        name="moe",
    )(h2, route, x1, wg, wu, wd)


def _block_diag(wh):
    n_heads, d, _ = wh.shape
    eye = jnp.eye(n_heads, dtype=wh.dtype)
    return jnp.einsum("hij,hg->higj", wh, eye).reshape(n_heads * d, n_heads * d)


def _row(vec):
    return vec.reshape(1, -1).astype(F32)


def _pad_rows(mat, start, total):
    return jnp.zeros((total, mat.shape[1]), mat.dtype).at[start:start + mat.shape[0]].set(mat)


def kernel(x, norm1_g, w_in, conv_w, conv_b, lru_wa, lru_ba, lru_wx, lru_bx, lru_lambda,
           lru_norm_g, fox_fb, fox_qnorm_g, fox_knorm_g, fox_norm_g, rwkv_mu, rwkv_w0,
           rwkv_w2, rwkv_a0, rwkv_a2, rwkv_g2, rwkv_kk, rwkv_ka, rwkv_rk, rwkv_ln_g,
           rwkv_ln_b, w_out, norm2_g, router_gw, router_gb, router_ew, router_eb,
           exp_w_gate, exp_w_up, exp_w_down):
    batch, seq, d_model = x.shape
    depth = w_in.shape[0]
    lru_w = conv_w.shape[2]
    fox_heads = fox_fb.shape[1]
    fox_w = fox_heads * HEAD_DIM
    rwkv_w = rwkv_w0.shape[1]
    rwkv_cols = rwkv_mu.shape[1]
    d_rank, a_rank, g_rank = rwkv_w2.shape[1], rwkv_a2.shape[1], rwkv_g2.shape[1]
    low = d_rank + a_rank + g_rank
    n_exp, _, d_exp = exp_w_gate.shape[1:]
    assert low == LANES and rwkv_cols == 3 * rwkv_w + low
    assert FOX_BIAS_PIECES * fox_heads <= LANES and n_exp == N_EXPERTS
    assert seq % TOKEN_TILE == 0 and seq % FOX_BLOCK == 0 and seq % RWKV_BLOCK == 0
    assert RWKV_BLOCK % RWKV_CUM_ROWS == 0 and RWKV_CUM_ROWS % CHUNK == 0
    assert (TOKEN_TILE // IN_PROJ_SUBTILES) % SUBLANES == 0 and TOKEN_TILE % MOE_BLOCK == 0

    o_fox = 2 * lru_w
    o_fl = o_fox + 3 * fox_w
    o_rwkv = o_fl + fox_heads
    seg = (0, o_fox, o_fox + fox_w, o_fox + 2 * fox_w, o_fl, o_fl + rwkv_cols,
           o_fl + rwkv_cols + LANES)

    x2 = x.reshape(batch * seq, d_model)
    for l in range(depth):
        wl = w_in[l]
        n_fl = FOX_BIAS_PIECES * fox_heads
        w_perm = jnp.concatenate(
            [wl[:, 0:o_fl], wl[:, o_rwkv:o_rwkv + rwkv_cols],
             jnp.repeat(wl[:, o_fl:o_rwkv], FOX_BIAS_PIECES, axis=1),
             jnp.zeros((d_model, LANES - n_fl), wl.dtype)], axis=1).astype(BF16)
        qg = _row(jnp.tile(fox_qnorm_g[l], fox_heads) * (HEAD_DIM ** -0.5 * LOG2E))
        kg = _row(jnp.tile(fox_knorm_g[l], fox_heads))
        fb = _row(jnp.pad(jnp.repeat(fox_fb[l], FOX_BIAS_PIECES), (0, LANES - n_fl)))
        wg = jnp.concatenate([_block_diag(lru_wa[l]), _block_diag(lru_wx[l])], axis=1).astype(BF16)
        bg = _row(jnp.concatenate([lru_ba[l], lru_bx[l]]))
        lru_params = (conv_w[l], _row(conv_b[l]), wg, bg, _row(lru_lambda[l]), _row(lru_norm_g[l]))
        ya, q, k_aug, v, p = _in_proj(x2, _row(norm1_g[l]), w_perm, qg, kg, fb, lru_params,
                                     seq=seq, tm=TOKEN_TILE, seg=seg)

        yb = _fox(q, k_aug, v, batch=batch, seq=seq, tq=FOX_BLOCK)

        yc = _rwkv(p, _row(rwkv_mu[l]), _row(rwkv_w0[l]),
                   _pad_rows(rwkv_w2[l], 0, low).astype(BF16), _row(rwkv_a0[l]),
                   _pad_rows(rwkv_a2[l], d_rank, low).astype(BF16),
                   _pad_rows(rwkv_g2[l], d_rank + a_rank, low).astype(BF16),
                   _row(rwkv_kk[l]), _row(rwkv_ka[l]), _row(rwkv_rk[l]),
                   _row(rwkv_ln_g[l]), _row(rwkv_ln_b[l]), batch=batch, seq=seq, tb=RWKV_BLOCK)

        wo = w_out[l].astype(BF16)
        wr = jnp.concatenate(
            [router_gw[l], router_ew[l],
             jnp.zeros((d_model, LANES - N_GROUPS - n_exp), F32)], axis=1)
        wr_hi = wr.astype(BF16)
        wr = jnp.concatenate([wr_hi, (wr - wr_hi.astype(F32)).astype(BF16)], axis=1)
        br = _row(jnp.pad(jnp.concatenate([router_gb[l], router_eb[l]]),
                          (0, LANES - N_GROUPS - n_exp)))
        x1, h2, comb = _out_proj(
            ya, yb, yc, x2, wo[0:lru_w], wo[lru_w:lru_w + fox_w], wo[lru_w + fox_w:],
            _row(fox_norm_g[l]), _row(norm2_g[l]), wr, br, tm=TOKEN_TILE)

        wd = exp_w_down[l].astype(BF16).reshape(N_GROUPS, EXPERTS_PER_GROUP * d_exp, d_model)
        x2 = _moe(h2, comb, x1, exp_w_gate[l].astype(BF16), exp_w_up[l].astype(BF16), wd,
                  tm=TOKEN_TILE)
    return x2.reshape(batch, seq, d_model)
```

```python
import functools

import jax
import jax.numpy as jnp
from jax import lax
from jax.experimental import pallas as pl
from jax.experimental.pallas import tpu as pltpu

F32 = jnp.float32
BF16 = jnp.bfloat16

NORM_EPS = 1e-6
GN_EPS = 64e-5
LRU_C = 8.0
HEAD_DIM = 64
HEAD_SHIFT = 6
LANES = 128
SUBLANES = 8
CHUNK = 64
TOKEN_TILE = 512
FOX_BLOCK = 512
RWKV_BLOCK = 512
IN_PROJ_SUBTILES = 2
RWKV_CUM_ROWS = 256
CONV_WIDTH = 4
N_GROUPS = 4
EXPERTS_PER_GROUP = 4
N_EXPERTS = N_GROUPS * EXPERTS_PER_GROUP
MOE_BLOCK = 128
NEG_BIG = -1e30
LOG2E = 1.4426950408889634
FOX_BIAS_PIECES = 3
FOX_DEN_ROWS = 16
VMEM_LIMIT = 56 * 1024 * 1024


def _dot(a, b):
    return jnp.dot(a, b, preferred_element_type=F32)


def _dot_nt(a, b):
    return lax.dot_general(a, b, (((1,), (1,)), ((), ())), preferred_element_type=F32)


def _dot_tn(a, b):
    return lax.dot_general(a, b, (((0,), (0,)), ((), ())), preferred_element_type=F32)


def _split_dot_right(x, ones, parts):
    acc = None
    rem = x
    for _ in range(parts):
        hi = rem.astype(BF16)
        t = _dot(hi, ones)
        acc = t if acc is None else acc + t
        rem = rem - hi.astype(F32)
    return acc


def _split_dot_left(ones, x, parts):
    acc = None
    rem = x
    for _ in range(parts):
        hi = rem.astype(BF16)
        t = _dot(ones, hi)
        acc = t if acc is None else acc + t
        rem = rem - hi.astype(F32)
    return acc


def _head_ones(n):
    r = lax.broadcasted_iota(jnp.int32, (n, n), 0) >> HEAD_SHIFT
    c = lax.broadcasted_iota(jnp.int32, (n, n), 1) >> HEAD_SHIFT
    return jnp.where(r == c, 1.0, 0.0).astype(BF16)


def _softplus(z):
    return jnp.maximum(z, 0.0) + jnp.log(1.0 + jnp.exp(-jnp.abs(z)))


def _rms(x, eps=NORM_EPS):
    return x * lax.rsqrt(jnp.mean(x * x, axis=-1, keepdims=True) + eps)


def _resident(shape):
    zeros = (0,) * len(shape)
    return pl.BlockSpec(shape, lambda *_: zeros, pipeline_mode=pl.Buffered(1))


def _in_proj_kernel(x_ref, g_ref, w_ref, qg_ref, kg_ref, fb_ref,
                    cw_ref, cb_ref, wg_ref, bg_ref, lam_ref, ng_ref,
                    ya_ref, q_ref, k_ref, v_ref, p_ref,
                    carry_ref, tail_ref, hst_ref, buf_ref,
                    *, tiles_per_seq, seg, sub):
    i = pl.program_id(0)

    @pl.when(i % tiles_per_seq == 0)
    def _():
        carry_ref[...] = jnp.zeros_like(carry_ref)
        tail_ref[...] = jnp.zeros_like(tail_ref)
        hst_ref[...] = jnp.zeros_like(hst_ref)

    fox = seg[2] - seg[1]
    n_heads = fox // HEAD_DIM
    ones = _head_ones(fox)
    inv_d = 1.0 / HEAD_DIM
    rr = lax.broadcasted_iota(jnp.int32, (sub, sub), 0)
    cc = lax.broadcasted_iota(jnp.int32, (sub, sub), 1)
    tri = jnp.where(cc <= rr, 1.0, 0.0).astype(BF16)
    lane = lax.broadcasted_iota(jnp.int32, (sub, LANES), 1)
    feat = lane < HEAD_DIM
    bias_lanes = (lane >= HEAD_DIM) & (lane < HEAD_DIM + FOX_BIAS_PIECES)
    q_bias = jnp.where(bias_lanes, 1.0, 0.0)
    piece_id = lax.rem(lane, FOX_BIAS_PIECES)
    carry = carry_ref[0:1, :]
    tail = tail_ref[...]
    h_lru = hst_ref[0:1, :]
    lru_w = (seg[1] - seg[0]) // 2

    def head_norm(t):
        return t * lax.rsqrt(_dot((t * t).astype(BF16), ones) * inv_d + NORM_EPS)

    sub_rows = [slice(r0, r0 + sub) for r0 in range(0, x_ref.shape[0], sub)]
    state = {"carry": carry, "tail": tail, "h": h_lru}

    def project_steps(rows, out):
        hb = []

        def first():
            hb.append((_rms(x_ref[rows, :]) * g_ref[...]).astype(BF16))
            out[0] = _dot(hb[0], w_ref[:, seg[0]:seg[1]])

        def segment(n):
            out[n] = _dot(hb[0], w_ref[:, seg[n]:seg[n + 1]])

        def rwkv_cols():
            p_ref[rows, :] = _dot(hb[0], w_ref[:, seg[4]:seg[5]])

        return [first, functools.partial(segment, 5), functools.partial(segment, 1),
                functools.partial(segment, 2), functools.partial(segment, 3), rwkv_cols]

    def finish_steps(idx, rows, out):
        local = {}

        def lru():
            u = out[0]
            ya_ref[rows, :], state["tail"], state["h"] = _lru_block(
                u[:, 0:lru_w], u[:, lru_w:2 * lru_w], state["tail"], state["h"], buf_ref.at[idx],
                cw_ref, cb_ref, wg_ref, bg_ref, lam_ref, ng_ref)

        def forget():
            lf = -_softplus(-(out[5] + fb_ref[...]))
            c = _split_dot_left(tri, lf, 3) + state["carry"]
            state["carry"] = c[sub - 1:sub, :]
            rem = c * (-LOG2E)
            pieces = None
            for piece in range(FOX_BIAS_PIECES):
                hi = rem.astype(BF16).astype(F32)
                pieces = hi if pieces is None else jnp.where(piece_id == piece, hi, pieces)
                rem = rem - hi
            local["pieces"] = pieces

        def spread(t, h):
            th = t[:, (h // 2) * LANES:(h // 2 + 1) * LANES]
            return pltpu.roll(th, HEAD_DIM, axis=1) if h % 2 else th

        def queries():
            q = head_norm(out[1]) * qg_ref[...]
            for h in range(n_heads):
                q_ref[h * LANES:(h + 1) * LANES, rows] = (
                    jnp.where(feat, spread(q, h), q_bias).T.astype(BF16))

        def keys():
            k = head_norm(out[2]) * kg_ref[...]
            for h in range(n_heads):
                k_bias = pltpu.roll(local["pieces"], HEAD_DIM - FOX_BIAS_PIECES * h, axis=1)
                k_ref[rows, h * LANES:(h + 1) * LANES] = jnp.where(
                    feat, spread(k, h), jnp.where(bias_lanes, k_bias, 0.0)).astype(BF16)

        def values():
            v_ref[:, rows] = out[3].T.astype(BF16)

        return [lru, forget, queries, keys, values]

    outs = [dict() for _ in sub_rows]
    for step in project_steps(sub_rows[0], outs[0]):
        step()
    for idx, rows in enumerate(sub_rows):
        ahead = project_steps(sub_rows[idx + 1], outs[idx + 1]) if idx + 1 < len(sub_rows) else []
        finish = finish_steps(idx, rows, outs[idx])
        for n in range(max(len(ahead), len(finish))):
            if n < len(ahead):
                ahead[n]()
            if n < len(finish):
                finish[n]()

    carry_ref[...] = jnp.broadcast_to(state["carry"], carry_ref.shape)
    tail_ref[...] = state["tail"]
    hst_ref[...] = jnp.broadcast_to(state["h"], hst_ref.shape)


def _in_proj(x2, g, w, qg, kg, fb, lru_params, *, seq, tm, seg):
    T, D = x2.shape
    n_out = w.shape[1]
    widths = [seg[j + 1] - seg[j] for j in range(6)]
    row = lambda i: (i, 0)
    col = lambda i: (0, i)
    outs = [
        ((T, widths[0] // 2), (tm, widths[0] // 2), row, BF16),
        ((2 * widths[1], T), (2 * widths[1], tm), col, BF16),
        ((T, 2 * widths[2]), (tm, 2 * widths[2]), row, BF16),
        ((widths[3], T), (widths[3], tm), col, BF16),
        ((T, widths[4]), (tm, widths[4]), row, F32),
    ]
    return pl.pallas_call(
        functools.partial(_in_proj_kernel, tiles_per_seq=seq // tm, seg=seg,
                          sub=tm // IN_PROJ_SUBTILES),
        out_shape=tuple(jax.ShapeDtypeStruct(shape, dt) for shape, _, _, dt in outs),
        grid=(T // tm,),
        in_specs=[
            pl.BlockSpec((tm, D), row),
            _resident((1, D)),
            _resident((D, n_out)),
            _resident((1, widths[1])),
            _resident((1, widths[2])),
            _resident((1, widths[5])),
        ] + [_resident(t.shape) for t in lru_params],
        out_specs=tuple(pl.BlockSpec(blk, imap) for _, blk, imap, _ in outs),
        scratch_shapes=[pltpu.VMEM((SUBLANES, widths[5]), F32),
                        pltpu.VMEM((SUBLANES, widths[0] // 2), F32),
                        pltpu.VMEM((SUBLANES, widths[0] // 2), F32),
                        pltpu.VMEM((IN_PROJ_SUBTILES, tm // IN_PROJ_SUBTILES + SUBLANES,
                                    widths[0] // 2), F32)],
        compiler_params=pltpu.CompilerParams(
            dimension_semantics=("arbitrary",), vmem_limit_bytes=VMEM_LIMIT),
        name="in_proj",
    )(x2, g, w, qg, kg, fb, *lru_params)


def _shift_rows(x, d, fill, row):
    return jnp.where(row >= d, pltpu.roll(x, d, axis=0), fill)


def _lru_block(xa, ga, tail, h_prev, buf_ref, cw_ref, cb_ref, wg_ref, bg_ref, lam_ref, ng_ref):
    tb, w = xa.shape
    pad = SUBLANES
    buf_ref[0:pad, :] = tail
    buf_ref[pad:pad + tb, :] = xa
    xc = cb_ref[...] + cw_ref[CONV_WIDTH - 1:CONV_WIDTH, :] * xa
    for d in range(1, CONV_WIDTH):
        xc = xc + cw_ref[CONV_WIDTH - 1 - d:CONV_WIDTH - d, :] * buf_ref[pad - d:pad - d + tb, :]

    gates = _dot(xc.astype(BF16), wg_ref[...]) + bg_ref[...]
    r = jax.nn.sigmoid(gates[:, 0:w])
    i = jax.nn.sigmoid(gates[:, w:2 * w])
    log_a = (-LRU_C) * r * _softplus(-lam_ref[...])
    a = jnp.exp(log_a)
    b = jnp.sqrt(1.0 - a * a) * (i * xc)

    row = lax.broadcasted_iota(jnp.int32, (tb, w), 0)
    d = 1
    while d < tb:
        a_sh = _shift_rows(a, d, 1.0, row)
        b_sh = _shift_rows(b, d, 0.0, row)
        b = a * b_sh + b
        a = a * a_sh
        d *= 2
    h = b + a * h_prev
    y = jax.nn.gelu(ga) * h
    return (_rms(y) * ng_ref[...]).astype(BF16), xa[tb - pad:tb, :], h[tb - 1:tb, :]


def _fox_kernel(q_ref, k_ref, v_ref, o_ref, s_buf, mx_buf, p_buf, m_ref, acc_ref, fin_ref, *, tq):
    n_heads = k_ref.shape[1] // LANES
    nq = k_ref.shape[0] // tq
    half = tq // 2
    causal_top = (lax.broadcasted_iota(jnp.int32, (half, tq), 0)
                  <= lax.broadcasted_iota(jnp.int32, (half, tq), 1))
    causal_bot = (lax.broadcasted_iota(jnp.int32, (half, half), 0)
                  <= lax.broadcasted_iota(jnp.int32, (half, half), 1))
    items = [(i, j) for i in range(nq) for j in range(i + 1)]

    def scores(t, u):
        i, j = items[t]
        slot = t % 2
        hl = slice(u * LANES, (u + 1) * LANES)
        k0, q0 = j * tq, i * tq
        if i != j:
            s = _dot(k_ref[k0:k0 + tq, hl], q_ref[hl, q0:q0 + tq])
            s_buf[slot, u] = s
            mx_buf[slot, u] = jnp.max(s, axis=0, keepdims=True)
        else:
            top = jnp.where(causal_top, _dot(k_ref[k0:k0 + half, hl], q_ref[hl, q0:q0 + tq]), NEG_BIG)
            bot = jnp.where(causal_bot,
                            _dot(k_ref[k0 + half:k0 + tq, hl], q_ref[hl, q0 + half:q0 + tq]), NEG_BIG)
            s_buf[slot, u, 0:half, :] = top
            s_buf[slot, u, half:tq, half:tq] = bot
            mx_top = jnp.max(top, axis=0, keepdims=True)
            mx_buf[slot, u] = jnp.concatenate(
                [mx_top[:, 0:half],
                 jnp.maximum(mx_top[:, half:tq], jnp.max(bot, axis=0, keepdims=True))], axis=1)

    def softmax_numerators(t, u):
        i, j = items[t]
        slot = t % 2
        m_new = mx_buf[slot, u] if j == 0 else jnp.maximum(m_ref[u], mx_buf[slot, u])
        if i != j:
            p_buf[u] = jnp.exp2(s_buf[slot, u] - m_new).astype(BF16)
        else:
            p_buf[u, 0:half, :] = jnp.exp2(s_buf[slot, u, 0:half, :] - m_new).astype(BF16)
            p_buf[u, half:tq, half:tq] = jnp.exp2(
                s_buf[slot, u, half:tq, half:tq] - m_new[:, half:tq]).astype(BF16)
        return m_new

    def accumulate(t, u, m_new):
        i, j = items[t]
        vt = jnp.concatenate([v_ref[u * HEAD_DIM:(u + 1) * HEAD_DIM, j * tq:(j + 1) * tq],
                              jnp.ones((FOX_DEN_ROWS, tq), BF16)], axis=0)
        if i != j:
            acc = _dot(vt, p_buf[u])
        else:
            late = _dot(vt[:, half:tq], p_buf[u, half:tq, half:tq])
            acc = _dot(vt[:, 0:half], p_buf[u, 0:half, :]) + jnp.concatenate(
                [jnp.zeros((vt.shape[0], half), F32), late], axis=1)
        if j > 0:
            acc = jnp.exp2(m_ref[u] - m_new) * acc_ref[u] + acc
        if j < i:
            m_ref[u] = m_new
            acc_ref[u] = acc
        else:
            fin_ref[u * HEAD_DIM:(u + 1) * HEAD_DIM, :] = acc[0:HEAD_DIM] / acc[HEAD_DIM:HEAD_DIM + 1]
            if u == n_heads - 1:
                o_ref[i * tq:(i + 1) * tq, :] = fin_ref[...].T.astype(BF16)

    for u in range(n_heads):
        scores(0, u)
    for t in range(len(items)):
        for u in range(n_heads):
            m_new = softmax_numerators(t, u)
            if t + 1 < len(items):
                scores(t + 1, u)
            accumulate(t, u, m_new)


def _fox(q_aug, k_aug, v_t, *, batch, seq, tq):
    T, wide = k_aug.shape
    heads_per_step = 2
    blk = heads_per_step * LANES
    out_w = heads_per_step * HEAD_DIM
    steps = wide // blk
    return pl.pallas_call(
        functools.partial(_fox_kernel, tq=tq),
        out_shape=jax.ShapeDtypeStruct((T, steps * out_w), BF16),
        grid=(batch, steps),
        in_specs=[
            pl.BlockSpec((blk, seq), lambda b, h: (h, b)),
            pl.BlockSpec((seq, blk), lambda b, h: (b, h)),
            pl.BlockSpec((out_w, seq), lambda b, h: (h, b)),
        ],
        out_specs=pl.BlockSpec((seq, out_w), lambda b, h: (b, h)),
        scratch_shapes=[
            pltpu.VMEM((2, heads_per_step, tq, tq), F32),
            pltpu.VMEM((2, heads_per_step, 1, tq), F32),
            pltpu.VMEM((heads_per_step, tq, tq), BF16),
            pltpu.VMEM((heads_per_step, 1, tq), F32),
            pltpu.VMEM((heads_per_step, HEAD_DIM + FOX_DEN_ROWS, tq), F32),
            pltpu.VMEM((out_w, tq), F32),
        ],
        compiler_params=pltpu.CompilerParams(
            dimension_semantics=("parallel", "parallel"), vmem_limit_bytes=VMEM_LIMIT),
        name="fox",
    )(q_aug, k_aug, v_t)


def _stack_heads(x, lane_head, n_heads):
    zero = jnp.zeros_like(x)
    return jnp.concatenate([jnp.where(lane_head == h, x, zero) for h in range(n_heads)], axis=0)


def _rwkv_kernel(p_ref, mu_ref, w0_ref, w2_ref, a0_ref, a2_ref, g2_ref, kk_ref, ka_ref,
                 rk_ref, lng_ref, lnb_ref, y_ref, prev_ref, st_ref):
    j = pl.program_id(1)
    tb, cols = p_ref.shape
    w = y_ref.shape[1]
    n_heads = w // HEAD_DIM
    C = CHUNK
    chunk_shift = CHUNK.bit_length() - 1

    @pl.when(j == 0)
    def _():
        prev_ref[...] = jnp.zeros_like(prev_ref)
        st_ref[...] = jnp.zeros_like(st_ref)

    p = p_ref[...]
    row = lax.broadcasted_iota(jnp.int32, (tb, cols), 0)
    prev = jnp.where(row == 0, prev_ref[0:1, :], pltpu.roll(p, 1, axis=0))
    prev_ref[...] = jnp.broadcast_to(p[tb - 1:tb, :], prev_ref.shape)
    ps = p + (prev - p) * mu_ref[...]
    r = ps[:, 0:w]
    k = ps[:, w:2 * w]
    v = ps[:, 2 * w:3 * w]
    lr = ps[:, 3 * w:cols]

    wpre = w0_ref[...] + _dot(jnp.tanh(lr).astype(BF16), w2_ref[...])
    lw = -jnp.exp(-_softplus(-wpre) - 0.5)
    a = jax.nn.sigmoid(a0_ref[...] + _dot(lr.astype(BF16), a2_ref[...]))
    g = _dot(jax.nn.sigmoid(lr).astype(BF16), g2_ref[...])

    ones = _head_ones(w)
    kk = k * kk_ref[...]
    kk = kk * lax.rsqrt(jnp.maximum(_split_dot_right(kk * kk, ones, 1), 1e-24))
    k2 = k * (1.0 + (a - 1.0) * ka_ref[...])
    b = kk * a
    bonus = _split_dot_right(r * k2 * rk_ref[...], ones, 1) * v

    span = min(RWKV_CUM_ROWS, tb)
    rr = lax.broadcasted_iota(jnp.int32, (span, span), 0)
    cc = lax.broadcasted_iota(jnp.int32, (span, span), 1)
    tri = jnp.where(((rr >> chunk_shift) == (cc >> chunk_shift)) & (cc <= rr), 1.0, 0.0).astype(BF16)
    LW = jnp.concatenate(
        [_split_dot_left(tri, lw[r0:r0 + span], 2) for r0 in range(0, tb, span)], axis=0)
    LT = jnp.concatenate(
        [jnp.broadcast_to(LW[c * C + C - 1:c * C + C, :], (C, w)) for c in range(tb // C)], axis=0)
    e_out = jnp.exp(-LW)
    e_end = jnp.exp(LT - LW)
    at = (-kk) * jnp.exp(LW - lw)
    rt = r * jnp.exp(LW)
    bt = b * e_out
    kt = k2 * e_out
    b_end = b * e_end
    k_end = k2 * e_end
    w_tot = jnp.exp(LT)

    lane_head = lax.broadcasted_iota(jnp.int32, (C, w), 1) >> HEAD_SHIFT
    t_idx = lax.broadcasted_iota(jnp.int32, (C, w), 0)
    j_idx = lax.broadcasted_iota(jnp.int32, (C, w), 1) & (HEAD_DIM - 1)
    strict = j_idx < t_idx
    incl = j_idx <= t_idx
    eye = jnp.where(j_idx == t_idx, 1.0, 0.0)
    sq_r = lax.broadcasted_iota(jnp.int32, (w, w), 0) >> HEAD_SHIFT
    sq_c = lax.broadcasted_iota(jnp.int32, (w, w), 1) >> HEAD_SHIFT
    block_diag = sq_r == sq_c
    stack = functools.partial(_stack_heads, lane_head=lane_head, n_heads=n_heads)

    chunks = range(tb // C)
    rows_of = [slice(c * C, (c + 1) * C) for c in chunks]
    bf = lambda t: t.astype(BF16)
    cat0 = lambda *ts: jnp.concatenate(ts, axis=0)
    cat1 = lambda *ts: jnp.concatenate(ts, axis=1)

    A = [_dot_nt(bf(cat0(at[sl], rt[sl])), cat0(stack(bf(bt[sl])), stack(bf(kt[sl]))))
         for sl in rows_of]
    a_ab = [jnp.where(strict, a[0:C, 0:w], 0.0) for a in A]
    a_ak = [jnp.where(strict, a[0:C, w:2 * w], 0.0) for a in A]
    a_rb = [jnp.where(incl, a[C:2 * C, 0:w], 0.0) for a in A]
    a_rk = [jnp.where(incl, a[C:2 * C, w:2 * w], 0.0) for a in A]

    pw = [_dot(bf(l), stack(bf(l))) for l in a_ab]
    tinv = [eye + l for l in a_ab]
    sz = 2
    while 2 * sz < C:
        both = [_dot(bf(cat0(t, p)), stack(bf(p))) for t, p in zip(tinv, pw)]
        tinv = [t + r[0:C] for t, r in zip(tinv, both)]
        pw = [r[C:2 * C] for r in both]
        sz *= 2
    tinv = [t + _dot(bf(t), stack(bf(p))) for t, p in zip(tinv, pw)]

    yo = [_dot(bf(cat0(ak, ark)), stack(bf(v[sl]))) for ak, ark, sl in zip(a_ak, a_rk, rows_of)]
    gu = [_dot(bf(t), cat1(stack(bf(at[sl])), stack(bf(y[0:C]))))
          for t, y, sl in zip(tinv, yo, rows_of)]
    pq = [_dot(bf(rb), cat1(stack(bf(x[:, 0:w])), stack(bf(x[:, w:2 * w]))))
          for rb, x in zip(a_rb, gu)]
    p_mat = [rt[sl] + x[:, 0:w] for sl, x in zip(rows_of, pq)]
    q_mat = [y[C:2 * C] + x[:, w:2 * w] for y, x in zip(yo, pq)]
    m_mat = [jnp.where(block_diag, _dot_tn(bf(x[:, 0:w]), bf(b_end[sl])), 0.0)
             for x, sl in zip(gu, rows_of)]
    n_mat = [jnp.where(block_diag,
                       _dot_tn(bf(cat0(x[:, w:2 * w], v[sl])), bf(cat0(b_end[sl], k_end[sl]))), 0.0)
             for x, sl in zip(gu, rows_of)]

    st = st_ref[...]
    outs = []
    for c in chunks:
        stb = bf(st)
        outs.append(_dot_nt(bf(p_mat[c]), stb) + q_mat[c])
        st = st * w_tot[c * C:c * C + 1, :] + _dot(stb, bf(m_mat[c])) + n_mat[c]
    st_ref[...] = st

    o = jnp.concatenate(outs, axis=0)
    inv_d = 1.0 / HEAD_DIM
    mean = _split_dot_right(o, ones, 1) * inv_d
    cen = o - mean
    var = _split_dot_right(cen * cen, ones, 1) * inv_d
    o = cen * lax.rsqrt(var + GN_EPS) * lng_ref[...] + lnb_ref[...]
    y_ref[...] = ((o + bonus) * g).astype(BF16)


def _rwkv(p, mu, w0, w2, a0, a2, g2, kkp, ka, rk, lng, lnb, *, batch, seq, tb):
    T, cols = p.shape
    w = w0.shape[1]
    nb = seq // tb
    vec = _resident((1, w))
    low = _resident((cols - 3 * w, w))
    return pl.pallas_call(
        _rwkv_kernel,
        out_shape=jax.ShapeDtypeStruct((T, w), BF16),
        grid=(batch, nb),
        in_specs=[
            pl.BlockSpec((tb, cols), lambda b, j: (b * nb + j, 0)),
            _resident((1, cols)), vec, low, vec, low, low, vec, vec, vec, vec, vec,
        ],
        out_specs=pl.BlockSpec((tb, w), lambda b, j: (b * nb + j, 0)),
        scratch_shapes=[pltpu.VMEM((SUBLANES, cols), F32), pltpu.VMEM((w, w), F32)],
        compiler_params=pltpu.CompilerParams(
            dimension_semantics=("parallel", "arbitrary"), vmem_limit_bytes=VMEM_LIMIT),
        name="rwkv",
    )(p, mu, w0, w2, a0, a2, g2, kkp, ka, rk, lng, lnb)


def _out_proj_kernel(ya_ref, yb_ref, yc_ref, x_ref, wa_ref, wb_ref, wc_ref, fg_ref, n2_ref,
                     wr_ref, br_ref, x1_ref, h2_ref, comb_ref, *, sub):
    sub_rows = [slice(r0, r0 + sub) for r0 in range(0, x_ref.shape[0], sub)]
    mixed = []
    for rows in sub_rows:
        ybn = (_rms(yb_ref[rows, :].astype(F32)) * fg_ref[...]).astype(BF16)
        mixed.append(_dot(ya_ref[rows, :], wa_ref[...]) + _dot(ybn, wb_ref[...])
                     + _dot(yc_ref[rows, :], wc_ref[...]))
    for rows, y in zip(sub_rows, mixed):
        x1 = x_ref[rows, :] + y
        x1_ref[rows, :] = x1
        h2 = _rms(x1) * n2_ref[...]
        h_hi = h2.astype(BF16)
        h2_ref[rows, :] = h_hi

        h_lo = (h2 - h_hi.astype(F32)).astype(BF16)
        hw = _dot(h_hi, wr_ref[...])
        logits = (hw[:, 0:LANES] + hw[:, LANES:2 * LANES] + _dot(h_lo, wr_ref[:, 0:LANES])
                  + br_ref[...])
        comb_ref[rows, :] = _route(logits)


def _route(logits):
    lane = lax.broadcasted_iota(jnp.int32, logits.shape, 1)
    lane_f = lane.astype(F32)
    far = float(LANES)

    def first_argmax(vals, vmax):
        return jnp.min(jnp.where(vals == vmax, lane_f, far), axis=-1, keepdims=True)

    gl = jnp.where(lane < N_GROUPS, logits, NEG_BIG)
    gmax = jnp.max(gl, axis=-1, keepdims=True)
    gidx = first_argmax(gl, gmax)
    g_p = 1.0 / jnp.sum(jnp.exp(gl - gmax), axis=-1, keepdims=True)

    e_group = ((lane - N_GROUPS) >> 2).astype(F32)
    el = jnp.where((lane >= N_GROUPS) & (e_group == gidx), logits, NEG_BIG)
    emax = jnp.max(el, axis=-1, keepdims=True)
    esum = jnp.sum(jnp.exp(el - emax), axis=-1, keepdims=True)
    i1 = first_argmax(el, emax)
    el2 = jnp.where(lane_f == i1, NEG_BIG, el)
    emax2 = jnp.max(el2, axis=-1, keepdims=True)
    i2 = first_argmax(el2, emax2)
    p1 = 1.0 / esum
    p2 = jnp.exp(emax2 - emax) / esum
    den = p1 + p2
    return (jnp.where(lane_f == i1, g_p * (p1 / den), 0.0)
            + jnp.where(lane_f == i2, g_p * (p2 / den), 0.0)
            + jnp.where(lane_f == gidx, 1.0, 0.0))


def _out_proj(ya, yb, yc, x2, wa, wb, wc, fg, n2, wr, br, *, tm):
    T, D = x2.shape
    row = lambda i: (i, 0)
    return pl.pallas_call(
        functools.partial(_out_proj_kernel, sub=tm // 2),
        out_shape=(jax.ShapeDtypeStruct((T, D), F32), jax.ShapeDtypeStruct((T, D), BF16),
                   jax.ShapeDtypeStruct((T, LANES), F32)),
        grid=(T // tm,),
        in_specs=[
            pl.BlockSpec((tm, ya.shape[1]), row),
            pl.BlockSpec((tm, yb.shape[1]), row),
            pl.BlockSpec((tm, yc.shape[1]), row),
            pl.BlockSpec((tm, D), row),
            _resident(wa.shape), _resident(wb.shape), _resident(wc.shape),
            _resident(fg.shape), _resident(n2.shape), _resident(wr.shape), _resident(br.shape),
        ],
        out_specs=(pl.BlockSpec((tm, D), row), pl.BlockSpec((tm, D), row),
                   pl.BlockSpec((tm, LANES), row)),
        compiler_params=pltpu.CompilerParams(
            dimension_semantics=("parallel",), vmem_limit_bytes=VMEM_LIMIT),
        name="out_proj",
    )(ya, yb, yc, x2, wa, wb, wc, fg, n2, wr, br)


def _moe_kernel(h_ref, route_ref, x_ref, wg_ref, wu_ref, wd_ref, o_ref, hs_ref, rs_ref, ys_ref):
    tm = h_ref.shape[0]
    n_groups = wd_ref.shape[0]
    route = route_ref[...]
    lane = lax.broadcasted_iota(jnp.int32, (tm, LANES), 1)
    onehot = jnp.where(lane < n_groups, route, 0.0)
    rr = lax.broadcasted_iota(jnp.int32, (tm, tm), 0)
    cc = lax.broadcasted_iota(jnp.int32, (tm, tm), 1)
    tri = jnp.where(cc <= rr, 1.0, 0.0).astype(BF16)
    cum = _dot(tri, onehot.astype(BF16))
    tot = cum[tm - 1:tm, :]
    start = jnp.zeros_like(tot)
    for d in range(1, n_groups):
        start = start + jnp.where(lane[0:1] >= d, pltpu.roll(tot, d, axis=1), 0.0)
    end = start + tot
    pos = jnp.sum(jnp.where(onehot > 0.0, start + cum - 1.0, 0.0), axis=-1, keepdims=True)
    perm_t = jnp.where(pos == cc.astype(F32), 1.0, 0.0).astype(BF16)
    pos_row = jnp.broadcast_to(pos, (tm, LANES)).T[0:1, :]
    perm = jnp.where(rr.astype(F32) == pos_row, 1.0, 0.0).astype(BF16)

    hs_ref[...] = _dot(perm, h_ref[...]).astype(BF16)
    rem = route
    sorted_route = None
    for _ in range(2):
        hi = rem.astype(BF16)
        t = _dot(perm, hi)
        sorted_route = t if sorted_route is None else sorted_route + t
        rem = rem - hi.astype(F32)
    rs_ref[...] = sorted_route
    ys_ref[...] = jnp.zeros_like(ys_ref)

    for g in range(n_groups):
        g_start = start[0, g]
        g_end = end[0, g]
        for r0 in range(0, tm, MOE_BLOCK):

            @pl.when((g_start < r0 + MOE_BLOCK) & (g_end > r0))
            def _(g=g, r0=r0):
                rows = slice(r0, r0 + MOE_BLOCK)
                hb = hs_ref[rows, :]
                rw = rs_ref[rows, :]
                hids = []
                for e in range(g * EXPERTS_PER_GROUP, (g + 1) * EXPERTS_PER_GROUP):
                    gate = _dot(hb, wg_ref[e])
                    hid = gate * jax.nn.sigmoid(gate) * _dot(hb, wu_ref[e])
                    hids.append((hid * rw[:, N_GROUPS + e:N_GROUPS + e + 1]).astype(BF16))
                ys_ref[rows, :] += _dot(jnp.concatenate(hids, axis=1), wd_ref[g])

    o_ref[...] = x_ref[...] + _dot(perm_t, ys_ref[...].astype(BF16))


def _moe(h2, route, x1, wg, wu, wd, *, tm):
    T, D = x1.shape
    row = lambda i: (i, 0)
    return pl.pallas_call(
        _moe_kernel,
        out_shape=jax.ShapeDtypeStruct((T, D), F32),
        grid=(T // tm,),
        in_specs=[
            pl.BlockSpec((tm, D), row),
            pl.BlockSpec((tm, LANES), row),
            pl.BlockSpec((tm, D), row),
            _resident(wg.shape),
            _resident(wu.shape),
            _resident(wd.shape),
        ],
        out_specs=pl.BlockSpec((tm, D), row),
        scratch_shapes=[pltpu.VMEM((tm, D), BF16), pltpu.VMEM((tm, LANES), F32),
                        pltpu.VMEM((tm, D), F32)],
        compiler_params=pltpu.CompilerParams(
            dimension_semantics=("parallel",), vmem_limit_bytes=VMEM_LIMIT),
        name="moe",
    )(h2, route, x1, wg, wu, wd)


def _block_diag(wh):
    n_heads, d, _ = wh.shape
    eye = jnp.eye(n_heads, dtype=wh.dtype)
    return jnp.einsum("hij,hg->higj", wh, eye).reshape(n_heads * d, n_heads * d)


def _row(vec):
    return vec.reshape(1, -1).astype(F32)


def _pad_rows(mat, start, total):
    return jnp.zeros((total, mat.shape[1]), mat.dtype).at[start:start + mat.shape[0]].set(mat)


def kernel(x, norm1_g, w_in, conv_w, conv_b, lru_wa, lru_ba, lru_wx, lru_bx, lru_lambda,
           lru_norm_g, fox_fb, fox_qnorm_g, fox_knorm_g, fox_norm_g, rwkv_mu, rwkv_w0,
           rwkv_w2, rwkv_a0, rwkv_a2, rwkv_g2, rwkv_kk, rwkv_ka, rwkv_rk, rwkv_ln_g,
           rwkv_ln_b, w_out, norm2_g, router_gw, router_gb, router_ew, router_eb,
           exp_w_gate, exp_w_up, exp_w_down):
    batch, seq, d_model = x.shape
    depth = w_in.shape[0]
    lru_w = conv_w.shape[2]
    fox_heads = fox_fb.shape[1]
    fox_w = fox_heads * HEAD_DIM
    rwkv_w = rwkv_w0.shape[1]
    rwkv_cols = rwkv_mu.shape[1]
    d_rank, a_rank, g_rank = rwkv_w2.shape[1], rwkv_a2.shape[1], rwkv_g2.shape[1]
    low = d_rank + a_rank + g_rank
    n_exp, _, d_exp = exp_w_gate.shape[1:]
    assert low == LANES and rwkv_cols == 3 * rwkv_w + low
    assert FOX_BIAS_PIECES * fox_heads <= LANES and n_exp == N_EXPERTS
    assert seq % TOKEN_TILE == 0 and seq % FOX_BLOCK == 0 and seq % RWKV_BLOCK == 0
    assert RWKV_BLOCK % RWKV_CUM_ROWS == 0 and RWKV_CUM_ROWS % CHUNK == 0
    assert (TOKEN_TILE // IN_PROJ_SUBTILES) % SUBLANES == 0 and TOKEN_TILE % MOE_BLOCK == 0

    o_fox = 2 * lru_w
    o_fl = o_fox + 3 * fox_w
    o_rwkv = o_fl + fox_heads
    seg = (0, o_fox, o_fox + fox_w, o_fox + 2 * fox_w, o_fl, o_fl + rwkv_cols,
           o_fl + rwkv_cols + LANES)

    x2 = x.reshape(batch * seq, d_model)
    for l in range(depth):
        wl = w_in[l]
        n_fl = FOX_BIAS_PIECES * fox_heads
        w_perm = jnp.concatenate(
            [wl[:, 0:o_fl], wl[:, o_rwkv:o_rwkv + rwkv_cols],
             jnp.repeat(wl[:, o_fl:o_rwkv], FOX_BIAS_PIECES, axis=1),
             jnp.zeros((d_model, LANES - n_fl), wl.dtype)], axis=1).astype(BF16)
        qg = _row(jnp.tile(fox_qnorm_g[l], fox_heads) * (HEAD_DIM ** -0.5 * LOG2E))
        kg = _row(jnp.tile(fox_knorm_g[l], fox_heads))
        fb = _row(jnp.pad(jnp.repeat(fox_fb[l], FOX_BIAS_PIECES), (0, LANES - n_fl)))
        wg = jnp.concatenate([_block_diag(lru_wa[l]), _block_diag(lru_wx[l])], axis=1).astype(BF16)
        bg = _row(jnp.concatenate([lru_ba[l], lru_bx[l]]))
        lru_params = (conv_w[l], _row(conv_b[l]), wg, bg, _row(lru_lambda[l]), _row(lru_norm_g[l]))
        ya, q, k_aug, v, p = _in_proj(x2, _row(norm1_g[l]), w_perm, qg, kg, fb, lru_params,
                                     seq=seq, tm=TOKEN_TILE, seg=seg)

        yb = _fox(q, k_aug, v, batch=batch, seq=seq, tq=FOX_BLOCK)

        yc = _rwkv(p, _row(rwkv_mu[l]), _row(rwkv_w0[l]),
                   _pad_rows(rwkv_w2[l], 0, low).astype(BF16), _row(rwkv_a0[l]),
                   _pad_rows(rwkv_a2[l], d_rank, low).astype(BF16),
                   _pad_rows(rwkv_g2[l], d_rank + a_rank, low).astype(BF16),
                   _row(rwkv_kk[l]), _row(rwkv_ka[l]), _row(rwkv_rk[l]),
                   _row(rwkv_ln_g[l]), _row(rwkv_ln_b[l]), batch=batch, seq=seq, tb=RWKV_BLOCK)

        wo = w_out[l].astype(BF16)
        wr = jnp.concatenate(
            [router_gw[l], router_ew[l],
             jnp.zeros((d_model, LANES - N_GROUPS - n_exp), F32)], axis=1)
        wr_hi = wr.astype(BF16)
        wr = jnp.concatenate([wr_hi, (wr - wr_hi.astype(F32)).astype(BF16)], axis=1)
        br = _row(jnp.pad(jnp.concatenate([router_gb[l], router_eb[l]]),
                          (0, LANES - N_GROUPS - n_exp)))
        x1, h2, comb = _out_proj(
            ya, yb, yc, x2, wo[0:lru_w], wo[lru_w:lru_w + fox_w], wo[lru_w + fox_w:],
            _row(fox_norm_g[l]), _row(norm2_g[l]), wr, br, tm=TOKEN_TILE)

        wd = exp_w_down[l].astype(BF16).reshape(N_GROUPS, EXPERTS_PER_GROUP * d_exp, d_model)
        x2 = _moe(h2, comb, x1, exp_w_gate[l].astype(BF16), exp_w_up[l].astype(BF16), wd,
                  tm=TOKEN_TILE)
    return x2.reshape(batch, seq, d_model)
```

```python
import functools

import jax
import jax.numpy as jnp
from jax import lax
from jax.experimental import pallas as pl
from jax.experimental.pallas import tpu as pltpu

F32 = jnp.float32
BF16 = jnp.bfloat16

NORM_EPS = 1e-6
GN_EPS = 64e-5
LRU_C = 8.0
HEAD_DIM = 64
HEAD_SHIFT = 6
LANES = 128
SUBLANES = 8
CHUNK = 64
TOKEN_TILE = 512
FOX_BLOCK = 512
RWKV_BLOCK = 512
IN_PROJ_SUBTILES = 2
RWKV_CUM_ROWS = 256
CONV_WIDTH = 4
N_GROUPS = 4
EXPERTS_PER_GROUP = 4
N_EXPERTS = N_GROUPS * EXPERTS_PER_GROUP
MOE_BLOCK = 128
NEG_BIG = -1e30
LOG2E = 1.4426950408889634
FOX_BIAS_PIECES = 3
FOX_DEN_ROWS = 16
VMEM_LIMIT = 56 * 1024 * 1024


def _dot(a, b):
    return jnp.dot(a, b, preferred_element_type=F32)


def _dot_nt(a, b):
    return lax.dot_general(a, b, (((1,), (1,)), ((), ())), preferred_element_type=F32)


def _dot_tn(a, b):
    return lax.dot_general(a, b, (((0,), (0,)), ((), ())), preferred_element_type=F32)


def _head_sum(x, ones):
    return _dot(x.astype(BF16), ones)


def _split_dot_left(ones, x, parts):
    acc = None
    rem = x
    for _ in range(parts):
        hi = rem.astype(BF16)
        t = _dot(ones, hi)
        acc = t if acc is None else acc + t
        rem = rem - hi.astype(F32)
    return acc


def _head_ones(n):
    r = lax.broadcasted_iota(jnp.int32, (n, n), 0) >> HEAD_SHIFT
    c = lax.broadcasted_iota(jnp.int32, (n, n), 1) >> HEAD_SHIFT
    return jnp.where(r == c, 1.0, 0.0).astype(BF16)


def _softplus(z):
    return jnp.maximum(z, 0.0) + jnp.log(1.0 + jnp.exp(-jnp.abs(z)))


def _rms(x, eps=NORM_EPS):
    return x * lax.rsqrt(jnp.mean(x * x, axis=-1, keepdims=True) + eps)


def _resident(shape):
    zeros = (0,) * len(shape)
    return pl.BlockSpec(shape, lambda *_: zeros, pipeline_mode=pl.Buffered(1))


def _in_proj_kernel(x_ref, g_ref, w_ref, qg_ref, kg_ref, fb_ref,
                    cw_ref, cb_ref, wg_ref, bg_ref, lam_ref, ng_ref,
                    ya_ref, q_ref, k_ref, v_ref, p_ref,
                    carry_ref, tail_ref, hst_ref, buf_ref,
                    *, tiles_per_seq, seg, sub):
    i = pl.program_id(0)

    @pl.when(i % tiles_per_seq == 0)
    def _():
        carry_ref[...] = jnp.zeros_like(carry_ref)
        tail_ref[...] = jnp.zeros_like(tail_ref)
        hst_ref[...] = jnp.zeros_like(hst_ref)

    fox = seg[2] - seg[1]
    n_heads = fox // HEAD_DIM
    ones = _head_ones(fox)
    inv_d = 1.0 / HEAD_DIM
    rr = lax.broadcasted_iota(jnp.int32, (sub, sub), 0)
    cc = lax.broadcasted_iota(jnp.int32, (sub, sub), 1)
    tri = jnp.where(cc <= rr, 1.0, 0.0).astype(BF16)
    lane = lax.broadcasted_iota(jnp.int32, (sub, LANES), 1)
    feat = lane < HEAD_DIM
    bias_lanes = (lane >= HEAD_DIM) & (lane < HEAD_DIM + FOX_BIAS_PIECES)
    q_bias = jnp.where(bias_lanes, 1.0, 0.0)
    piece_id = lax.rem(lane, FOX_BIAS_PIECES)
    carry = carry_ref[0:1, :]
    tail = tail_ref[...]
    h_lru = hst_ref[0:1, :]
    lru_w = (seg[1] - seg[0]) // 2

    def head_norm(t):
        return t * lax.rsqrt(_head_sum(t * t, ones) * inv_d + NORM_EPS)

    sub_rows = [slice(r0, r0 + sub) for r0 in range(0, x_ref.shape[0], sub)]
    state = {"carry": carry, "tail": tail, "h": h_lru}

    def project_steps(rows, out):
        hb = []

        def first():
            hb.append((_rms(x_ref[rows, :]) * g_ref[...]).astype(BF16))
            out[0] = _dot(hb[0], w_ref[:, seg[0]:seg[1]])

        def segment(n):
            out[n] = _dot(hb[0], w_ref[:, seg[n]:seg[n + 1]])

        def rwkv_cols():
            p_ref[rows, :] = _dot(hb[0], w_ref[:, seg[4]:seg[5]])

        return [first, functools.partial(segment, 5), functools.partial(segment, 1),
                functools.partial(segment, 2), functools.partial(segment, 3), rwkv_cols]

    def finish_steps(idx, rows, out):
        local = {}

        def lru():
            u = out[0]
            ya_ref[rows, :], state["tail"], state["h"] = _lru_block(
                u[:, 0:lru_w], u[:, lru_w:2 * lru_w], state["tail"], state["h"], buf_ref.at[idx],
                cw_ref, cb_ref, wg_ref, bg_ref, lam_ref, ng_ref)

        def forget():
            lf = -_softplus(-(out[5] + fb_ref[...]))
            c = _split_dot_left(tri, lf, 3) + state["carry"]
            state["carry"] = c[sub - 1:sub, :]
            rem = c * (-LOG2E)
            pieces = None
            for piece in range(FOX_BIAS_PIECES):
                hi = rem.astype(BF16).astype(F32)
                pieces = hi if pieces is None else jnp.where(piece_id == piece, hi, pieces)
                rem = rem - hi
            local["pieces"] = pieces

        def spread(t, h):
            th = t[:, (h // 2) * LANES:(h // 2 + 1) * LANES]
            return pltpu.roll(th, HEAD_DIM, axis=1) if h % 2 else th

        def queries():
            q = head_norm(out[1]) * qg_ref[...]
            for h in range(n_heads):
                q_ref[h * LANES:(h + 1) * LANES, rows] = (
                    jnp.where(feat, spread(q, h), q_bias).T.astype(BF16))

        def keys():
            k = head_norm(out[2]) * kg_ref[...]
            for h in range(n_heads):
                k_bias = pltpu.roll(local["pieces"], HEAD_DIM - FOX_BIAS_PIECES * h, axis=1)
                k_ref[rows, h * LANES:(h + 1) * LANES] = jnp.where(
                    feat, spread(k, h), jnp.where(bias_lanes, k_bias, 0.0)).astype(BF16)

        def values():
            v_ref[:, rows] = out[3].T.astype(BF16)

        return [lru, forget, queries, keys, values]

    outs = [dict() for _ in sub_rows]
    for step in project_steps(sub_rows[0], outs[0]):
        step()
    for idx, rows in enumerate(sub_rows):
        ahead = project_steps(sub_rows[idx + 1], outs[idx + 1]) if idx + 1 < len(sub_rows) else []
        finish = finish_steps(idx, rows, outs[idx])
        for n in range(max(len(ahead), len(finish))):
            if n < len(ahead):
                ahead[n]()
            if n < len(finish):
                finish[n]()

    carry_ref[...] = jnp.broadcast_to(state["carry"], carry_ref.shape)
    tail_ref[...] = state["tail"]
    hst_ref[...] = jnp.broadcast_to(state["h"], hst_ref.shape)


def _in_proj(x2, g, w, qg, kg, fb, lru_params, *, seq, tm, seg):
    T, D = x2.shape
    n_out = w.shape[1]
    widths = [seg[j + 1] - seg[j] for j in range(6)]
    row = lambda i: (i, 0)
    col = lambda i: (0, i)
    outs = [
        ((T, widths[0] // 2), (tm, widths[0] // 2), row, BF16),
        ((2 * widths[1], T), (2 * widths[1], tm), col, BF16),
        ((T, 2 * widths[2]), (tm, 2 * widths[2]), row, BF16),
        ((widths[3], T), (widths[3], tm), col, BF16),
        ((T, widths[4]), (tm, widths[4]), row, F32),
    ]
    return pl.pallas_call(
        functools.partial(_in_proj_kernel, tiles_per_seq=seq // tm, seg=seg,
                          sub=tm // IN_PROJ_SUBTILES),
        out_shape=tuple(jax.ShapeDtypeStruct(shape, dt) for shape, _, _, dt in outs),
        grid=(T // tm,),
        in_specs=[
            pl.BlockSpec((tm, D), row),
            _resident((1, D)),
            _resident((D, n_out)),
            _resident((1, widths[1])),
            _resident((1, widths[2])),
            _resident((1, widths[5])),
        ] + [_resident(t.shape) for t in lru_params],
        out_specs=tuple(pl.BlockSpec(blk, imap) for _, blk, imap, _ in outs),
        scratch_shapes=[pltpu.VMEM((SUBLANES, widths[5]), F32),
                        pltpu.VMEM((SUBLANES, widths[0] // 2), F32),
                        pltpu.VMEM((SUBLANES, widths[0] // 2), F32),
                        pltpu.VMEM((IN_PROJ_SUBTILES, tm // IN_PROJ_SUBTILES + SUBLANES,
                                    widths[0] // 2), F32)],
        compiler_params=pltpu.CompilerParams(
            dimension_semantics=("arbitrary",), vmem_limit_bytes=VMEM_LIMIT),
        name="in_proj",
    )(x2, g, w, qg, kg, fb, *lru_params)


def _shift_rows(x, d, fill, row):
    return jnp.where(row >= d, pltpu.roll(x, d, axis=0), fill)


def _lru_block(xa, ga, tail, h_prev, buf_ref, cw_ref, cb_ref, wg_ref, bg_ref, lam_ref, ng_ref):
    tb, w = xa.shape
    pad = SUBLANES
    buf_ref[0:pad, :] = tail
    buf_ref[pad:pad + tb, :] = xa
    xc = cb_ref[...] + cw_ref[CONV_WIDTH - 1:CONV_WIDTH, :] * xa
    for d in range(1, CONV_WIDTH):
        xc = xc + cw_ref[CONV_WIDTH - 1 - d:CONV_WIDTH - d, :] * buf_ref[pad - d:pad - d + tb, :]

    gates = _dot(xc.astype(BF16), wg_ref[...]) + bg_ref[...]
    r = jax.nn.sigmoid(gates[:, 0:w])
    i = jax.nn.sigmoid(gates[:, w:2 * w])
    log_a = (-LRU_C) * r * _softplus(-lam_ref[...])
    a = jnp.exp(log_a)
    b = jnp.sqrt(1.0 - a * a) * (i * xc)

    row = lax.broadcasted_iota(jnp.int32, (tb, w), 0)
    d = 1
    while d < tb:
        a_sh = _shift_rows(a, d, 1.0, row)
        b_sh = _shift_rows(b, d, 0.0, row)
        b = a * b_sh + b
        a = a * a_sh
        d *= 2
    h = b + a * h_prev
    y = jax.nn.gelu(ga) * h
    return (_rms(y) * ng_ref[...]).astype(BF16), xa[tb - pad:tb, :], h[tb - 1:tb, :]


def _fox_kernel(q_ref, k_ref, v_ref, o_ref, s_buf, mx_buf, p_buf, m_ref, acc_ref, fin_ref, *, tq):
    n_heads = k_ref.shape[1] // LANES
    nq = k_ref.shape[0] // tq
    half = tq // 2
    causal_top = (lax.broadcasted_iota(jnp.int32, (half, tq), 0)
                  <= lax.broadcasted_iota(jnp.int32, (half, tq), 1))
    causal_bot = (lax.broadcasted_iota(jnp.int32, (half, half), 0)
                  <= lax.broadcasted_iota(jnp.int32, (half, half), 1))
    items = [(i, j) for i in range(nq) for j in range(i + 1)]

    def scores(t, u):
        i, j = items[t]
        slot = t % 2
        hl = slice(u * LANES, (u + 1) * LANES)
        k0, q0 = j * tq, i * tq
        if i != j:
            s = _dot(k_ref[k0:k0 + tq, hl], q_ref[hl, q0:q0 + tq])
            s_buf[slot, u] = s
            mx_buf[slot, u] = jnp.max(s, axis=0, keepdims=True)
        else:
            top = jnp.where(causal_top, _dot(k_ref[k0:k0 + half, hl], q_ref[hl, q0:q0 + tq]), NEG_BIG)
            bot = jnp.where(causal_bot,
                            _dot(k_ref[k0 + half:k0 + tq, hl], q_ref[hl, q0 + half:q0 + tq]), NEG_BIG)
            s_buf[slot, u, 0:half, :] = top
            s_buf[slot, u, half:tq, half:tq] = bot
            mx_top = jnp.max(top, axis=0, keepdims=True)
            mx_buf[slot, u] = jnp.concatenate(
                [mx_top[:, 0:half],
                 jnp.maximum(mx_top[:, half:tq], jnp.max(bot, axis=0, keepdims=True))], axis=1)

    def softmax_numerators(t, u):
        i, j = items[t]
        slot = t % 2
        m_new = mx_buf[slot, u] if j == 0 else jnp.maximum(m_ref[u], mx_buf[slot, u])
        if i != j:
            p_buf[u] = jnp.exp2(s_buf[slot, u] - m_new).astype(BF16)
        else:
            p_buf[u, 0:half, :] = jnp.exp2(s_buf[slot, u, 0:half, :] - m_new).astype(BF16)
            p_buf[u, half:tq, half:tq] = jnp.exp2(
                s_buf[slot, u, half:tq, half:tq] - m_new[:, half:tq]).astype(BF16)
        return m_new

    def accumulate(t, u, m_new):
        i, j = items[t]
        vt = jnp.concatenate([v_ref[u * HEAD_DIM:(u + 1) * HEAD_DIM, j * tq:(j + 1) * tq],
                              jnp.ones((FOX_DEN_ROWS, tq), BF16)], axis=0)
        if i != j:
            acc = _dot(vt, p_buf[u])
        else:
            late = _dot(vt[:, half:tq], p_buf[u, half:tq, half:tq])
            acc = _dot(vt[:, 0:half], p_buf[u, 0:half, :]) + jnp.concatenate(
                [jnp.zeros((vt.shape[0], half), F32), late], axis=1)
        if j > 0:
            acc = jnp.exp2(m_ref[u] - m_new) * acc_ref[u] + acc
        if j < i:
            m_ref[u] = m_new
            acc_ref[u] = acc
        else:
            fin_ref[u * HEAD_DIM:(u + 1) * HEAD_DIM, :] = acc[0:HEAD_DIM] / acc[HEAD_DIM:HEAD_DIM + 1]
            if u == n_heads - 1:
                o_ref[i * tq:(i + 1) * tq, :] = fin_ref[...].T.astype(BF16)

    for u in range(n_heads):
        scores(0, u)
    for t in range(len(items)):
        for u in range(n_heads):
            m_new = softmax_numerators(t, u)
            if t + 1 < len(items):
                scores(t + 1, u)
            accumulate(t, u, m_new)


def _fox(q_aug, k_aug, v_t, *, batch, seq, tq):
    T, wide = k_aug.shape
    heads_per_step = 2
    blk = heads_per_step * LANES
    out_w = heads_per_step * HEAD_DIM
    steps = wide // blk
    return pl.pallas_call(
        functools.partial(_fox_kernel, tq=tq),
        out_shape=jax.ShapeDtypeStruct((T, steps * out_w), BF16),
        grid=(batch, steps),
        in_specs=[
            pl.BlockSpec((blk, seq), lambda b, h: (h, b)),
            pl.BlockSpec((seq, blk), lambda b, h: (b, h)),
            pl.BlockSpec((out_w, seq), lambda b, h: (h, b)),
        ],
        out_specs=pl.BlockSpec((seq, out_w), lambda b, h: (b, h)),
        scratch_shapes=[
            pltpu.VMEM((2, heads_per_step, tq, tq), F32),
            pltpu.VMEM((2, heads_per_step, 1, tq), F32),
            pltpu.VMEM((heads_per_step, tq, tq), BF16),
            pltpu.VMEM((heads_per_step, 1, tq), F32),
            pltpu.VMEM((heads_per_step, HEAD_DIM + FOX_DEN_ROWS, tq), F32),
            pltpu.VMEM((out_w, tq), F32),
        ],
        compiler_params=pltpu.CompilerParams(
            dimension_semantics=("parallel", "parallel"), vmem_limit_bytes=VMEM_LIMIT),
        name="fox",
    )(q_aug, k_aug, v_t)


def _stack_heads(x, lane_head, n_heads):
    zero = jnp.zeros_like(x)
    return jnp.concatenate([jnp.where(lane_head == h, x, zero) for h in range(n_heads)], axis=0)


def _rwkv_kernel(p_ref, mu_ref, w0_ref, w2_ref, a0_ref, a2_ref, g2_ref, kk_ref, ka_ref,
                 rk_ref, lng_ref, lnb_ref, y_ref, prev_ref, st_ref):
    j = pl.program_id(1)
    tb, cols = p_ref.shape
    w = y_ref.shape[1]
    n_heads = w // HEAD_DIM
    C = CHUNK
    chunk_shift = CHUNK.bit_length() - 1

    @pl.when(j == 0)
    def _():
        prev_ref[...] = jnp.zeros_like(prev_ref)
        st_ref[...] = jnp.zeros_like(st_ref)

    p = p_ref[...]
    row = lax.broadcasted_iota(jnp.int32, (tb, cols), 0)
    prev = jnp.where(row == 0, prev_ref[0:1, :], pltpu.roll(p, 1, axis=0))
    prev_ref[...] = jnp.broadcast_to(p[tb - 1:tb, :], prev_ref.shape)
    ps = p + (prev - p) * mu_ref[...]
    r = ps[:, 0:w]
    k = ps[:, w:2 * w]
    v = ps[:, 2 * w:3 * w]
    lr = ps[:, 3 * w:cols]

    wpre = w0_ref[...] + _dot(jnp.tanh(lr).astype(BF16), w2_ref[...])
    lw = -jnp.exp(-_softplus(-wpre) - 0.5)
    a = jax.nn.sigmoid(a0_ref[...] + _dot(lr.astype(BF16), a2_ref[...]))
    g = _dot(jax.nn.sigmoid(lr).astype(BF16), g2_ref[...])

    ones = _head_ones(w)
    kk = k * kk_ref[...]
    kk = kk * lax.rsqrt(jnp.maximum(_head_sum(kk * kk, ones), 1e-24))
    k2 = k * (1.0 + (a - 1.0) * ka_ref[...])
    b = kk * a
    bonus = _head_sum(r * k2 * rk_ref[...], ones) * v

    span = min(RWKV_CUM_ROWS, tb)
    rr = lax.broadcasted_iota(jnp.int32, (span, span), 0)
    cc = lax.broadcasted_iota(jnp.int32, (span, span), 1)
    tri = jnp.where(((rr >> chunk_shift) == (cc >> chunk_shift)) & (cc <= rr), 1.0, 0.0).astype(BF16)
    LW = jnp.concatenate(
        [_split_dot_left(tri, lw[r0:r0 + span], 2) for r0 in range(0, tb, span)], axis=0)
    LT = jnp.concatenate(
        [jnp.broadcast_to(LW[c * C + C - 1:c * C + C, :], (C, w)) for c in range(tb // C)], axis=0)
    e_out = jnp.exp(-LW)
    e_end = jnp.exp(LT - LW)
    at = (-kk) * jnp.exp(LW - lw)
    rt = r * jnp.exp(LW)
    bt = b * e_out
    kt = k2 * e_out
    b_end = b * e_end
    k_end = k2 * e_end
    w_tot = jnp.exp(LT)

    lane_head = lax.broadcasted_iota(jnp.int32, (C, w), 1) >> HEAD_SHIFT
    t_idx = lax.broadcasted_iota(jnp.int32, (C, w), 0)
    j_idx = lax.broadcasted_iota(jnp.int32, (C, w), 1) & (HEAD_DIM - 1)
    strict = j_idx < t_idx
    incl = j_idx <= t_idx
    eye = jnp.where(j_idx == t_idx, 1.0, 0.0)
    sq_r = lax.broadcasted_iota(jnp.int32, (w, w), 0) >> HEAD_SHIFT
    sq_c = lax.broadcasted_iota(jnp.int32, (w, w), 1) >> HEAD_SHIFT
    block_diag = sq_r == sq_c
    stack = functools.partial(_stack_heads, lane_head=lane_head, n_heads=n_heads)

    chunks = range(tb // C)
    rows_of = [slice(c * C, (c + 1) * C) for c in chunks]
    bf = lambda t: t.astype(BF16)
    cat0 = lambda *ts: jnp.concatenate(ts, axis=0)
    cat1 = lambda *ts: jnp.concatenate(ts, axis=1)

    A = [_dot_nt(bf(cat0(at[sl], rt[sl])), cat0(stack(bf(bt[sl])), stack(bf(kt[sl]))))
         for sl in rows_of]
    a_ab = [jnp.where(strict, a[0:C, 0:w], 0.0) for a in A]
    a_ak = [jnp.where(strict, a[0:C, w:2 * w], 0.0) for a in A]
    a_rb = [jnp.where(incl, a[C:2 * C, 0:w], 0.0) for a in A]
    a_rk = [jnp.where(incl, a[C:2 * C, w:2 * w], 0.0) for a in A]

    pw = [_dot(bf(l), stack(bf(l))) for l in a_ab]
    tinv = [eye + l for l in a_ab]
    sz = 2
    while 2 * sz < C:
        both = [_dot(bf(cat0(t, p)), stack(bf(p))) for t, p in zip(tinv, pw)]
        tinv = [t + r[0:C] for t, r in zip(tinv, both)]
        pw = [r[C:2 * C] for r in both]
        sz *= 2
    tinv = [t + _dot(bf(t), stack(bf(p))) for t, p in zip(tinv, pw)]

    yo = [_dot(bf(cat0(ak, ark)), stack(bf(v[sl]))) for ak, ark, sl in zip(a_ak, a_rk, rows_of)]
    gu = [_dot(bf(t), cat1(stack(bf(at[sl])), stack(bf(y[0:C]))))
          for t, y, sl in zip(tinv, yo, rows_of)]
    pq = [_dot(bf(rb), cat1(stack(bf(x[:, 0:w])), stack(bf(x[:, w:2 * w]))))
          for rb, x in zip(a_rb, gu)]
    p_mat = [rt[sl] + x[:, 0:w] for sl, x in zip(rows_of, pq)]
    q_mat = [y[C:2 * C] + x[:, w:2 * w] for y, x in zip(yo, pq)]
    m_mat = [jnp.where(block_diag, _dot_tn(bf(x[:, 0:w]), bf(b_end[sl])), 0.0)
             for x, sl in zip(gu, rows_of)]
    n_mat = [jnp.where(block_diag,
                       _dot_tn(bf(cat0(x[:, w:2 * w], v[sl])), bf(cat0(b_end[sl], k_end[sl]))), 0.0)
             for x, sl in zip(gu, rows_of)]

    st = st_ref[...]
    outs = []
    for c in chunks:
        stb = bf(st)
        outs.append(_dot_nt(bf(p_mat[c]), stb) + q_mat[c])
        st = st * w_tot[c * C:c * C + 1, :] + _dot(stb, bf(m_mat[c])) + n_mat[c]
    st_ref[...] = st

    o = jnp.concatenate(outs, axis=0)
    inv_d = 1.0 / HEAD_DIM
    mean = _head_sum(o, ones) * inv_d
    cen = o - mean
    var = _head_sum(cen * cen, ones) * inv_d
    o = cen * lax.rsqrt(var + GN_EPS) * lng_ref[...] + lnb_ref[...]
    y_ref[...] = ((o + bonus) * g).astype(BF16)


def _rwkv(p, mu, w0, w2, a0, a2, g2, kkp, ka, rk, lng, lnb, *, batch, seq, tb):
    T, cols = p.shape
    w = w0.shape[1]
    nb = seq // tb
    vec = _resident((1, w))
    low = _resident((cols - 3 * w, w))
    return pl.pallas_call(
        _rwkv_kernel,
        out_shape=jax.ShapeDtypeStruct((T, w), BF16),
        grid=(batch, nb),
        in_specs=[
            pl.BlockSpec((tb, cols), lambda b, j: (b * nb + j, 0)),
            _resident((1, cols)), vec, low, vec, low, low, vec, vec, vec, vec, vec,
        ],
        out_specs=pl.BlockSpec((tb, w), lambda b, j: (b * nb + j, 0)),
        scratch_shapes=[pltpu.VMEM((SUBLANES, cols), F32), pltpu.VMEM((w, w), F32)],
        compiler_params=pltpu.CompilerParams(
            dimension_semantics=("parallel", "arbitrary"), vmem_limit_bytes=VMEM_LIMIT),
        name="rwkv",
    )(p, mu, w0, w2, a0, a2, g2, kkp, ka, rk, lng, lnb)


def _out_proj_kernel(ya_ref, yb_ref, yc_ref, x_ref, wa_ref, wb_ref, wc_ref, fg_ref, n2_ref,
                     wr_ref, br_ref, x1_ref, h2_ref, comb_ref, *, sub):
    sub_rows = [slice(r0, r0 + sub) for r0 in range(0, x_ref.shape[0], sub)]
    mixed = []
    for rows in sub_rows:
        ybn = (_rms(yb_ref[rows, :].astype(F32)) * fg_ref[...]).astype(BF16)
        mixed.append(_dot(ya_ref[rows, :], wa_ref[...]) + _dot(ybn, wb_ref[...])
                     + _dot(yc_ref[rows, :], wc_ref[...]))
    for rows, y in zip(sub_rows, mixed):
        x1 = x_ref[rows, :] + y
        x1_ref[rows, :] = x1
        h2 = _rms(x1) * n2_ref[...]
        h_hi = h2.astype(BF16)
        h2_ref[rows, :] = h_hi

        h_lo = (h2 - h_hi.astype(F32)).astype(BF16)
        hw = _dot(h_hi, wr_ref[...])
        logits = (hw[:, 0:LANES] + hw[:, LANES:2 * LANES] + _dot(h_lo, wr_ref[:, 0:LANES])
                  + br_ref[...])
        comb_ref[rows, :] = _route(logits)


def _route(logits):
    lane = lax.broadcasted_iota(jnp.int32, logits.shape, 1)
    lane_f = lane.astype(F32)
    far = float(LANES)

    def first_argmax(vals, vmax):
        return jnp.min(jnp.where(vals == vmax, lane_f, far), axis=-1, keepdims=True)

    gl = jnp.where(lane < N_GROUPS, logits, NEG_BIG)
    gmax = jnp.max(gl, axis=-1, keepdims=True)
    gidx = first_argmax(gl, gmax)
    g_p = 1.0 / jnp.sum(jnp.exp(gl - gmax), axis=-1, keepdims=True)

    e_group = ((lane - N_GROUPS) >> 2).astype(F32)
    el = jnp.where((lane >= N_GROUPS) & (e_group == gidx), logits, NEG_BIG)
    emax = jnp.max(el, axis=-1, keepdims=True)
    esum = jnp.sum(jnp.exp(el - emax), axis=-1, keepdims=True)
    i1 = first_argmax(el, emax)
    el2 = jnp.where(lane_f == i1, NEG_BIG, el)
    emax2 = jnp.max(el2, axis=-1, keepdims=True)
    i2 = first_argmax(el2, emax2)
    p1 = 1.0 / esum
    p2 = jnp.exp(emax2 - emax) / esum
    den = p1 + p2
    return (jnp.where(lane_f == i1, g_p * (p1 / den), 0.0)
            + jnp.where(lane_f == i2, g_p * (p2 / den), 0.0)
            + jnp.where(lane_f == gidx, 1.0, 0.0))


def _out_proj(ya, yb, yc, x2, wa, wb, wc, fg, n2, wr, br, *, tm):
    T, D = x2.shape
    row = lambda i: (i, 0)
    return pl.pallas_call(
        functools.partial(_out_proj_kernel, sub=tm // 2),
        out_shape=(jax.ShapeDtypeStruct((T, D), F32), jax.ShapeDtypeStruct((T, D), BF16),
                   jax.ShapeDtypeStruct((T, LANES), F32)),
        grid=(T // tm,),
        in_specs=[
            pl.BlockSpec((tm, ya.shape[1]), row),
            pl.BlockSpec((tm, yb.shape[1]), row),
            pl.BlockSpec((tm, yc.shape[1]), row),
            pl.BlockSpec((tm, D), row),
            _resident(wa.shape), _resident(wb.shape), _resident(wc.shape),
            _resident(fg.shape), _resident(n2.shape), _resident(wr.shape), _resident(br.shape),
        ],
        out_specs=(pl.BlockSpec((tm, D), row), pl.BlockSpec((tm, D), row),
                   pl.BlockSpec((tm, LANES), row)),
        compiler_params=pltpu.CompilerParams(
            dimension_semantics=("parallel",), vmem_limit_bytes=VMEM_LIMIT),
        name="out_proj",
    )(ya, yb, yc, x2, wa, wb, wc, fg, n2, wr, br)


def _moe_kernel(h_ref, route_ref, x_ref, wg_ref, wu_ref, wd_ref, o_ref, hs_ref, rs_ref, ys_ref):
    tm = h_ref.shape[0]
    n_groups = wd_ref.shape[0]
    route = route_ref[...]
    lane = lax.broadcasted_iota(jnp.int32, (tm, LANES), 1)
    onehot = jnp.where(lane < n_groups, route, 0.0)
    rr = lax.broadcasted_iota(jnp.int32, (tm, tm), 0)
    cc = lax.broadcasted_iota(jnp.int32, (tm, tm), 1)
    tri = jnp.where(cc <= rr, 1.0, 0.0).astype(BF16)
    cum = _dot(tri, onehot.astype(BF16))
    tot = cum[tm - 1:tm, :]
    start = jnp.zeros_like(tot)
    for d in range(1, n_groups):
        start = start + jnp.where(lane[0:1] >= d, pltpu.roll(tot, d, axis=1), 0.0)
    end = start + tot
    pos = jnp.sum(jnp.where(onehot > 0.0, start + cum - 1.0, 0.0), axis=-1, keepdims=True)
    perm_t = jnp.where(pos == cc.astype(F32), 1.0, 0.0).astype(BF16)
    pos_row = jnp.broadcast_to(pos, (tm, LANES)).T[0:1, :]
    perm = jnp.where(rr.astype(F32) == pos_row, 1.0, 0.0).astype(BF16)

    hs_ref[...] = _dot(perm, h_ref[...]).astype(BF16)
    rem = route
    sorted_route = None
    for _ in range(2):
        hi = rem.astype(BF16)
        t = _dot(perm, hi)
        sorted_route = t if sorted_route is None else sorted_route + t
        rem = rem - hi.astype(F32)
    rs_ref[...] = sorted_route
    ys_ref[...] = jnp.zeros_like(ys_ref)

    for g in range(n_groups):
        g_start = start[0, g]
        g_end = end[0, g]
        for r0 in range(0, tm, MOE_BLOCK):

            @pl.when((g_start < r0 + MOE_BLOCK) & (g_end > r0))
            def _(g=g, r0=r0):
                rows = slice(r0, r0 + MOE_BLOCK)
                hb = hs_ref[rows, :]
                rw = rs_ref[rows, :]
                hids = []
                for e in range(g * EXPERTS_PER_GROUP, (g + 1) * EXPERTS_PER_GROUP):
                    gate = _dot(hb, wg_ref[e])
                    hid = gate * jax.nn.sigmoid(gate) * _dot(hb, wu_ref[e])
                    hids.append((hid * rw[:, N_GROUPS + e:N_GROUPS + e + 1]).astype(BF16))
                ys_ref[rows, :] += _dot(jnp.concatenate(hids, axis=1), wd_ref[g])

    o_ref[...] = x_ref[...] + _dot(perm_t, ys_ref[...].astype(BF16))


def _moe(h2, route, x1, wg, wu, wd, *, tm):
    T, D = x1.shape
    row = lambda i: (i, 0)
    return pl.pallas_call(
        _moe_kernel,
        out_shape=jax.ShapeDtypeStruct((T, D), F32),
        grid=(T // tm,),
        in_specs=[
            pl.BlockSpec((tm, D), row),
            pl.BlockSpec((tm, LANES), row),
            pl.BlockSpec((tm, D), row),
            _resident(wg.shape),
            _resident(wu.shape),
            _resident(wd.shape),
        ],
        out_specs=pl.BlockSpec((tm, D), row),
        scratch_shapes=[pltpu.VMEM((tm, D), BF16), pltpu.VMEM((tm, LANES), F32),
                        pltpu.VMEM((tm, D), F32)],
        compiler_params=pltpu.CompilerParams(
            dimension_semantics=("parallel",), vmem_limit_bytes=VMEM_LIMIT),
        name="moe",
    )(h2, route, x1, wg, wu, wd)


def _block_diag(wh):
    n_heads, d, _ = wh.shape
    eye = jnp.eye(n_heads, dtype=wh.dtype)
    return jnp.einsum("hij,hg->higj", wh, eye).reshape(n_heads * d, n_heads * d)


def _row(vec):
    return vec.reshape(1, -1).astype(F32)


def _pad_rows(mat, start, total):
    return jnp.zeros((total, mat.shape[1]), mat.dtype).at[start:start + mat.shape[0]].set(mat)


def kernel(x, norm1_g, w_in, conv_w, conv_b, lru_wa, lru_ba, lru_wx, lru_bx, lru_lambda,
           lru_norm_g, fox_fb, fox_qnorm_g, fox_knorm_g, fox_norm_g, rwkv_mu, rwkv_w0,
           rwkv_w2, rwkv_a0, rwkv_a2, rwkv_g2, rwkv_kk, rwkv_ka, rwkv_rk, rwkv_ln_g,
           rwkv_ln_b, w_out, norm2_g, router_gw, router_gb, router_ew, router_eb,
           exp_w_gate, exp_w_up, exp_w_down):
    batch, seq, d_model = x.shape
    depth = w_in.shape[0]
    lru_w = conv_w.shape[2]
    fox_heads = fox_fb.shape[1]
    fox_w = fox_heads * HEAD_DIM
    rwkv_w = rwkv_w0.shape[1]
    rwkv_cols = rwkv_mu.shape[1]
    d_rank, a_rank, g_rank = rwkv_w2.shape[1], rwkv_a2.shape[1], rwkv_g2.shape[1]
    low = d_rank + a_rank + g_rank
    n_exp, _, d_exp = exp_w_gate.shape[1:]
    assert low == LANES and rwkv_cols == 3 * rwkv_w + low
    assert FOX_BIAS_PIECES * fox_heads <= LANES and n_exp == N_EXPERTS
    assert seq % TOKEN_TILE == 0 and seq % FOX_BLOCK == 0 and seq % RWKV_BLOCK == 0
    assert RWKV_BLOCK % RWKV_CUM_ROWS == 0 and RWKV_CUM_ROWS % CHUNK == 0
    assert (TOKEN_TILE // IN_PROJ_SUBTILES) % SUBLANES == 0 and TOKEN_TILE % MOE_BLOCK == 0

    o_fox = 2 * lru_w
    o_fl = o_fox + 3 * fox_w
    o_rwkv = o_fl + fox_heads
    seg = (0, o_fox, o_fox + fox_w, o_fox + 2 * fox_w, o_fl, o_fl + rwkv_cols,
           o_fl + rwkv_cols + LANES)

    x2 = x.reshape(batch * seq, d_model)
    for l in range(depth):
        wl = w_in[l]
        n_fl = FOX_BIAS_PIECES * fox_heads
        w_perm = jnp.concatenate(
            [wl[:, 0:o_fl], wl[:, o_rwkv:o_rwkv + rwkv_cols],
             jnp.repeat(wl[:, o_fl:o_rwkv], FOX_BIAS_PIECES, axis=1),
             jnp.zeros((d_model, LANES - n_fl), wl.dtype)], axis=1).astype(BF16)
        qg = _row(jnp.tile(fox_qnorm_g[l], fox_heads) * (HEAD_DIM ** -0.5 * LOG2E))
        kg = _row(jnp.tile(fox_knorm_g[l], fox_heads))
        fb = _row(jnp.pad(jnp.repeat(fox_fb[l], FOX_BIAS_PIECES), (0, LANES - n_fl)))
        wg = jnp.concatenate([_block_diag(lru_wa[l]), _block_diag(lru_wx[l])], axis=1).astype(BF16)
        bg = _row(jnp.concatenate([lru_ba[l], lru_bx[l]]))
        lru_params = (conv_w[l], _row(conv_b[l]), wg, bg, _row(lru_lambda[l]), _row(lru_norm_g[l]))
        ya, q, k_aug, v, p = _in_proj(x2, _row(norm1_g[l]), w_perm, qg, kg, fb, lru_params,
                                     seq=seq, tm=TOKEN_TILE, seg=seg)

        yb = _fox(q, k_aug, v, batch=batch, seq=seq, tq=FOX_BLOCK)

        yc = _rwkv(p, _row(rwkv_mu[l]), _row(rwkv_w0[l]),
                   _pad_rows(rwkv_w2[l], 0, low).astype(BF16), _row(rwkv_a0[l]),
                   _pad_rows(rwkv_a2[l], d_rank, low).astype(BF16),
                   _pad_rows(rwkv_g2[l], d_rank + a_rank, low).astype(BF16),
                   _row(rwkv_kk[l]), _row(rwkv_ka[l]), _row(rwkv_rk[l]),
                   _row(rwkv_ln_g[l]), _row(rwkv_ln_b[l]), batch=batch, seq=seq, tb=RWKV_BLOCK)

        wo = w_out[l].astype(BF16)
        wr = jnp.concatenate(
            [router_gw[l], router_ew[l],
             jnp.zeros((d_model, LANES - N_GROUPS - n_exp), F32)], axis=1)
        wr_hi = wr.astype(BF16)
        wr = jnp.concatenate([wr_hi, (wr - wr_hi.astype(F32)).astype(BF16)], axis=1)
        br = _row(jnp.pad(jnp.concatenate([router_gb[l], router_eb[l]]),
                          (0, LANES - N_GROUPS - n_exp)))
        x1, h2, comb = _out_proj(
            ya, yb, yc, x2, wo[0:lru_w], wo[lru_w:lru_w + fox_w], wo[lru_w + fox_w:],
            _row(fox_norm_g[l]), _row(norm2_g[l]), wr, br, tm=TOKEN_TILE)

        wd = exp_w_down[l].astype(BF16).reshape(N_GROUPS, EXPERTS_PER_GROUP * d_exp, d_model)
        x2 = _moe(h2, comb, x1, exp_w_gate[l].astype(BF16), exp_w_up[l].astype(BF16), wd,
                  tm=TOKEN_TILE)
    return x2.reshape(batch, seq, d_model)
```

```python
import functools

import jax
import jax.numpy as jnp
from jax import lax
from jax.experimental import pallas as pl
from jax.experimental.pallas import tpu as pltpu

F32 = jnp.float32
BF16 = jnp.bfloat16

NORM_EPS = 1e-6
GN_EPS = 64e-5
LRU_C = 8.0
HEAD_DIM = 64
HEAD_SHIFT = 6
LANES = 128
SUBLANES = 8
CHUNK = 64
TOKEN_TILE = 512
FOX_BLOCK = 512
RWKV_BLOCK = 512
IN_PROJ_SUBTILES = 2
RWKV_CUM_ROWS = 256
CONV_WIDTH = 4
N_GROUPS = 4
EXPERTS_PER_GROUP = 4
N_EXPERTS = N_GROUPS * EXPERTS_PER_GROUP
MOE_BLOCK = 128
NEG_BIG = -1e30
LOG2E = 1.4426950408889634
FOX_BIAS_PIECES = 3
FOX_DEN_ROWS = 16
VMEM_LIMIT = 56 * 1024 * 1024


def _dot(a, b):
    return jnp.dot(a, b, preferred_element_type=F32)


def _dot_nt(a, b):
    return lax.dot_general(a, b, (((1,), (1,)), ((), ())), preferred_element_type=F32)


def _dot_tn(a, b):
    return lax.dot_general(a, b, (((0,), (0,)), ((), ())), preferred_element_type=F32)


def _head_sum(x, ones):
    return _dot(x.astype(BF16), ones)


def _split_dot_left(ones, x, parts):
    acc = None
    rem = x
    for _ in range(parts):
        hi = rem.astype(BF16)
        t = _dot(ones, hi)
        acc = t if acc is None else acc + t
        rem = rem - hi.astype(F32)
    return acc


def _head_ones(n):
    r = lax.broadcasted_iota(jnp.int32, (n, n), 0) >> HEAD_SHIFT
    c = lax.broadcasted_iota(jnp.int32, (n, n), 1) >> HEAD_SHIFT
    return jnp.where(r == c, 1.0, 0.0).astype(BF16)


def _softplus(z):
    return jnp.maximum(z, 0.0) + jnp.log(1.0 + jnp.exp(-jnp.abs(z)))


def _rms(x, eps=NORM_EPS):
    return x * lax.rsqrt(jnp.mean(x * x, axis=-1, keepdims=True) + eps)


def _resident(shape):
    zeros = (0,) * len(shape)
    return pl.BlockSpec(shape, lambda *_: zeros, pipeline_mode=pl.Buffered(1))


def _in_proj_kernel(x_ref, g_ref, w_ref, qg_ref, kg_ref, fb_ref,
                    cw_ref, cb_ref, wg_ref, bg_ref, lam_ref, ng_ref,
                    ya_ref, q_ref, k_ref, v_ref, p_ref,
                    carry_ref, tail_ref, hst_ref, buf_ref,
                    *, tiles_per_seq, seg, sub):
    i = pl.program_id(0)

    @pl.when(i % tiles_per_seq == 0)
    def _():
        carry_ref[...] = jnp.zeros_like(carry_ref)
        tail_ref[...] = jnp.zeros_like(tail_ref)
        hst_ref[...] = jnp.zeros_like(hst_ref)

    fox = seg[2] - seg[1]
    n_heads = fox // HEAD_DIM
    ones = _head_ones(fox)
    inv_d = 1.0 / HEAD_DIM
    rr = lax.broadcasted_iota(jnp.int32, (sub, sub), 0)
    cc = lax.broadcasted_iota(jnp.int32, (sub, sub), 1)
    tri = jnp.where(cc <= rr, 1.0, 0.0).astype(BF16)
    lane = lax.broadcasted_iota(jnp.int32, (sub, LANES), 1)
    feat = lane < HEAD_DIM
    bias_lanes = (lane >= HEAD_DIM) & (lane < HEAD_DIM + FOX_BIAS_PIECES)
    q_bias = jnp.where(bias_lanes, 1.0, 0.0)
    piece_id = lax.rem(lane, FOX_BIAS_PIECES)
    carry = carry_ref[0:1, :]
    tail = tail_ref[...]
    h_lru = hst_ref[0:1, :]
    lru_w = (seg[1] - seg[0]) // 2

    def head_norm(t):
        return t * lax.rsqrt(_head_sum(t * t, ones) * inv_d + NORM_EPS)

    sub_rows = [slice(r0, r0 + sub) for r0 in range(0, x_ref.shape[0], sub)]
    state = {"carry": carry, "tail": tail, "h": h_lru}

    def project_steps(rows, out):
        hb = []

        def first():
            hb.append((_rms(x_ref[rows, :]) * g_ref[...]).astype(BF16))
            out[0] = _dot(hb[0], w_ref[:, seg[0]:seg[1]])

        def segment(n):
            out[n] = _dot(hb[0], w_ref[:, seg[n]:seg[n + 1]])

        def rwkv_cols():
            p_ref[rows, :] = _dot(hb[0], w_ref[:, seg[4]:seg[5]])

        return [first, functools.partial(segment, 5), functools.partial(segment, 1),
                functools.partial(segment, 2), functools.partial(segment, 3), rwkv_cols]

    def finish_steps(idx, rows, out):
        local = {}

        def lru():
            u = out[0]
            ya_ref[rows, :], state["tail"], state["h"] = _lru_block(
                u[:, 0:lru_w], u[:, lru_w:2 * lru_w], state["tail"], state["h"], buf_ref.at[idx],
                cw_ref, cb_ref, wg_ref, bg_ref, lam_ref, ng_ref)

        def forget():
            lf = -_softplus(-(out[5] + fb_ref[...]))
            c = _split_dot_left(tri, lf, 3) + state["carry"]
            state["carry"] = c[sub - 1:sub, :]
            rem = c * (-LOG2E)
            pieces = None
            for piece in range(FOX_BIAS_PIECES):
                hi = rem.astype(BF16).astype(F32)
                pieces = hi if pieces is None else jnp.where(piece_id == piece, hi, pieces)
                rem = rem - hi
            local["pieces"] = pieces

        def spread(t, h):
            th = t[:, (h // 2) * LANES:(h // 2 + 1) * LANES]
            return pltpu.roll(th, HEAD_DIM, axis=1) if h % 2 else th

        def queries():
            q = head_norm(out[1]) * qg_ref[...]
            for h in range(n_heads):
                q_ref[h * LANES:(h + 1) * LANES, rows] = (
                    jnp.where(feat, spread(q, h), q_bias).T.astype(BF16))

        def keys():
            k = head_norm(out[2]) * kg_ref[...]
            for h in range(n_heads):
                k_bias = pltpu.roll(local["pieces"], HEAD_DIM - FOX_BIAS_PIECES * h, axis=1)
                k_ref[rows, h * LANES:(h + 1) * LANES] = jnp.where(
                    feat, spread(k, h), jnp.where(bias_lanes, k_bias, 0.0)).astype(BF16)

        def values():
            v_ref[:, rows] = out[3].T.astype(BF16)

        return [lru, forget, queries, keys, values]

    outs = [dict() for _ in sub_rows]
    for step in project_steps(sub_rows[0], outs[0]):
        step()
    for idx, rows in enumerate(sub_rows):
        ahead = project_steps(sub_rows[idx + 1], outs[idx + 1]) if idx + 1 < len(sub_rows) else []
        finish = finish_steps(idx, rows, outs[idx])
        for n in range(max(len(ahead), len(finish))):
            if n < len(ahead):
                ahead[n]()
            if n < len(finish):
                finish[n]()

    carry_ref[...] = jnp.broadcast_to(state["carry"], carry_ref.shape)
    tail_ref[...] = state["tail"]
    hst_ref[...] = jnp.broadcast_to(state["h"], hst_ref.shape)


def _in_proj(x2, g, w, qg, kg, fb, lru_params, *, seq, tm, seg):
    T, D = x2.shape
    n_out = w.shape[1]
    widths = [seg[j + 1] - seg[j] for j in range(6)]
    row = lambda i: (i, 0)
    col = lambda i: (0, i)
    outs = [
        ((T, widths[0] // 2), (tm, widths[0] // 2), row, BF16),
        ((2 * widths[1], T), (2 * widths[1], tm), col, BF16),
        ((T, 2 * widths[2]), (tm, 2 * widths[2]), row, BF16),
        ((widths[3], T), (widths[3], tm), col, BF16),
        ((T, widths[4]), (tm, widths[4]), row, F32),
    ]
    return pl.pallas_call(
        functools.partial(_in_proj_kernel, tiles_per_seq=seq // tm, seg=seg,
                          sub=tm // IN_PROJ_SUBTILES),
        out_shape=tuple(jax.ShapeDtypeStruct(shape, dt) for shape, _, _, dt in outs),
        grid=(T // tm,),
        in_specs=[
            pl.BlockSpec((tm, D), row),
            _resident((1, D)),
            _resident((D, n_out)),
            _resident((1, widths[1])),
            _resident((1, widths[2])),
            _resident((1, widths[5])),
        ] + [_resident(t.shape) for t in lru_params],
        out_specs=tuple(pl.BlockSpec(blk, imap) for _, blk, imap, _ in outs),
        scratch_shapes=[pltpu.VMEM((SUBLANES, widths[5]), F32),
                        pltpu.VMEM((SUBLANES, widths[0] // 2), F32),
                        pltpu.VMEM((SUBLANES, widths[0] // 2), F32),
                        pltpu.VMEM((IN_PROJ_SUBTILES, tm // IN_PROJ_SUBTILES + SUBLANES,
                                    widths[0] // 2), F32)],
        compiler_params=pltpu.CompilerParams(
            dimension_semantics=("arbitrary",), vmem_limit_bytes=VMEM_LIMIT),
        name="in_proj",
    )(x2, g, w, qg, kg, fb, *lru_params)


def _shift_rows(x, d, fill, row):
    return jnp.where(row >= d, pltpu.roll(x, d, axis=0), fill)


def _lru_block(xa, ga, tail, h_prev, buf_ref, cw_ref, cb_ref, wg_ref, bg_ref, lam_ref, ng_ref):
    tb, w = xa.shape
    pad = SUBLANES
    buf_ref[0:pad, :] = tail
    buf_ref[pad:pad + tb, :] = xa
    xc = cb_ref[...] + cw_ref[CONV_WIDTH - 1:CONV_WIDTH, :] * xa
    for d in range(1, CONV_WIDTH):
        xc = xc + cw_ref[CONV_WIDTH - 1 - d:CONV_WIDTH - d, :] * buf_ref[pad - d:pad - d + tb, :]

    gates = _dot(xc.astype(BF16), wg_ref[...]) + bg_ref[...]
    r = jax.nn.sigmoid(gates[:, 0:w])
    i = jax.nn.sigmoid(gates[:, w:2 * w])
    log_a = (-LRU_C) * r * _softplus(-lam_ref[...])
    a = jnp.exp(log_a)
    b = jnp.sqrt(1.0 - a * a) * (i * xc)

    row = lax.broadcasted_iota(jnp.int32, (tb, w), 0)
    d = 1
    while d < tb:
        a_sh = _shift_rows(a, d, 1.0, row)
        b_sh = _shift_rows(b, d, 0.0, row)
        b = a * b_sh + b
        a = a * a_sh
        d *= 2
    h = b + a * h_prev
    y = jax.nn.gelu(ga) * h
    return (_rms(y) * ng_ref[...]).astype(BF16), xa[tb - pad:tb, :], h[tb - 1:tb, :]


def _fox_kernel(q_ref, k_ref, v_ref, o_ref, s_buf, mx_buf, p_buf, m_ref, acc_ref, fin_ref, *, tq):
    n_heads = k_ref.shape[1] // LANES
    nq = k_ref.shape[0] // tq
    half = tq // 2
    causal_early = (lax.broadcasted_iota(jnp.int32, (half, half), 0)
                    <= lax.broadcasted_iota(jnp.int32, (half, half), 1))
    causal_late = (lax.broadcasted_iota(jnp.int32, (tq, half), 0)
                   <= lax.broadcasted_iota(jnp.int32, (tq, half), 1) + half)
    items = [(i, j) for i in range(nq) for j in range(i + 1)]
    units = [(u, hf) for u in range(n_heads) for hf in range(2)]

    def keys_of(t, hf):
        i, j = items[t]
        return slice(0, half) if (i == j and hf == 0) else slice(0, tq)

    def scores(t, u, hf):
        i, j = items[t]
        slot = t % 2
        hl = slice(u * LANES, (u + 1) * LANES)
        kr = keys_of(t, hf)
        qc = slice(hf * half, (hf + 1) * half)
        s = _dot(k_ref[j * tq + kr.start:j * tq + kr.stop, hl],
                 q_ref[hl, i * tq + qc.start:i * tq + qc.stop])
        if i == j:
            s = jnp.where(causal_early if hf == 0 else causal_late, s, NEG_BIG)
        s_buf[slot, u, kr, qc] = s
        mx_buf[slot, u, :, qc] = jnp.max(s, axis=0, keepdims=True)

    def softmax_numerators(t, u, hf):
        _, j = items[t]
        slot = t % 2
        kr = keys_of(t, hf)
        qc = slice(hf * half, (hf + 1) * half)
        mx = mx_buf[slot, u, :, qc]
        m_new = mx if j == 0 else jnp.maximum(m_ref[u, :, qc], mx)
        p_buf[u, kr, qc] = jnp.exp2(s_buf[slot, u, kr, qc] - m_new).astype(BF16)
        return m_new

    def accumulate(t, u, hf, m_new):
        i, j = items[t]
        kr = keys_of(t, hf)
        qc = slice(hf * half, (hf + 1) * half)
        vt = jnp.concatenate(
            [v_ref[u * HEAD_DIM:(u + 1) * HEAD_DIM, j * tq + kr.start:j * tq + kr.stop],
             jnp.ones((FOX_DEN_ROWS, kr.stop - kr.start), BF16)], axis=0)
        acc = _dot(vt, p_buf[u, kr, qc])
        if j > 0:
            acc = jnp.exp2(m_ref[u, :, qc] - m_new) * acc_ref[u, :, qc] + acc
        if j < i:
            m_ref[u, :, qc] = m_new
            acc_ref[u, :, qc] = acc
        else:
            fin_ref[u * HEAD_DIM:(u + 1) * HEAD_DIM, qc] = (
                acc[0:HEAD_DIM] / acc[HEAD_DIM:HEAD_DIM + 1])
            if (u, hf) == units[-1]:
                o_ref[i * tq:(i + 1) * tq, :] = fin_ref[...].T.astype(BF16)

    for u, hf in units:
        scores(0, u, hf)
    for t in range(len(items)):
        for u, hf in units:
            m_new = softmax_numerators(t, u, hf)
            if t + 1 < len(items):
                scores(t + 1, u, hf)
            accumulate(t, u, hf, m_new)


def _fox(q_aug, k_aug, v_t, *, batch, seq, tq):
    T, wide = k_aug.shape
    heads_per_step = 2
    blk = heads_per_step * LANES
    out_w = heads_per_step * HEAD_DIM
    steps = wide // blk
    return pl.pallas_call(
        functools.partial(_fox_kernel, tq=tq),
        out_shape=jax.ShapeDtypeStruct((T, steps * out_w), BF16),
        grid=(batch, steps),
        in_specs=[
            pl.BlockSpec((blk, seq), lambda b, h: (h, b)),
            pl.BlockSpec((seq, blk), lambda b, h: (b, h)),
            pl.BlockSpec((out_w, seq), lambda b, h: (h, b)),
        ],
        out_specs=pl.BlockSpec((seq, out_w), lambda b, h: (b, h)),
        scratch_shapes=[
            pltpu.VMEM((2, heads_per_step, tq, tq), F32),
            pltpu.VMEM((2, heads_per_step, 1, tq), F32),
            pltpu.VMEM((heads_per_step, tq, tq), BF16),
            pltpu.VMEM((heads_per_step, 1, tq), F32),
            pltpu.VMEM((heads_per_step, HEAD_DIM + FOX_DEN_ROWS, tq), F32),
            pltpu.VMEM((out_w, tq), F32),
        ],
        compiler_params=pltpu.CompilerParams(
            dimension_semantics=("parallel", "parallel"), vmem_limit_bytes=VMEM_LIMIT),
        name="fox",
    )(q_aug, k_aug, v_t)


def _stack_heads(x, lane_head, n_heads):
    zero = jnp.zeros_like(x)
    return jnp.concatenate([jnp.where(lane_head == h, x, zero) for h in range(n_heads)], axis=0)


def _rwkv_kernel(p_ref, mu_ref, w0_ref, w2_ref, a0_ref, a2_ref, g2_ref, kk_ref, ka_ref,
                 rk_ref, lng_ref, lnb_ref, y_ref, prev_ref, st_ref):
    j = pl.program_id(1)
    tb, cols = p_ref.shape
    w = y_ref.shape[1]
    n_heads = w // HEAD_DIM
    C = CHUNK
    chunk_shift = CHUNK.bit_length() - 1

    @pl.when(j == 0)
    def _():
        prev_ref[...] = jnp.zeros_like(prev_ref)
        st_ref[...] = jnp.zeros_like(st_ref)

    p = p_ref[...]
    row = lax.broadcasted_iota(jnp.int32, (tb, cols), 0)
    prev = jnp.where(row == 0, prev_ref[0:1, :], pltpu.roll(p, 1, axis=0))
    prev_ref[...] = jnp.broadcast_to(p[tb - 1:tb, :], prev_ref.shape)
    ps = p + (prev - p) * mu_ref[...]
    r = ps[:, 0:w]
    k = ps[:, w:2 * w]
    v = ps[:, 2 * w:3 * w]
    lr = ps[:, 3 * w:cols]

    wpre = w0_ref[...] + _dot(jnp.tanh(lr).astype(BF16), w2_ref[...])
    lw = -jnp.exp(-_softplus(-wpre) - 0.5)
    a = jax.nn.sigmoid(a0_ref[...] + _dot(lr.astype(BF16), a2_ref[...]))
    g = _dot(jax.nn.sigmoid(lr).astype(BF16), g2_ref[...])

    ones = _head_ones(w)
    kk = k * kk_ref[...]
    kk = kk * lax.rsqrt(jnp.maximum(_head_sum(kk * kk, ones), 1e-24))
    k2 = k * (1.0 + (a - 1.0) * ka_ref[...])
    b = kk * a
    bonus = _head_sum(r * k2 * rk_ref[...], ones) * v

    span = min(RWKV_CUM_ROWS, tb)
    rr = lax.broadcasted_iota(jnp.int32, (span, span), 0)
    cc = lax.broadcasted_iota(jnp.int32, (span, span), 1)
    tri = jnp.where(((rr >> chunk_shift) == (cc >> chunk_shift)) & (cc <= rr), 1.0, 0.0).astype(BF16)
    LW = jnp.concatenate(
        [_split_dot_left(tri, lw[r0:r0 + span], 2) for r0 in range(0, tb, span)], axis=0)
    LT = jnp.concatenate(
        [jnp.broadcast_to(LW[c * C + C - 1:c * C + C, :], (C, w)) for c in range(tb // C)], axis=0)
    e_out = jnp.exp(-LW)
    e_end = jnp.exp(LT - LW)
    at = (-kk) * jnp.exp(LW - lw)
    rt = r * jnp.exp(LW)
    bt = b * e_out
    kt = k2 * e_out
    b_end = b * e_end
    k_end = k2 * e_end
    w_tot = jnp.exp(LT)

    lane_head = lax.broadcasted_iota(jnp.int32, (C, w), 1) >> HEAD_SHIFT
    t_idx = lax.broadcasted_iota(jnp.int32, (C, w), 0)
    j_idx = lax.broadcasted_iota(jnp.int32, (C, w), 1) & (HEAD_DIM - 1)
    strict = j_idx < t_idx
    incl = j_idx <= t_idx
    eye = jnp.where(j_idx == t_idx, 1.0, 0.0)
    sq_r = lax.broadcasted_iota(jnp.int32, (w, w), 0) >> HEAD_SHIFT
    sq_c = lax.broadcasted_iota(jnp.int32, (w, w), 1) >> HEAD_SHIFT
    block_diag = sq_r == sq_c
    stack = functools.partial(_stack_heads, lane_head=lane_head, n_heads=n_heads)

    chunks = range(tb // C)
    rows_of = [slice(c * C, (c + 1) * C) for c in chunks]
    bf = lambda t: t.astype(BF16)
    cat0 = lambda *ts: jnp.concatenate(ts, axis=0)
    cat1 = lambda *ts: jnp.concatenate(ts, axis=1)

    A = [_dot_nt(bf(cat0(at[sl], rt[sl])), cat0(stack(bf(bt[sl])), stack(bf(kt[sl]))))
         for sl in rows_of]
    a_ab = [jnp.where(strict, a[0:C, 0:w], 0.0) for a in A]
    a_ak = [jnp.where(strict, a[0:C, w:2 * w], 0.0) for a in A]
    a_rb = [jnp.where(incl, a[C:2 * C, 0:w], 0.0) for a in A]
    a_rk = [jnp.where(incl, a[C:2 * C, w:2 * w], 0.0) for a in A]

    pw = [_dot(bf(l), stack(bf(l))) for l in a_ab]
    tinv = [eye + l for l in a_ab]
    sz = 2
    while 2 * sz < C:
        both = [_dot(bf(cat0(t, p)), stack(bf(p))) for t, p in zip(tinv, pw)]
        tinv = [t + r[0:C] for t, r in zip(tinv, both)]
        pw = [r[C:2 * C] for r in both]
        sz *= 2
    tinv = [t + _dot(bf(t), stack(bf(p))) for t, p in zip(tinv, pw)]

    yo = [_dot(bf(cat0(ak, ark)), stack(bf(v[sl]))) for ak, ark, sl in zip(a_ak, a_rk, rows_of)]
    gu = [_dot(bf(t), cat1(stack(bf(at[sl])), stack(bf(y[0:C]))))
          for t, y, sl in zip(tinv, yo, rows_of)]
    pq = [_dot(bf(rb), cat1(stack(bf(x[:, 0:w])), stack(bf(x[:, w:2 * w]))))
          for rb, x in zip(a_rb, gu)]
    p_mat = [rt[sl] + x[:, 0:w] for sl, x in zip(rows_of, pq)]
    q_mat = [y[C:2 * C] + x[:, w:2 * w] for y, x in zip(yo, pq)]
    m_mat = [jnp.where(block_diag, _dot_tn(bf(x[:, 0:w]), bf(b_end[sl])), 0.0)
             for x, sl in zip(gu, rows_of)]
    n_mat = [jnp.where(block_diag,
                       _dot_tn(bf(cat0(x[:, w:2 * w], v[sl])), bf(cat0(b_end[sl], k_end[sl]))), 0.0)
             for x, sl in zip(gu, rows_of)]

    st = st_ref[...]
    outs = []
    for c in chunks:
        stb = bf(st)
        outs.append(_dot_nt(bf(p_mat[c]), stb) + q_mat[c])
        st = st * w_tot[c * C:c * C + 1, :] + _dot(stb, bf(m_mat[c])) + n_mat[c]
    st_ref[...] = st

    o = jnp.concatenate(outs, axis=0)
    inv_d = 1.0 / HEAD_DIM
    mean = _head_sum(o, ones) * inv_d
    cen = o - mean
    var = _head_sum(cen * cen, ones) * inv_d
    o = cen * lax.rsqrt(var + GN_EPS) * lng_ref[...] + lnb_ref[...]
    y_ref[...] = ((o + bonus) * g).astype(BF16)


def _rwkv(p, mu, w0, w2, a0, a2, g2, kkp, ka, rk, lng, lnb, *, batch, seq, tb):
    T, cols = p.shape
    w = w0.shape[1]
    nb = seq // tb
    vec = _resident((1, w))
    low = _resident((cols - 3 * w, w))
    return pl.pallas_call(
        _rwkv_kernel,
        out_shape=jax.ShapeDtypeStruct((T, w), BF16),
        grid=(batch, nb),
        in_specs=[
            pl.BlockSpec((tb, cols), lambda b, j: (b * nb + j, 0)),
            _resident((1, cols)), vec, low, vec, low, low, vec, vec, vec, vec, vec,
        ],
        out_specs=pl.BlockSpec((tb, w), lambda b, j: (b * nb + j, 0)),
        scratch_shapes=[pltpu.VMEM((SUBLANES, cols), F32), pltpu.VMEM((w, w), F32)],
        compiler_params=pltpu.CompilerParams(
            dimension_semantics=("parallel", "arbitrary"), vmem_limit_bytes=VMEM_LIMIT),
        name="rwkv",
    )(p, mu, w0, w2, a0, a2, g2, kkp, ka, rk, lng, lnb)


def _out_proj_kernel(ya_ref, yb_ref, yc_ref, x_ref, wa_ref, wb_ref, wc_ref, fg_ref, n2_ref,
                     wr_ref, br_ref, x1_ref, h2_ref, comb_ref, *, sub):
    sub_rows = [slice(r0, r0 + sub) for r0 in range(0, x_ref.shape[0], sub)]
    mixed = []
    for rows in sub_rows:
        ybn = (_rms(yb_ref[rows, :].astype(F32)) * fg_ref[...]).astype(BF16)
        mixed.append(_dot(ya_ref[rows, :], wa_ref[...]) + _dot(ybn, wb_ref[...])
                     + _dot(yc_ref[rows, :], wc_ref[...]))
    for rows, y in zip(sub_rows, mixed):
        x1 = x_ref[rows, :] + y
        x1_ref[rows, :] = x1
        h2 = _rms(x1) * n2_ref[...]
        h_hi = h2.astype(BF16)
        h2_ref[rows, :] = h_hi

        h_lo = (h2 - h_hi.astype(F32)).astype(BF16)
        hw = _dot(h_hi, wr_ref[...])
        logits = (hw[:, 0:LANES] + hw[:, LANES:2 * LANES] + _dot(h_lo, wr_ref[:, 0:LANES])
                  + br_ref[...])
        comb_ref[rows, :] = _route(logits)


def _route(logits):
    lane = lax.broadcasted_iota(jnp.int32, logits.shape, 1)
    lane_f = lane.astype(F32)
    far = float(LANES)

    def first_argmax(vals, vmax):
        return jnp.min(jnp.where(vals == vmax, lane_f, far), axis=-1, keepdims=True)

    gl = jnp.where(lane < N_GROUPS, logits, NEG_BIG)
    gmax = jnp.max(gl, axis=-1, keepdims=True)
    gidx = first_argmax(gl, gmax)
    g_p = 1.0 / jnp.sum(jnp.exp(gl - gmax), axis=-1, keepdims=True)

    e_group = ((lane - N_GROUPS) >> 2).astype(F32)
    el = jnp.where((lane >= N_GROUPS) & (e_group == gidx), logits, NEG_BIG)
    emax = jnp.max(el, axis=-1, keepdims=True)
    esum = jnp.sum(jnp.exp(el - emax), axis=-1, keepdims=True)
    i1 = first_argmax(el, emax)
    el2 = jnp.where(lane_f == i1, NEG_BIG, el)
    emax2 = jnp.max(el2, axis=-1, keepdims=True)
    i2 = first_argmax(el2, emax2)
    p1 = 1.0 / esum
    p2 = jnp.exp(emax2 - emax) / esum
    den = p1 + p2
    return (jnp.where(lane_f == i1, g_p * (p1 / den), 0.0)
            + jnp.where(lane_f == i2, g_p * (p2 / den), 0.0)
            + jnp.where(lane_f == gidx, 1.0, 0.0))


def _out_proj(ya, yb, yc, x2, wa, wb, wc, fg, n2, wr, br, *, tm):
    T, D = x2.shape
    row = lambda i: (i, 0)
    return pl.pallas_call(
        functools.partial(_out_proj_kernel, sub=tm // 2),
        out_shape=(jax.ShapeDtypeStruct((T, D), F32), jax.ShapeDtypeStruct((T, D), BF16),
                   jax.ShapeDtypeStruct((T, LANES), F32)),
        grid=(T // tm,),
        in_specs=[
            pl.BlockSpec((tm, ya.shape[1]), row),
            pl.BlockSpec((tm, yb.shape[1]), row),
            pl.BlockSpec((tm, yc.shape[1]), row),
            pl.BlockSpec((tm, D), row),
            _resident(wa.shape), _resident(wb.shape), _resident(wc.shape),
            _resident(fg.shape), _resident(n2.shape), _resident(wr.shape), _resident(br.shape),
        ],
        out_specs=(pl.BlockSpec((tm, D), row), pl.BlockSpec((tm, D), row),
                   pl.BlockSpec((tm, LANES), row)),
        compiler_params=pltpu.CompilerParams(
            dimension_semantics=("parallel",), vmem_limit_bytes=VMEM_LIMIT),
        name="out_proj",
    )(ya, yb, yc, x2, wa, wb, wc, fg, n2, wr, br)


def _moe_kernel(h_ref, route_ref, x_ref, wg_ref, wu_ref, wd_ref, o_ref, hs_ref, rs_ref, ys_ref):
    tm = h_ref.shape[0]
    n_groups = wd_ref.shape[0]
    route = route_ref[...]
    lane = lax.broadcasted_iota(jnp.int32, (tm, LANES), 1)
    onehot = jnp.where(lane < n_groups, route, 0.0)
    rr = lax.broadcasted_iota(jnp.int32, (tm, tm), 0)
    cc = lax.broadcasted_iota(jnp.int32, (tm, tm), 1)
    tri = jnp.where(cc <= rr, 1.0, 0.0).astype(BF16)
    cum = _dot(tri, onehot.astype(BF16))
    tot = cum[tm - 1:tm, :]
    start = jnp.zeros_like(tot)
    for d in range(1, n_groups):
        start = start + jnp.where(lane[0:1] >= d, pltpu.roll(tot, d, axis=1), 0.0)
    end = start + tot
    pos = jnp.sum(jnp.where(onehot > 0.0, start + cum - 1.0, 0.0), axis=-1, keepdims=True)
    perm_t = jnp.where(pos == cc.astype(F32), 1.0, 0.0).astype(BF16)
    pos_row = jnp.broadcast_to(pos, (tm, LANES)).T[0:1, :]
    perm = jnp.where(rr.astype(F32) == pos_row, 1.0, 0.0).astype(BF16)

    hs_ref[...] = _dot(perm, h_ref[...]).astype(BF16)
    rem = route
    sorted_route = None
    for _ in range(2):
        hi = rem.astype(BF16)
        t = _dot(perm, hi)
        sorted_route = t if sorted_route is None else sorted_route + t
        rem = rem - hi.astype(F32)
    rs_ref[...] = sorted_route
    ys_ref[...] = jnp.zeros_like(ys_ref)

    for g in range(n_groups):
        g_start = start[0, g]
        g_end = end[0, g]
        for r0 in range(0, tm, MOE_BLOCK):

            @pl.when((g_start < r0 + MOE_BLOCK) & (g_end > r0))
            def _(g=g, r0=r0):
                rows = slice(r0, r0 + MOE_BLOCK)
                hb = hs_ref[rows, :]
                rw = rs_ref[rows, :]
                hids = []
                for e in range(g * EXPERTS_PER_GROUP, (g + 1) * EXPERTS_PER_GROUP):
                    gate = _dot(hb, wg_ref[e])
                    hid = gate * jax.nn.sigmoid(gate) * _dot(hb, wu_ref[e])
                    hids.append((hid * rw[:, N_GROUPS + e:N_GROUPS + e + 1]).astype(BF16))
                ys_ref[rows, :] += _dot(jnp.concatenate(hids, axis=1), wd_ref[g])

    o_ref[...] = x_ref[...] + _dot(perm_t, ys_ref[...].astype(BF16))


def _moe(h2, route, x1, wg, wu, wd, *, tm):
    T, D = x1.shape
    row = lambda i: (i, 0)
    return pl.pallas_call(
        _moe_kernel,
        out_shape=jax.ShapeDtypeStruct((T, D), F32),
        grid=(T // tm,),
        in_specs=[
            pl.BlockSpec((tm, D), row),
            pl.BlockSpec((tm, LANES), row),
            pl.BlockSpec((tm, D), row),
            _resident(wg.shape),
            _resident(wu.shape),
            _resident(wd.shape),
        ],
        out_specs=pl.BlockSpec((tm, D), row),
        scratch_shapes=[pltpu.VMEM((tm, D), BF16), pltpu.VMEM((tm, LANES), F32),
                        pltpu.VMEM((tm, D), F32)],
        compiler_params=pltpu.CompilerParams(
            dimension_semantics=("parallel",), vmem_limit_bytes=VMEM_LIMIT),
        name="moe",
    )(h2, route, x1, wg, wu, wd)


def _block_diag(wh):
    n_heads, d, _ = wh.shape
    eye = jnp.eye(n_heads, dtype=wh.dtype)
    return jnp.einsum("hij,hg->higj", wh, eye).reshape(n_heads * d, n_heads * d)


def _row(vec):
    return vec.reshape(1, -1).astype(F32)


def _pad_rows(mat, start, total):
    return jnp.zeros((total, mat.shape[1]), mat.dtype).at[start:start + mat.shape[0]].set(mat)


def kernel(x, norm1_g, w_in, conv_w, conv_b, lru_wa, lru_ba, lru_wx, lru_bx, lru_lambda,
           lru_norm_g, fox_fb, fox_qnorm_g, fox_knorm_g, fox_norm_g, rwkv_mu, rwkv_w0,
           rwkv_w2, rwkv_a0, rwkv_a2, rwkv_g2, rwkv_kk, rwkv_ka, rwkv_rk, rwkv_ln_g,
           rwkv_ln_b, w_out, norm2_g, router_gw, router_gb, router_ew, router_eb,
           exp_w_gate, exp_w_up, exp_w_down):
    batch, seq, d_model = x.shape
    depth = w_in.shape[0]
    lru_w = conv_w.shape[2]
    fox_heads = fox_fb.shape[1]
    fox_w = fox_heads * HEAD_DIM
    rwkv_w = rwkv_w0.shape[1]
    rwkv_cols = rwkv_mu.shape[1]
    d_rank, a_rank, g_rank = rwkv_w2.shape[1], rwkv_a2.shape[1], rwkv_g2.shape[1]
    low = d_rank + a_rank + g_rank
    n_exp, _, d_exp = exp_w_gate.shape[1:]
    assert low == LANES and rwkv_cols == 3 * rwkv_w + low
    assert FOX_BIAS_PIECES * fox_heads <= LANES and n_exp == N_EXPERTS
    assert seq % TOKEN_TILE == 0 and seq % FOX_BLOCK == 0 and seq % RWKV_BLOCK == 0
    assert RWKV_BLOCK % RWKV_CUM_ROWS == 0 and RWKV_CUM_ROWS % CHUNK == 0
    assert (TOKEN_TILE // IN_PROJ_SUBTILES) % SUBLANES == 0 and TOKEN_TILE % MOE_BLOCK == 0

    o_fox = 2 * lru_w
    o_fl = o_fox + 3 * fox_w
    o_rwkv = o_fl + fox_heads
    seg = (0, o_fox, o_fox + fox_w, o_fox + 2 * fox_w, o_fl, o_fl + rwkv_cols,
           o_fl + rwkv_cols + LANES)

    x2 = x.reshape(batch * seq, d_model)
    for l in range(depth):
        wl = w_in[l]
        n_fl = FOX_BIAS_PIECES * fox_heads
        w_perm = jnp.concatenate(
            [wl[:, 0:o_fl], wl[:, o_rwkv:o_rwkv + rwkv_cols],
             jnp.repeat(wl[:, o_fl:o_rwkv], FOX_BIAS_PIECES, axis=1),
             jnp.zeros((d_model, LANES - n_fl), wl.dtype)], axis=1).astype(BF16)
        qg = _row(jnp.tile(fox_qnorm_g[l], fox_heads) * (HEAD_DIM ** -0.5 * LOG2E))
        kg = _row(jnp.tile(fox_knorm_g[l], fox_heads))
        fb = _row(jnp.pad(jnp.repeat(fox_fb[l], FOX_BIAS_PIECES), (0, LANES - n_fl)))
        wg = jnp.concatenate([_block_diag(lru_wa[l]), _block_diag(lru_wx[l])], axis=1).astype(BF16)
        bg = _row(jnp.concatenate([lru_ba[l], lru_bx[l]]))
        lru_params = (conv_w[l], _row(conv_b[l]), wg, bg, _row(lru_lambda[l]), _row(lru_norm_g[l]))
        ya, q, k_aug, v, p = _in_proj(x2, _row(norm1_g[l]), w_perm, qg, kg, fb, lru_params,
                                     seq=seq, tm=TOKEN_TILE, seg=seg)

        yb = _fox(q, k_aug, v, batch=batch, seq=seq, tq=FOX_BLOCK)

        yc = _rwkv(p, _row(rwkv_mu[l]), _row(rwkv_w0[l]),
                   _pad_rows(rwkv_w2[l], 0, low).astype(BF16), _row(rwkv_a0[l]),
                   _pad_rows(rwkv_a2[l], d_rank, low).astype(BF16),
                   _pad_rows(rwkv_g2[l], d_rank + a_rank, low).astype(BF16),
                   _row(rwkv_kk[l]), _row(rwkv_ka[l]), _row(rwkv_rk[l]),
                   _row(rwkv_ln_g[l]), _row(rwkv_ln_b[l]), batch=batch, seq=seq, tb=RWKV_BLOCK)

        wo = w_out[l].astype(BF16)
        wr = jnp.concatenate(
            [router_gw[l], router_ew[l],
             jnp.zeros((d_model, LANES - N_GROUPS - n_exp), F32)], axis=1)
        wr_hi = wr.astype(BF16)
        wr = jnp.concatenate([wr_hi, (wr - wr_hi.astype(F32)).astype(BF16)], axis=1)
        br = _row(jnp.pad(jnp.concatenate([router_gb[l], router_eb[l]]),
                          (0, LANES - N_GROUPS - n_exp)))
        x1, h2, comb = _out_proj(
            ya, yb, yc, x2, wo[0:lru_w], wo[lru_w:lru_w + fox_w], wo[lru_w + fox_w:],
            _row(fox_norm_g[l]), _row(norm2_g[l]), wr, br, tm=TOKEN_TILE)

        wd = exp_w_down[l].astype(BF16).reshape(N_GROUPS, EXPERTS_PER_GROUP * d_exp, d_model)
        x2 = _moe(h2, comb, x1, exp_w_gate[l].astype(BF16), exp_w_up[l].astype(BF16), wd,
                  tm=TOKEN_TILE)
    return x2.reshape(batch, seq, d_model)
```
